```python
import math
import jax
import jax.numpy as jnp
from jax import lax
import numpy as np

D_MODEL = 1024
BATCH = 8
SEQ = 4096
DEPTH = 2

F32 = jnp.float32
NORM_EPS = 1e-6
ROPE_THETA = 10000.0
HEAD_DIM = 64
ATTN_BLOCK = 128
A_HEADS = 8
A_KV_HEADS = 2
IDX_HEADS = 8
IDX_DIM = 64
A_TOPK_MAX = 256
B_WIDTH = 512
B_GROUP = 16
B_GROUPS = B_WIDTH // B_GROUP
B_STATE = 64
C_HEADS = 8
C_KV_HEADS = 2
CMP_LEN = 32
CMP_STRIDE = 16
CMP_HIDDEN = 256
SLC_LEN = 64
SLC_TOPN = 16
WIN = 512
C_BLOCK = 64
FORCE_SCORE = 1e9
D_HEADS = 4
D_KDIM = 64
D_VDIM = 128
GATE_RANK = 16
GATE_TAU = 16.0
GLA_CHUNK = 32
PEER_HEADS = 8
PEER_KEYS = 128
PEER_EXPERTS = PEER_KEYS * PEER_KEYS
PEER_QDIM = 256
PEER_TOPK = 16
PEER_BLOCK = 128

MIX_WIDTH = A_HEADS * HEAD_DIM + B_WIDTH
EVEN_COLS = (A_HEADS * HEAD_DIM, A_KV_HEADS * HEAD_DIM, A_KV_HEADS * HEAD_DIM,
             IDX_HEADS * IDX_DIM, IDX_DIM, IDX_HEADS, B_WIDTH)
ODD_COLS = (C_HEADS * HEAD_DIM,) + (C_KV_HEADS * HEAD_DIM,) * 6 + (
    C_HEADS * 3, D_HEADS * D_KDIM, D_HEADS * D_KDIM, D_HEADS * D_VDIM, GATE_RANK, D_HEADS * D_VDIM)

kernel_name = 'hybrid_dsa_s5_nsa_gla_peer'


def rms_norm(x, gain):
    xf = x.astype(F32)
    y = xf * lax.rsqrt(jnp.mean(xf * xf, axis=-1, keepdims=True) + NORM_EPS)
    return (y * gain.astype(F32)).astype(x.dtype)


def rope(x, pos):
    half = x.shape[-1] // 2
    inv_freq = ROPE_THETA ** (-jnp.arange(half, dtype=F32) / half)
    ang = pos.astype(F32)[:, None] * inv_freq[None, :]
    cos = jnp.cos(ang)[:, None, :]
    sin = jnp.sin(ang)[:, None, :]
    xf = x.astype(F32)
    x1, x2 = xf[..., :half], xf[..., half:]
    return jnp.concatenate([x1 * cos - x2 * sin, x1 * sin + x2 * cos], axis=-1).astype(x.dtype)


def masked_softmax(s, mask):
    s = jnp.where(mask, s.astype(F32), -1e30)
    m = jnp.max(s, axis=-1, keepdims=True)
    p = jnp.exp(s - m) * mask
    return p / jnp.maximum(jnp.sum(p, axis=-1, keepdims=True), 1e-30)


def split_cols(h, sizes):
    return jnp.split(h, np.cumsum(sizes)[:-1].tolist(), axis=-1)


def to_blocks(t, blk):
    return t.reshape(t.shape[0], t.shape[1] // blk, blk, *t.shape[2:]).swapaxes(0, 1)


def dsa_attention(q, k, v, iq, ik, iw, q_gain, k_gain, pos):
    bsz, seq = q.shape[:2]
    topk = min(A_TOPK_MAX, seq // 4)
    rep = A_HEADS // A_KV_HEADS
    q = rope(rms_norm(q, q_gain), pos).astype(F32) * HEAD_DIM ** -0.5
    k = rope(rms_norm(k, k_gain), pos).astype(F32)
    v = v.astype(F32)
    iq = rope(iq, pos).astype(F32) * IDX_DIM ** -0.5
    ik = rope(ik[:, :, None, :], pos)[:, :, 0, :].astype(F32)
    iw = iw.astype(F32) * IDX_HEADS ** -0.5
    key_pos = jnp.arange(seq)
    gather = jax.vmap(lambda t, idx: t[idx])

    def block(args):
        qb, iqb, iwb, start = args
        tq = start + jnp.arange(ATTN_BLOCK)
        rel = jax.nn.relu(jnp.einsum('bqhd,bsd->bqhs', iqb, ik))
        score = jnp.einsum('bqh,bqhs->bqs', iwb, rel)
        causal = key_pos[None, :] <= tq[:, None]
        score = jnp.where(causal[None], score, -jnp.inf)
        _, sel = lax.top_k(score, topk)
        valid = sel <= tq[None, :, None]
        k_sel = gather(k, sel)
        v_sel = gather(v, sel)
        qg = qb.reshape(bsz, ATTN_BLOCK, A_KV_HEADS, rep, HEAD_DIM)
        s = jnp.einsum('bqgrd,bqkgd->bqgrk', qg, k_sel)
        p = masked_softmax(s, valid[:, :, None, None, :])
        o = jnp.einsum('bqgrk,bqkgd->bqgrd', p, v_sel)
        return o.reshape(bsz, ATTN_BLOCK, A_HEADS * HEAD_DIM)

    starts = jnp.arange(seq // ATTN_BLOCK) * ATTN_BLOCK
    out = lax.map(block, (to_blocks(q, ATTN_BLOCK), to_blocks(iq, ATTN_BLOCK),
                          to_blocks(iw, ATTN_BLOCK), starts))
    return out.swapaxes(0, 1).reshape(bsz, seq, A_HEADS * HEAD_DIM)


def s5_ssm(u, log_dt, a_re, a_im, b_re, b_im, c_re, c_im, d_skip, w_glu, b_glu):
    bsz, seq = u.shape[:2]
    uf = u.astype(F32).reshape(bsz, seq, B_GROUPS, B_GROUP)
    dt = jnp.exp(log_dt.astype(F32))[:, None]
    lr, li = a_re.astype(F32), a_im.astype(F32)
    mag = jnp.exp(dt * lr)
    ab_re = mag * jnp.cos(dt * li)
    ab_im = mag * jnp.sin(dt * li)
    den = lr * lr + li * li
    f_re = ((ab_re - 1.0) * lr + ab_im * li) / den
    f_im = (ab_im * lr - (ab_re - 1.0) * li) / den
    br, bi = b_re.astype(F32), b_im.astype(F32)
    bb_re = f_re[..., None] * br - f_im[..., None] * bi
    bb_im = f_re[..., None] * bi + f_im[..., None] * br
    x_re = jnp.einsum('gpc,bsgc->bsgp', bb_re, uf)
    x_im = jnp.einsum('gpc,bsgc->bsgp', bb_im, uf)
    a_re_t = jnp.broadcast_to(ab_re, x_re.shape)
    a_im_t = jnp.broadcast_to(ab_im, x_im.shape)

    def combine(e1, e2):
        a1r, a1i, b1r, b1i = e1
        a2r, a2i, b2r, b2i = e2
        return (a1r * a2r - a1i * a2i, a1r * a2i + a1i * a2r,
                a2r * b1r - a2i * b1i + b2r, a2r * b1i + a2i * b1r + b2i)

    _, _, h_re, h_im = lax.associative_scan(combine, (a_re_t, a_im_t, x_re, x_im), axis=1)
    y = (jnp.einsum('gcp,bsgp->bsgc', c_re.astype(F32), h_re)
         - jnp.einsum('gcp,bsgp->bsgc', c_im.astype(F32), h_im)
         + d_skip.astype(F32) * uf)
    y = jax.nn.gelu(y.reshape(bsz, seq, B_WIDTH))
    return y * jax.nn.sigmoid(y @ w_glu.astype(F32) + b_glu.astype(F32))


def compress_blocks(t, cmp_pos, w1, w2):
    bsz, seq, groups, dh = t.shape
    n_cmp = (seq - CMP_LEN) // CMP_STRIDE + 1
    idx = jnp.arange(n_cmp)[:, None] * CMP_STRIDE + jnp.arange(CMP_LEN)[None, :]
    blocks = t.astype(F32)[:, idx] + cmp_pos.astype(F32)[:, None, :]
    blocks = blocks.transpose(0, 1, 3, 2, 4).reshape(bsz, n_cmp, groups, CMP_LEN * dh)
    return jax.nn.gelu(blocks @ w1.astype(F32)) @ w2.astype(F32)


def nsa_attention(q, k_cmp, v_cmp, k_slc, v_slc, k_win, v_win, gate_logits,
                  q_gain, k_gain, cmp_pos, k_w1, k_w2, v_w1, v_w2, pos):
    bsz, seq = q.shape[:2]
    rep = C_HEADS // C_KV_HEADS
    n_cmp = (seq - CMP_LEN) // CMP_STRIDE + 1
    n_slc = seq // SLC_LEN
    top_n = min(SLC_TOPN, n_slc)
    q = rope(rms_norm(q, q_gain), pos).astype(F32) * HEAD_DIM ** -0.5
    cmp_end = jnp.arange(n_cmp) * CMP_STRIDE + (CMP_LEN - 1)
    kc = rope(rms_norm(compress_blocks(k_cmp, cmp_pos, k_w1, k_w2), k_gain), cmp_end)
    vc = compress_blocks(v_cmp, cmp_pos, v_w1, v_w2)
    ks = rope(rms_norm(k_slc, k_gain), pos).astype(F32)
    ks = ks.reshape(bsz, n_slc, SLC_LEN, C_KV_HEADS, HEAD_DIM).transpose(0, 3, 1, 2, 4)
    vs = v_slc.astype(F32).reshape(bsz, n_slc, SLC_LEN, C_KV_HEADS, HEAD_DIM).transpose(0, 3, 1, 2, 4)
    pad = ((0, 0), (WIN, 0), (0, 0), (0, 0))
    kw = jnp.pad(rope(rms_norm(k_win, k_gain), pos).astype(F32), pad)
    vw = jnp.pad(v_win.astype(F32), pad)
    gates = jax.nn.sigmoid(gate_logits.astype(F32))
    ratio = SLC_LEN // CMP_STRIDE
    span = CMP_LEN // CMP_STRIDE
    off = jnp.arange(n_cmp)[:, None] - ratio * jnp.arange(n_slc)[None, :]
    cmp_to_slc = jnp.maximum(
        jnp.minimum(ratio - 1, off) - jnp.maximum(0, off - span + 1) + 1, 0).astype(F32)
    slc_ids = jnp.arange(n_slc)
    tok_off = jnp.arange(SLC_LEN)
    win_off = jnp.arange(WIN + C_BLOCK)
    gather = jax.vmap(jax.vmap(lambda t, idx: t[idx]))

    def block(args):
        qb, gb, start = args
        tq = start + jnp.arange(C_BLOCK)
        qg = qb.reshape(bsz, C_BLOCK, C_KV_HEADS, rep, HEAD_DIM)
        gb = gb.reshape(bsz, C_BLOCK, C_KV_HEADS, rep, 3)
        s_c = jnp.einsum('bqgrd,bcgd->bqgrc', qg, kc)
        p_c = masked_softmax(s_c, (cmp_end[None, :] <= tq[:, None])[None, :, None, None, :])
        o_c = jnp.einsum('bqgrc,bcgd->bqgrd', p_c, vc)
        imp = jnp.einsum('bqgrc,cj->bqgj', p_c, cmp_to_slc)
        cur = tq // SLC_LEN
        forced = (slc_ids[None, :] == cur[:, None]) | (slc_ids[None, :] == 0)
        admissible = slc_ids[None, :] <= cur[:, None]
        imp = jnp.where(forced[None, :, None, :], FORCE_SCORE, imp)
        imp = jnp.where(admissible[None, :, None, :], imp, -jnp.inf)
        _, sel = lax.top_k(imp, top_n)
        sel_g = sel.transpose(0, 2, 1, 3)
        k_sel = gather(ks, sel_g)
        v_sel = gather(vs, sel_g)
        s_s = jnp.einsum('bqgrd,bgqnld->bqgrnl', qg, k_sel)
        kpos = sel[..., None] * SLC_LEN + tok_off
        smask = (kpos <= tq[None, :, None, None, None]).reshape(bsz, C_BLOCK, C_KV_HEADS, 1, -1)
        p_s = masked_softmax(s_s.reshape(bsz, C_BLOCK, C_KV_HEADS, rep, -1), smask)
        o_s = jnp.einsum('bqgrnl,bgqnld->bqgrd', p_s.reshape(s_s.shape), v_sel)
        kwb = lax.dynamic_slice_in_dim(kw, start, WIN + C_BLOCK, axis=1)
        vwb = lax.dynamic_slice_in_dim(vw, start, WIN + C_BLOCK, axis=1)
        wpos = start - WIN + win_off
        wmask = ((wpos[None, :] <= tq[:, None]) & (wpos[None, :] > tq[:, None] - WIN)
                 & (wpos[None, :] >= 0))
        s_w = jnp.einsum('bqgrd,bkgd->bqgrk', qg, kwb)
        p_w = masked_softmax(s_w, wmask[None, :, None, None, :])
        o_w = jnp.einsum('bqgrk,bkgd->bqgrd', p_w, vwb)
        o = gb[..., 0:1] * o_c + gb[..., 1:2] * o_s + gb[..., 2:3] * o_w
        return o.reshape(bsz, C_BLOCK, C_HEADS * HEAD_DIM)

    starts = jnp.arange(seq // C_BLOCK) * C_BLOCK
    out = lax.map(block, (to_blocks(q, C_BLOCK), to_blocks(gates, C_BLOCK), starts))
    return out.swapaxes(0, 1).reshape(bsz, seq, C_HEADS * HEAD_DIM)


def gla_attention(q, k, v, g_low, r, w_gate, b_gate, o_gain):
    bsz, seq = q.shape[:2]
    n_chunk = seq // GLA_CHUNK
    logit = g_low.astype(F32) @ w_gate.astype(F32) + b_gate.astype(F32)
    log_a = jax.nn.log_sigmoid(logit) / GATE_TAU

    def chunks(t):
        return t.astype(F32).reshape(bsz, n_chunk, GLA_CHUNK, D_HEADS, -1).transpose(0, 3, 1, 2, 4)

    qc = chunks(q) * D_KDIM ** -0.5
    kc, vc, gc = chunks(k), chunks(v), chunks(log_a)
    bcum = jnp.cumsum(gc, axis=3)
    blast = bcum[:, :, :, -1:, :]
    q_dec = qc * jnp.exp(bcum)
    k_inv = kc * jnp.exp(-bcum)
    causal = jnp.tril(jnp.ones((GLA_CHUNK, GLA_CHUNK), dtype=bool))
    att = jnp.where(causal, jnp.einsum('bhncd,bhnsd->bhncs', q_dec, k_inv), 0.0)
    o_intra = jnp.einsum('bhncs,bhnse->bhnce', att, vc)
    upd = jnp.einsum('bhncd,bhnce->bhnde', kc * jnp.exp(blast - bcum), vc)
    decay = jnp.exp(blast[:, :, :, 0, :])

    def step(state, inp):
        dec, u = inp
        return dec[..., None] * state + u, state

    init = jnp.zeros((bsz, D_HEADS, D_KDIM, D_VDIM), F32)
    _, prev = lax.scan(step, init, (jnp.moveaxis(decay, 2, 0), jnp.moveaxis(upd, 2, 0)))
    o_inter = jnp.einsum('bhncd,nbhde->bhnce', q_dec, prev)
    o = (o_intra + o_inter).transpose(0, 2, 3, 1, 4).reshape(bsz, seq, D_HEADS, D_VDIM)
    o = rms_norm(o, o_gain).reshape(bsz, seq, D_HEADS * D_VDIM)
    return o * jax.nn.silu(r.astype(F32))


def peer_ffn(x, wq, subkeys, u_tab, v_tab):
    bsz, seq, dm = x.shape
    xt = x.reshape(bsz * seq // PEER_BLOCK, PEER_BLOCK, dm)
    half = PEER_QDIM // 2

    def block(xb):
        q = (xb @ wq).reshape(PEER_BLOCK, PEER_HEADS, 2, half)
        s = jnp.einsum('thpc,hpkc->thpk', q, subkeys).astype(F32)
        s1, i1 = lax.top_k(s[:, :, 0], PEER_TOPK)
        s2, i2 = lax.top_k(s[:, :, 1], PEER_TOPK)
        cand = (s1[..., :, None] + s2[..., None, :]).reshape(PEER_BLOCK, PEER_HEADS, -1)
        cand_id = (i1[..., :, None] * PEER_KEYS + i2[..., None, :]).reshape(PEER_BLOCK, PEER_HEADS, -1)
        top_s, top_pos = lax.top_k(cand, PEER_TOPK)
        expert = jnp.take_along_axis(cand_id, top_pos, axis=-1).reshape(PEER_BLOCK, -1)
        gate = jax.nn.softmax(top_s, axis=-1).reshape(PEER_BLOCK, -1)
        h = jax.nn.gelu(jnp.einsum('td,ted->te', xb, u_tab[expert]).astype(F32))
        return jnp.einsum('te,ted->td', gate * h, v_tab[expert].astype(F32))

    return lax.map(block, xt).reshape(bsz, seq, dm)


def even_mixer(h, pos, w_in, q_gain, k_gain, log_dt, a_re, a_im, b_re, b_im,
               c_re, c_im, d_skip, w_glu, b_glu, w_out):
    bsz, seq = h.shape[:2]
    q, k, v, iq, ik, iw, u = split_cols(h @ w_in, EVEN_COLS)
    heads = lambda t, n: t.reshape(bsz, seq, n, -1)
    o_a = dsa_attention(heads(q, A_HEADS), heads(k, A_KV_HEADS), heads(v, A_KV_HEADS),
                        heads(iq, IDX_HEADS), ik, iw, q_gain, k_gain, pos)
    o_b = s5_ssm(u, log_dt, a_re, a_im, b_re, b_im, c_re, c_im, d_skip, w_glu, b_glu)
    return jnp.concatenate([o_a, o_b], axis=-1) @ w_out.astype(F32)


def odd_mixer(h, pos, w_in, q_gain, k_gain, cmp_pos, k_w1, k_w2, v_w1, v_w2,
              w_gate, b_gate, o_gain, w_out):
    bsz, seq = h.shape[:2]
    (q, kc, vc, ks, vs, kw, vw, gl, gq, gk, gv, glow, gr) = split_cols(h @ w_in, ODD_COLS)
    heads = lambda t, n: t.reshape(bsz, seq, n, -1)
    o_c = nsa_attention(heads(q, C_HEADS), heads(kc, C_KV_HEADS), heads(vc, C_KV_HEADS),
                        heads(ks, C_KV_HEADS), heads(vs, C_KV_HEADS), heads(kw, C_KV_HEADS),
                        heads(vw, C_KV_HEADS), heads(gl, C_HEADS), q_gain, k_gain,
                        cmp_pos, k_w1, k_w2, v_w1, v_w2, pos)
    o_d = gla_attention(gq, gk, gv, glow, gr, w_gate, b_gate, o_gain)
    return jnp.concatenate([o_c, o_d], axis=-1) @ w_out.astype(F32)


def setup_inputs(seed: int = 0) -> dict:
    key = jax.random.key(seed)
    keys = iter(list(jax.random.split(key, 64)))

    def normal(shape, scale):
        return jax.random.normal(next(keys), shape, F32) * scale

    def gain(n):
        return 1.0 + normal((n,), 0.02)

    def peer_params():
        return (normal((D_MODEL, PEER_HEADS * PEER_QDIM), D_MODEL ** -0.5),
                normal((PEER_HEADS, 2, PEER_KEYS, PEER_QDIM // 2), (PEER_QDIM // 2) ** -0.5),
                normal((PEER_EXPERTS, D_MODEL), D_MODEL ** -0.5),
                normal((PEER_EXPERTS, D_MODEL), PEER_TOPK ** -0.5))

    x = normal((BATCH, SEQ, D_MODEL), 1.0)
    l0_mix_norm = gain(D_MODEL)
    l0_w_in = normal((D_MODEL, sum(EVEN_COLS)), D_MODEL ** -0.5)
    l0_a_q_gain = gain(HEAD_DIM)
    l0_a_k_gain = gain(HEAD_DIM)
    l0_s5_log_dt = jax.random.uniform(next(keys), (B_GROUPS,), F32, math.log(1e-3), math.log(1e-1))
    l0_s5_a_re = -0.5 + normal((B_GROUPS, B_STATE), 0.01)
    l0_s5_a_im = math.pi * jnp.arange(B_STATE, dtype=F32)[None, :] + normal((B_GROUPS, B_STATE), 0.01)
    l0_s5_b_re = normal((B_GROUPS, B_STATE, B_GROUP), (2 * B_GROUP) ** -0.5)
    l0_s5_b_im = normal((B_GROUPS, B_STATE, B_GROUP), (2 * B_GROUP) ** -0.5)
    l0_s5_c_re = normal((B_GROUPS, B_GROUP, B_STATE), 0.5)
    l0_s5_c_im = normal((B_GROUPS, B_GROUP, B_STATE), 0.5)
    l0_s5_d = normal((B_GROUPS, B_GROUP), 0.5)
    l0_s5_w_glu = normal((B_WIDTH, B_WIDTH), B_WIDTH ** -0.5)
    l0_s5_b_glu = normal((B_WIDTH,), 0.01)
    l0_w_out = normal((MIX_WIDTH, D_MODEL), MIX_WIDTH ** -0.5)
    l0_ffn_norm = gain(D_MODEL)
    l0_peer_wq, l0_peer_subkeys, l0_peer_u, l0_peer_v = peer_params()
    l1_mix_norm = gain(D_MODEL)
    l1_w_in = normal((D_MODEL, sum(ODD_COLS)), D_MODEL ** -0.5)
    l1_c_q_gain = gain(HEAD_DIM)
    l1_c_k_gain = gain(HEAD_DIM)
    l1_nsa_cmp_pos = normal((CMP_LEN, HEAD_DIM), 0.02)
    l1_nsa_k_w1 = normal((CMP_LEN * HEAD_DIM, CMP_HIDDEN), (CMP_LEN * HEAD_DIM) ** -0.5)
    l1_nsa_k_w2 = normal((CMP_HIDDEN, HEAD_DIM), CMP_HIDDEN ** -0.5)
    l1_nsa_v_w1 = normal((CMP_LEN * HEAD_DIM, CMP_HIDDEN), (CMP_LEN * HEAD_DIM) ** -0.5)
    l1_nsa_v_w2 = normal((CMP_HIDDEN, HEAD_DIM), CMP_HIDDEN ** -0.5)
    l1_gla_w_gate = normal((GATE_RANK, D_HEADS * D_KDIM), GATE_RANK ** -0.5)
    l1_gla_b_gate = normal((D_HEADS * D_KDIM,), 0.01)
    l1_gla_o_gain = gain(D_VDIM)
    l1_w_out = normal((MIX_WIDTH, D_MODEL), MIX_WIDTH ** -0.5)
    l1_ffn_norm = gain(D_MODEL)
    l1_peer_wq, l1_peer_subkeys, l1_peer_u, l1_peer_v = peer_params()
    return {
        'x': x,
        'l0_mix_norm': l0_mix_norm, 'l0_w_in': l0_w_in,
        'l0_a_q_gain': l0_a_q_gain, 'l0_a_k_gain': l0_a_k_gain,
        'l0_s5_log_dt': l0_s5_log_dt, 'l0_s5_a_re': l0_s5_a_re, 'l0_s5_a_im': l0_s5_a_im,
        'l0_s5_b_re': l0_s5_b_re, 'l0_s5_b_im': l0_s5_b_im,
        'l0_s5_c_re': l0_s5_c_re, 'l0_s5_c_im': l0_s5_c_im, 'l0_s5_d': l0_s5_d,
        'l0_s5_w_glu': l0_s5_w_glu, 'l0_s5_b_glu': l0_s5_b_glu, 'l0_w_out': l0_w_out,
        'l0_ffn_norm': l0_ffn_norm, 'l0_peer_wq': l0_peer_wq, 'l0_peer_subkeys': l0_peer_subkeys,
        'l0_peer_u': l0_peer_u, 'l0_peer_v': l0_peer_v,
        'l1_mix_norm': l1_mix_norm, 'l1_w_in': l1_w_in,
        'l1_c_q_gain': l1_c_q_gain, 'l1_c_k_gain': l1_c_k_gain,
        'l1_nsa_cmp_pos': l1_nsa_cmp_pos, 'l1_nsa_k_w1': l1_nsa_k_w1, 'l1_nsa_k_w2': l1_nsa_k_w2,
        'l1_nsa_v_w1': l1_nsa_v_w1, 'l1_nsa_v_w2': l1_nsa_v_w2,
        'l1_gla_w_gate': l1_gla_w_gate, 'l1_gla_b_gate': l1_gla_b_gate, 'l1_gla_o_gain': l1_gla_o_gain,
        'l1_w_out': l1_w_out,
        'l1_ffn_norm': l1_ffn_norm, 'l1_peer_wq': l1_peer_wq, 'l1_peer_subkeys': l1_peer_subkeys,
        'l1_peer_u': l1_peer_u, 'l1_peer_v': l1_peer_v,
    }


def reference(x, l0_mix_norm, l0_w_in, l0_a_q_gain, l0_a_k_gain, l0_s5_log_dt, l0_s5_a_re,
              l0_s5_a_im, l0_s5_b_re, l0_s5_b_im, l0_s5_c_re, l0_s5_c_im, l0_s5_d, l0_s5_w_glu,
              l0_s5_b_glu, l0_w_out, l0_ffn_norm, l0_peer_wq, l0_peer_subkeys, l0_peer_u, l0_peer_v,
              l1_mix_norm, l1_w_in, l1_c_q_gain, l1_c_k_gain, l1_nsa_cmp_pos, l1_nsa_k_w1,
              l1_nsa_k_w2, l1_nsa_v_w1, l1_nsa_v_w2, l1_gla_w_gate, l1_gla_b_gate, l1_gla_o_gain,
              l1_w_out, l1_ffn_norm, l1_peer_wq, l1_peer_subkeys, l1_peer_u, l1_peer_v):
    seq = x.shape[1]
    pos = jnp.arange(seq, dtype=jnp.int32)
    mix_norms = (l0_mix_norm, l1_mix_norm)
    mix_params = (
        (l0_w_in, l0_a_q_gain, l0_a_k_gain, l0_s5_log_dt, l0_s5_a_re, l0_s5_a_im, l0_s5_b_re,
         l0_s5_b_im, l0_s5_c_re, l0_s5_c_im, l0_s5_d, l0_s5_w_glu, l0_s5_b_glu, l0_w_out),
        (l1_w_in, l1_c_q_gain, l1_c_k_gain, l1_nsa_cmp_pos, l1_nsa_k_w1, l1_nsa_k_w2,
         l1_nsa_v_w1, l1_nsa_v_w2, l1_gla_w_gate, l1_gla_b_gate, l1_gla_o_gain, l1_w_out),
    )
    ffn_params = (
        (l0_ffn_norm, l0_peer_wq, l0_peer_subkeys, l0_peer_u, l0_peer_v),
        (l1_ffn_norm, l1_peer_wq, l1_peer_subkeys, l1_peer_u, l1_peer_v),
    )
    for layer in range(DEPTH):
        mixer = even_mixer if layer % 2 == 0 else odd_mixer
        x = x + mixer(rms_norm(x, mix_norms[layer]), pos, *mix_params[layer]).astype(x.dtype)
        g_ffn, wq, subkeys, u_tab, v_tab = ffn_params[layer]
        x = x + peer_ffn(rms_norm(x, g_ffn), wq, subkeys, u_tab, v_tab).astype(x.dtype)
    return x
```

```python
import functools
import math

import jax
import jax.numpy as jnp
import numpy as np
from jax import lax
from jax.experimental import pallas as pl
from jax.experimental.pallas import tpu as pltpu

F32 = jnp.float32
BF16 = jnp.bfloat16
NORM_EPS = 1e-6
ROPE_THETA = 10000.0
HEAD_DIM = 64
ATTN_BLOCK = 128
A_HEADS = 8
A_KV_HEADS = 2
IDX_HEADS = 8
IDX_DIM = 64
A_TOPK_MAX = 256
B_WIDTH = 512
B_GROUP = 16
B_GROUPS = B_WIDTH // B_GROUP
B_STATE = 64
C_HEADS = 8
C_KV_HEADS = 2
CMP_LEN = 32
CMP_STRIDE = 16
CMP_HIDDEN = 256
SLC_LEN = 64
SLC_TOPN = 16
WIN = 512
C_BLOCK = 64
FORCE_SCORE = 1e9
D_HEADS = 4
D_KDIM = 64
D_VDIM = 128
GATE_RANK = 16
GATE_TAU = 16.0
GLA_CHUNK = 32
PEER_HEADS = 8
PEER_KEYS = 128
PEER_QDIM = 256
PEER_TOPK = 16
PEER_BLOCK = 128

EVEN_COLS = (A_HEADS * HEAD_DIM, A_KV_HEADS * HEAD_DIM, A_KV_HEADS * HEAD_DIM,
             IDX_HEADS * IDX_DIM, IDX_DIM, IDX_HEADS, B_WIDTH)
ODD_COLS = (C_HEADS * HEAD_DIM,) + (C_KV_HEADS * HEAD_DIM,) * 6 + (
    C_HEADS * 3, D_HEADS * D_KDIM, D_HEADS * D_KDIM, D_HEADS * D_VDIM, GATE_RANK, D_HEADS * D_VDIM)

LANES = 128
MXU_DIM = 256
VMEM_LIMIT = 48 * 1024 * 1024


def _norm_proj_kernel(x_ref, g_ref, w_ref, o_ref):
    x = x_ref[...]
    y = x * lax.rsqrt(jnp.mean(x * x, axis=-1, keepdims=True) + NORM_EPS) * g_ref[...]
    o_ref[...] = jnp.dot(y.astype(BF16), w_ref[...].astype(BF16), preferred_element_type=F32)


def _norm_proj(x2d, gain, w, tm=512, tn=512):
    m, k = x2d.shape
    n = w.shape[1]
    n_pad = -(-n // tn) * tn
    w_p = jnp.pad(w, ((0, 0), (0, n_pad - n)))
    out = pl.pallas_call(
        _norm_proj_kernel,
        grid=(m // tm, n_pad // tn),
        in_specs=[pl.BlockSpec((tm, k), lambda i, j: (i, 0)),
                  pl.BlockSpec((1, k), lambda i, j: (0, 0)),
                  pl.BlockSpec((k, tn), lambda i, j: (0, j))],
        out_specs=pl.BlockSpec((tm, tn), lambda i, j: (i, j)),
        out_shape=jax.ShapeDtypeStruct((m, n_pad), F32),
        compiler_params=pltpu.CompilerParams(
            dimension_semantics=("arbitrary", "arbitrary"), vmem_limit_bytes=VMEM_LIMIT),
    )(x2d, gain.reshape(1, k), w_p)
    return out[:, :n]


def _proj_res_kernel(a_ref, w_ref, r_ref, o_ref):
    o_ref[...] = r_ref[...] + jnp.dot(a_ref[...].astype(BF16), w_ref[...].astype(BF16),
                                      preferred_element_type=F32)


def _proj_residual(a2d, w, res2d, tm=512):
    m, k = a2d.shape
    n = w.shape[1]
    return pl.pallas_call(
        _proj_res_kernel,
        grid=(m // tm,),
        in_specs=[pl.BlockSpec((tm, k), lambda i: (i, 0)),
                  pl.BlockSpec((k, n), lambda i: (0, 0)),
                  pl.BlockSpec((tm, n), lambda i: (i, 0))],
        out_specs=pl.BlockSpec((tm, n), lambda i: (i, 0)),
        out_shape=jax.ShapeDtypeStruct((m, n), F32),
        compiler_params=pltpu.CompilerParams(
            dimension_semantics=("arbitrary",), vmem_limit_bytes=VMEM_LIMIT),
    )(a2d, w, res2d)


def _rms_norm(x, gain):
    xf = x.astype(F32)
    y = xf * lax.rsqrt(jnp.mean(xf * xf, axis=-1, keepdims=True) + NORM_EPS)
    return (y * gain.astype(F32)).astype(x.dtype)


def _rope(x, pos):
    half = x.shape[-1] // 2
    inv_freq = ROPE_THETA ** (-jnp.arange(half, dtype=F32) / half)
    ang = pos.astype(F32)[:, None] * inv_freq[None, :]
    cos = jnp.cos(ang)[:, None, :]
    sin = jnp.sin(ang)[:, None, :]
    xf = x.astype(F32)
    x1, x2 = xf[..., :half], xf[..., half:]
    return jnp.concatenate([x1 * cos - x2 * sin, x1 * sin + x2 * cos], axis=-1).astype(x.dtype)


def _masked_softmax(s, mask):
    s = jnp.where(mask, s.astype(F32), -1e30)
    m = jnp.max(s, axis=-1, keepdims=True)
    p = jnp.exp(s - m) * mask
    return p / jnp.maximum(jnp.sum(p, axis=-1, keepdims=True), 1e-30)


def _split_cols(h, sizes):
    return jnp.split(h, np.cumsum(sizes)[:-1].tolist(), axis=-1)


def _to_blocks(t, blk):
    return t.reshape(t.shape[0], t.shape[1] // blk, blk, *t.shape[2:]).swapaxes(0, 1)


def _dsa_attention(q, k, v, iq, ik, iw, q_gain, k_gain, pos):
    bsz, seq = q.shape[:2]
    topk = min(A_TOPK_MAX, seq // 4)
    rep = A_HEADS // A_KV_HEADS
    q = _rope(_rms_norm(q, q_gain), pos).astype(F32) * HEAD_DIM ** -0.5
    k = _rope(_rms_norm(k, k_gain), pos).astype(F32)
    v = v.astype(F32)
    iq = _rope(iq, pos).astype(F32) * IDX_DIM ** -0.5
    ik = _rope(ik[:, :, None, :], pos)[:, :, 0, :].astype(F32)
    iw = iw.astype(F32) * IDX_HEADS ** -0.5
    key_pos = jnp.arange(seq)
    gather = jax.vmap(lambda t, idx: t[idx])

    def block(args):
        qb, iqb, iwb, start = args
        tq = start + jnp.arange(ATTN_BLOCK)
        rel = jax.nn.relu(jnp.einsum('bqhd,bsd->bqhs', iqb, ik))
        score = jnp.einsum('bqh,bqhs->bqs', iwb, rel)
        causal = key_pos[None, :] <= tq[:, None]
        score = jnp.where(causal[None], score, -jnp.inf)
        _, sel = lax.top_k(score, topk)
        valid = sel <= tq[None, :, None]
        k_sel = gather(k, sel)
        v_sel = gather(v, sel)
        qg = qb.reshape(bsz, ATTN_BLOCK, A_KV_HEADS, rep, HEAD_DIM)
        s = jnp.einsum('bqgrd,bqkgd->bqgrk', qg, k_sel)
        p = _masked_softmax(s, valid[:, :, None, None, :])
        o = jnp.einsum('bqgrk,bqkgd->bqgrd', p, v_sel)
        return o.reshape(bsz, ATTN_BLOCK, A_HEADS * HEAD_DIM)

    starts = jnp.arange(seq // ATTN_BLOCK) * ATTN_BLOCK
    out = lax.map(block, (_to_blocks(q, ATTN_BLOCK), _to_blocks(iq, ATTN_BLOCK),
                          _to_blocks(iw, ATTN_BLOCK), starts))
    return out.swapaxes(0, 1).reshape(bsz, seq, A_HEADS * HEAD_DIM)


def _s5_ssm(u, log_dt, a_re, a_im, b_re, b_im, c_re, c_im, d_skip, w_glu, b_glu):
    bsz, seq = u.shape[:2]
    uf = u.astype(F32).reshape(bsz, seq, B_GROUPS, B_GROUP)
    dt = jnp.exp(log_dt.astype(F32))[:, None]
    lr, li = a_re.astype(F32), a_im.astype(F32)
    mag = jnp.exp(dt * lr)
    ab_re = mag * jnp.cos(dt * li)
    ab_im = mag * jnp.sin(dt * li)
    den = lr * lr + li * li
    f_re = ((ab_re - 1.0) * lr + ab_im * li) / den
    f_im = (ab_im * lr - (ab_re - 1.0) * li) / den
    br, bi = b_re.astype(F32), b_im.astype(F32)
    bb_re = f_re[..., None] * br - f_im[..., None] * bi
    bb_im = f_re[..., None] * bi + f_im[..., None] * br
    x_re = jnp.einsum('gpc,bsgc->bsgp', bb_re, uf)
    x_im = jnp.einsum('gpc,bsgc->bsgp', bb_im, uf)
    a_re_t = jnp.broadcast_to(ab_re, x_re.shape)
    a_im_t = jnp.broadcast_to(ab_im, x_im.shape)

    def combine(e1, e2):
        a1r, a1i, b1r, b1i = e1
        a2r, a2i, b2r, b2i = e2
        return (a1r * a2r - a1i * a2i, a1r * a2i + a1i * a2r,
                a2r * b1r - a2i * b1i + b2r, a2r * b1i + a2i * b1r + b2i)

    _, _, h_re, h_im = lax.associative_scan(combine, (a_re_t, a_im_t, x_re, x_im), axis=1)
    y = (jnp.einsum('gcp,bsgp->bsgc', c_re.astype(F32), h_re)
         - jnp.einsum('gcp,bsgp->bsgc', c_im.astype(F32), h_im)
         + d_skip.astype(F32) * uf)
    y = jax.nn.gelu(y.reshape(bsz, seq, B_WIDTH))
    return y * jax.nn.sigmoid(y @ w_glu.astype(F32) + b_glu.astype(F32))


def _compress_blocks(t, cmp_pos, w1, w2):
    bsz, seq, groups, dh = t.shape
    n_cmp = (seq - CMP_LEN) // CMP_STRIDE + 1
    idx = jnp.arange(n_cmp)[:, None] * CMP_STRIDE + jnp.arange(CMP_LEN)[None, :]
    blocks = t.astype(F32)[:, idx] + cmp_pos.astype(F32)[:, None, :]
    blocks = blocks.transpose(0, 1, 3, 2, 4).reshape(bsz, n_cmp, groups, CMP_LEN * dh)
    return jax.nn.gelu(blocks @ w1.astype(F32)) @ w2.astype(F32)


def _nsa_attention(q, k_cmp, v_cmp, k_slc, v_slc, k_win, v_win, gate_logits,
                   q_gain, k_gain, cmp_pos, k_w1, k_w2, v_w1, v_w2, pos):
    bsz, seq = q.shape[:2]
    rep = C_HEADS // C_KV_HEADS
    n_cmp = (seq - CMP_LEN) // CMP_STRIDE + 1
    n_slc = seq // SLC_LEN
    top_n = min(SLC_TOPN, n_slc)
    q = _rope(_rms_norm(q, q_gain), pos).astype(F32) * HEAD_DIM ** -0.5
    cmp_end = jnp.arange(n_cmp) * CMP_STRIDE + (CMP_LEN - 1)
    kc = _rope(_rms_norm(_compress_blocks(k_cmp, cmp_pos, k_w1, k_w2), k_gain), cmp_end)
    vc = _compress_blocks(v_cmp, cmp_pos, v_w1, v_w2)
    ks = _rope(_rms_norm(k_slc, k_gain), pos).astype(F32)
    ks = ks.reshape(bsz, n_slc, SLC_LEN, C_KV_HEADS, HEAD_DIM).transpose(0, 3, 1, 2, 4)
    vs = v_slc.astype(F32).reshape(bsz, n_slc, SLC_LEN, C_KV_HEADS, HEAD_DIM).transpose(0, 3, 1, 2, 4)
    pad = ((0, 0), (WIN, 0), (0, 0), (0, 0))
    kw = jnp.pad(_rope(_rms_norm(k_win, k_gain), pos).astype(F32), pad)
    vw = jnp.pad(v_win.astype(F32), pad)
    gates = jax.nn.sigmoid(gate_logits.astype(F32))
    ratio = SLC_LEN // CMP_STRIDE
    span = CMP_LEN // CMP_STRIDE
    off = jnp.arange(n_cmp)[:, None] - ratio * jnp.arange(n_slc)[None, :]
    cmp_to_slc = jnp.maximum(
        jnp.minimum(ratio - 1, off) - jnp.maximum(0, off - span + 1) + 1, 0).astype(F32)
    slc_ids = jnp.arange(n_slc)
    tok_off = jnp.arange(SLC_LEN)
    win_off = jnp.arange(WIN + C_BLOCK)
    gather = jax.vmap(jax.vmap(lambda t, idx: t[idx]))

    def block(args):
        qb, gb, start = args
        tq = start + jnp.arange(C_BLOCK)
        qg = qb.reshape(bsz, C_BLOCK, C_KV_HEADS, rep, HEAD_DIM)
        gb = gb.reshape(bsz, C_BLOCK, C_KV_HEADS, rep, 3)
        s_c = jnp.einsum('bqgrd,bcgd->bqgrc', qg, kc)
        p_c = _masked_softmax(s_c, (cmp_end[None, :] <= tq[:, None])[None, :, None, None, :])
        o_c = jnp.einsum('bqgrc,bcgd->bqgrd', p_c, vc)
        imp = jnp.einsum('bqgrc,cj->bqgj', p_c, cmp_to_slc)
        cur = tq // SLC_LEN
        forced = (slc_ids[None, :] == cur[:, None]) | (slc_ids[None, :] == 0)
        admissible = slc_ids[None, :] <= cur[:, None]
        imp = jnp.where(forced[None, :, None, :], FORCE_SCORE, imp)
        imp = jnp.where(admissible[None, :, None, :], imp, -jnp.inf)
        _, sel = lax.top_k(imp, top_n)
        sel_g = sel.transpose(0, 2, 1, 3)
        k_sel = gather(ks, sel_g)
        v_sel = gather(vs, sel_g)
        s_s = jnp.einsum('bqgrd,bgqnld->bqgrnl', qg, k_sel)
        kpos = sel[..., None] * SLC_LEN + tok_off
        smask = (kpos <= tq[None, :, None, None, None]).reshape(bsz, C_BLOCK, C_KV_HEADS, 1, -1)
        p_s = _masked_softmax(s_s.reshape(bsz, C_BLOCK, C_KV_HEADS, rep, -1), smask)
        o_s = jnp.einsum('bqgrnl,bgqnld->bqgrd', p_s.reshape(s_s.shape), v_sel)
        kwb = lax.dynamic_slice_in_dim(kw, start, WIN + C_BLOCK, axis=1)
        vwb = lax.dynamic_slice_in_dim(vw, start, WIN + C_BLOCK, axis=1)
        wpos = start - WIN + win_off
        wmask = ((wpos[None, :] <= tq[:, None]) & (wpos[None, :] > tq[:, None] - WIN)
                 & (wpos[None, :] >= 0))
        s_w = jnp.einsum('bqgrd,bkgd->bqgrk', qg, kwb)
        p_w = _masked_softmax(s_w, wmask[None, :, None, None, :])
        o_w = jnp.einsum('bqgrk,bkgd->bqgrd', p_w, vwb)
        o = gb[..., 0:1] * o_c + gb[..., 1:2] * o_s + gb[..., 2:3] * o_w
        return o.reshape(bsz, C_BLOCK, C_HEADS * HEAD_DIM)

    starts = jnp.arange(seq // C_BLOCK) * C_BLOCK
    out = lax.map(block, (_to_blocks(q, C_BLOCK), _to_blocks(gates, C_BLOCK), starts))
    return out.swapaxes(0, 1).reshape(bsz, seq, C_HEADS * HEAD_DIM)


def _gla_attention(q, k, v, g_low, r, w_gate, b_gate, o_gain):
    bsz, seq = q.shape[:2]
    n_chunk = seq // GLA_CHUNK
    logit = g_low.astype(F32) @ w_gate.astype(F32) + b_gate.astype(F32)
    log_a = jax.nn.log_sigmoid(logit) / GATE_TAU

    def chunks(t):
        return t.astype(F32).reshape(bsz, n_chunk, GLA_CHUNK, D_HEADS, -1).transpose(0, 3, 1, 2, 4)

    qc = chunks(q) * D_KDIM ** -0.5
    kc, vc, gc = chunks(k), chunks(v), chunks(log_a)
    bcum = jnp.cumsum(gc, axis=3)
    blast = bcum[:, :, :, -1:, :]
    q_dec = qc * jnp.exp(bcum)
    k_inv = kc * jnp.exp(-bcum)
    causal = jnp.tril(jnp.ones((GLA_CHUNK, GLA_CHUNK), dtype=bool))
    att = jnp.where(causal, jnp.einsum('bhncd,bhnsd->bhncs', q_dec, k_inv), 0.0)
    o_intra = jnp.einsum('bhncs,bhnse->bhnce', att, vc)
    upd = jnp.einsum('bhncd,bhnce->bhnde', kc * jnp.exp(blast - bcum), vc)
    decay = jnp.exp(blast[:, :, :, 0, :])

    def step(state, inp):
        dec, u = inp
        return dec[..., None] * state + u, state

    init = jnp.zeros((bsz, D_HEADS, D_KDIM, D_VDIM), F32)
    _, prev = lax.scan(step, init, (jnp.moveaxis(decay, 2, 0), jnp.moveaxis(upd, 2, 0)))
    o_inter = jnp.einsum('bhncd,nbhde->bhnce', q_dec, prev)
    o = (o_intra + o_inter).transpose(0, 2, 3, 1, 4).reshape(bsz, seq, D_HEADS, D_VDIM)
    o = _rms_norm(o, o_gain).reshape(bsz, seq, D_HEADS * D_VDIM)
    return o * jax.nn.silu(r.astype(F32))


def _peer_ffn(x, g_ffn, wq, subkeys, u_tab, v_tab):
    bsz, seq, dm = x.shape
    n_tok = bsz * seq
    x2d = x.reshape(n_tok, dm)
    xn = _rms_norm(x2d, g_ffn).reshape(n_tok // PEER_BLOCK, PEER_BLOCK, dm)
    q_all = _norm_proj(x2d, g_ffn, wq).reshape(n_tok // PEER_BLOCK, PEER_BLOCK, -1)
    half = PEER_QDIM // 2

    def block(args):
        xb, qb = args
        q = qb.reshape(PEER_BLOCK, PEER_HEADS, 2, half)
        s = jnp.einsum('thpc,hpkc->thpk', q, subkeys).astype(F32)
        s1, i1 = lax.top_k(s[:, :, 0], PEER_TOPK)
        s2, i2 = lax.top_k(s[:, :, 1], PEER_TOPK)
        cand = (s1[..., :, None] + s2[..., None, :]).reshape(PEER_BLOCK, PEER_HEADS, -1)
        cand_id = (i1[..., :, None] * PEER_KEYS + i2[..., None, :]).reshape(PEER_BLOCK, PEER_HEADS, -1)
        top_s, top_pos = lax.top_k(cand, PEER_TOPK)
        expert = jnp.take_along_axis(cand_id, top_pos, axis=-1).reshape(PEER_BLOCK, -1)
        gate = jax.nn.softmax(top_s, axis=-1).reshape(PEER_BLOCK, -1)
        h = jax.nn.gelu(jnp.einsum('td,ted->te', xb, u_tab[expert]).astype(F32))
        return jnp.einsum('te,ted->td', gate * h, v_tab[expert].astype(F32))

    return lax.map(block, (xn, q_all)).reshape(bsz, seq, dm)


def _even_layer(x, pos, mix_norm, w_in, q_gain, k_gain, log_dt, a_re, a_im, b_re, b_im,
                c_re, c_im, d_skip, w_glu, b_glu, w_out):
    bsz, seq, dm = x.shape
    x2d = x.reshape(bsz * seq, dm)
    h = _norm_proj(x2d, mix_norm, w_in).reshape(bsz, seq, -1)
    q, k, v, iq, ik, iw, u = _split_cols(h, EVEN_COLS)
    heads = lambda t, n: t.reshape(bsz, seq, n, -1)
    o_a = _dsa_attention(heads(q, A_HEADS), heads(k, A_KV_HEADS), heads(v, A_KV_HEADS),
                         heads(iq, IDX_HEADS), ik, iw, q_gain, k_gain, pos)
    o_b = _s5_ssm(u, log_dt, a_re, a_im, b_re, b_im, c_re, c_im, d_skip, w_glu, b_glu)
    mix = jnp.concatenate([o_a, o_b], axis=-1).reshape(bsz * seq, -1)
    return _proj_residual(mix, w_out, x2d).reshape(bsz, seq, dm)


def _odd_layer(x, pos, mix_norm, w_in, q_gain, k_gain, cmp_pos, k_w1, k_w2, v_w1, v_w2,
               w_gate, b_gate, o_gain, w_out):
    bsz, seq, dm = x.shape
    x2d = x.reshape(bsz * seq, dm)
    h = _norm_proj(x2d, mix_norm, w_in).reshape(bsz, seq, -1)
    (q, kc, vc, ks, vs, kw, vw, gl, gq, gk, gv, glow, gr) = _split_cols(h, ODD_COLS)
    heads = lambda t, n: t.reshape(bsz, seq, n, -1)
    o_c = _nsa_attention(heads(q, C_HEADS), heads(kc, C_KV_HEADS), heads(vc, C_KV_HEADS),
                         heads(ks, C_KV_HEADS), heads(vs, C_KV_HEADS), heads(kw, C_KV_HEADS),
                         heads(vw, C_KV_HEADS), heads(gl, C_HEADS), q_gain, k_gain,
                         cmp_pos, k_w1, k_w2, v_w1, v_w2, pos)
    o_d = _gla_attention(gq, gk, gv, glow, gr, w_gate, b_gate, o_gain)
    mix = jnp.concatenate([o_c, o_d], axis=-1).reshape(bsz * seq, -1)
    return _proj_residual(mix, w_out, x2d).reshape(bsz, seq, dm)


def kernel(x, l0_mix_norm, l0_w_in, l0_a_q_gain, l0_a_k_gain, l0_s5_log_dt, l0_s5_a_re, l0_s5_a_im, l0_s5_b_re, l0_s5_b_im, l0_s5_c_re, l0_s5_c_im, l0_s5_d, l0_s5_w_glu, l0_s5_b_glu, l0_w_out, l0_ffn_norm, l0_peer_wq, l0_peer_subkeys, l0_peer_u, l0_peer_v, l1_mix_norm, l1_w_in, l1_c_q_gain, l1_c_k_gain, l1_nsa_cmp_pos, l1_nsa_k_w1, l1_nsa_k_w2, l1_nsa_v_w1, l1_nsa_v_w2, l1_gla_w_gate, l1_gla_b_gate, l1_gla_o_gain, l1_w_out, l1_ffn_norm, l1_peer_wq, l1_peer_subkeys, l1_peer_u, l1_peer_v):
    seq = x.shape[1]
    pos = jnp.arange(seq, dtype=jnp.int32)
    x = _even_layer(x, pos, l0_mix_norm, l0_w_in, l0_a_q_gain, l0_a_k_gain, l0_s5_log_dt,
                    l0_s5_a_re, l0_s5_a_im, l0_s5_b_re, l0_s5_b_im, l0_s5_c_re, l0_s5_c_im,
                    l0_s5_d, l0_s5_w_glu, l0_s5_b_glu, l0_w_out)
    x = x + _peer_ffn(x, l0_ffn_norm, l0_peer_wq, l0_peer_subkeys, l0_peer_u, l0_peer_v)
    x = _odd_layer(x, pos, l1_mix_norm, l1_w_in, l1_c_q_gain, l1_c_k_gain, l1_nsa_cmp_pos,
                   l1_nsa_k_w1, l1_nsa_k_w2, l1_nsa_v_w1, l1_nsa_v_w2, l1_gla_w_gate,
                   l1_gla_b_gate, l1_gla_o_gain, l1_w_out)
    x = x + _peer_ffn(x, l1_ffn_norm, l1_peer_wq, l1_peer_subkeys, l1_peer_u, l1_peer_v)
    return x
```

```python
import functools
import math

import jax
import jax.numpy as jnp
import numpy as np
from jax import lax
from jax.experimental import pallas as pl
from jax.experimental.pallas import tpu as pltpu

F32 = jnp.float32
BF16 = jnp.bfloat16
NORM_EPS = 1e-6
ROPE_THETA = 10000.0
HEAD_DIM = 64
ATTN_BLOCK = 128
A_HEADS = 8
A_KV_HEADS = 2
IDX_HEADS = 8
IDX_DIM = 64
A_TOPK_MAX = 256
B_WIDTH = 512
B_GROUP = 16
B_GROUPS = B_WIDTH // B_GROUP
B_STATE = 64
C_HEADS = 8
C_KV_HEADS = 2
CMP_LEN = 32
CMP_STRIDE = 16
CMP_HIDDEN = 256
SLC_LEN = 64
SLC_TOPN = 16
WIN = 512
C_BLOCK = 64
FORCE_SCORE = 1e9
D_HEADS = 4
D_KDIM = 64
D_VDIM = 128
GATE_RANK = 16
GATE_TAU = 16.0
GLA_CHUNK = 32
PEER_HEADS = 8
PEER_KEYS = 128
PEER_QDIM = 256
PEER_TOPK = 16
PEER_BLOCK = 128

EVEN_COLS = (A_HEADS * HEAD_DIM, A_KV_HEADS * HEAD_DIM, A_KV_HEADS * HEAD_DIM,
             IDX_HEADS * IDX_DIM, IDX_DIM, IDX_HEADS, B_WIDTH)
ODD_COLS = (C_HEADS * HEAD_DIM,) + (C_KV_HEADS * HEAD_DIM,) * 6 + (
    C_HEADS * 3, D_HEADS * D_KDIM, D_HEADS * D_KDIM, D_HEADS * D_VDIM, GATE_RANK, D_HEADS * D_VDIM)

LANES = 128
MXU_DIM = 256
VMEM_LIMIT = 48 * 1024 * 1024
PEER_VMEM_LIMIT = 56 * 1024 * 1024


def _norm_proj_kernel(x_ref, g_ref, w_ref, o_ref):
    x = x_ref[...]
    y = x * lax.rsqrt(jnp.mean(x * x, axis=-1, keepdims=True) + NORM_EPS) * g_ref[...]
    o_ref[...] = jnp.dot(y.astype(BF16), w_ref[...].astype(BF16), preferred_element_type=F32)


def _norm_proj(x2d, gain, w, tm=512, tn=512):
    m, k = x2d.shape
    n = w.shape[1]
    n_pad = -(-n // tn) * tn
    w_p = jnp.pad(w, ((0, 0), (0, n_pad - n)))
    out = pl.pallas_call(
        _norm_proj_kernel,
        grid=(m // tm, n_pad // tn),
        in_specs=[pl.BlockSpec((tm, k), lambda i, j: (i, 0)),
                  pl.BlockSpec((1, k), lambda i, j: (0, 0)),
                  pl.BlockSpec((k, tn), lambda i, j: (0, j))],
        out_specs=pl.BlockSpec((tm, tn), lambda i, j: (i, j)),
        out_shape=jax.ShapeDtypeStruct((m, n_pad), F32),
        compiler_params=pltpu.CompilerParams(
            dimension_semantics=("arbitrary", "arbitrary"), vmem_limit_bytes=VMEM_LIMIT),
    )(x2d, gain.reshape(1, k), w_p)
    return out[:, :n]


def _proj_res_kernel(a_ref, w_ref, r_ref, o_ref):
    o_ref[...] = r_ref[...] + jnp.dot(a_ref[...].astype(BF16), w_ref[...].astype(BF16),
                                      preferred_element_type=F32)


def _proj_residual(a2d, w, res2d, tm=512):
    m, k = a2d.shape
    n = w.shape[1]
    return pl.pallas_call(
        _proj_res_kernel,
        grid=(m // tm,),
        in_specs=[pl.BlockSpec((tm, k), lambda i: (i, 0)),
                  pl.BlockSpec((k, n), lambda i: (0, 0)),
                  pl.BlockSpec((tm, n), lambda i: (i, 0))],
        out_specs=pl.BlockSpec((tm, n), lambda i: (i, 0)),
        out_shape=jax.ShapeDtypeStruct((m, n), F32),
        compiler_params=pltpu.CompilerParams(
            dimension_semantics=("arbitrary",), vmem_limit_bytes=VMEM_LIMIT),
    )(a2d, w, res2d)


def _rms_norm(x, gain):
    xf = x.astype(F32)
    y = xf * lax.rsqrt(jnp.mean(xf * xf, axis=-1, keepdims=True) + NORM_EPS)
    return (y * gain.astype(F32)).astype(x.dtype)


def _rope(x, pos):
    half = x.shape[-1] // 2
    inv_freq = ROPE_THETA ** (-jnp.arange(half, dtype=F32) / half)
    ang = pos.astype(F32)[:, None] * inv_freq[None, :]
    cos = jnp.cos(ang)[:, None, :]
    sin = jnp.sin(ang)[:, None, :]
    xf = x.astype(F32)
    x1, x2 = xf[..., :half], xf[..., half:]
    return jnp.concatenate([x1 * cos - x2 * sin, x1 * sin + x2 * cos], axis=-1).astype(x.dtype)


def _masked_softmax(s, mask):
    s = jnp.where(mask, s.astype(F32), -1e30)
    m = jnp.max(s, axis=-1, keepdims=True)
    p = jnp.exp(s - m) * mask
    return p / jnp.maximum(jnp.sum(p, axis=-1, keepdims=True), 1e-30)


def _split_cols(h, sizes):
    return jnp.split(h, np.cumsum(sizes)[:-1].tolist(), axis=-1)


def _to_blocks(t, blk):
    return t.reshape(t.shape[0], t.shape[1] // blk, blk, *t.shape[2:]).swapaxes(0, 1)


def _dsa_attention(q, k, v, iq, ik, iw, q_gain, k_gain, pos):
    bsz, seq = q.shape[:2]
    topk = min(A_TOPK_MAX, seq // 4)
    rep = A_HEADS // A_KV_HEADS
    q = _rope(_rms_norm(q, q_gain), pos).astype(F32) * HEAD_DIM ** -0.5
    k = _rope(_rms_norm(k, k_gain), pos).astype(F32)
    v = v.astype(F32)
    iq = _rope(iq, pos).astype(F32) * IDX_DIM ** -0.5
    ik = _rope(ik[:, :, None, :], pos)[:, :, 0, :].astype(F32)
    iw = iw.astype(F32) * IDX_HEADS ** -0.5
    key_pos = jnp.arange(seq)
    gather = jax.vmap(lambda t, idx: t[idx])

    def block(args):
        qb, iqb, iwb, start = args
        tq = start + jnp.arange(ATTN_BLOCK)
        rel = jax.nn.relu(jnp.einsum('bqhd,bsd->bqhs', iqb, ik))
        score = jnp.einsum('bqh,bqhs->bqs', iwb, rel)
        causal = key_pos[None, :] <= tq[:, None]
        score = jnp.where(causal[None], score, -jnp.inf)
        _, sel = lax.top_k(score, topk)
        valid = sel <= tq[None, :, None]
        k_sel = gather(k, sel)
        v_sel = gather(v, sel)
        qg = qb.reshape(bsz, ATTN_BLOCK, A_KV_HEADS, rep, HEAD_DIM)
        s = jnp.einsum('bqgrd,bqkgd->bqgrk', qg, k_sel)
        p = _masked_softmax(s, valid[:, :, None, None, :])
        o = jnp.einsum('bqgrk,bqkgd->bqgrd', p, v_sel)
        return o.reshape(bsz, ATTN_BLOCK, A_HEADS * HEAD_DIM)

    starts = jnp.arange(seq // ATTN_BLOCK) * ATTN_BLOCK
    out = lax.map(block, (_to_blocks(q, ATTN_BLOCK), _to_blocks(iq, ATTN_BLOCK),
                          _to_blocks(iw, ATTN_BLOCK), starts))
    return out.swapaxes(0, 1).reshape(bsz, seq, A_HEADS * HEAD_DIM)


def _s5_ssm(u, log_dt, a_re, a_im, b_re, b_im, c_re, c_im, d_skip, w_glu, b_glu):
    bsz, seq = u.shape[:2]
    uf = u.astype(F32).reshape(bsz, seq, B_GROUPS, B_GROUP)
    dt = jnp.exp(log_dt.astype(F32))[:, None]
    lr, li = a_re.astype(F32), a_im.astype(F32)
    mag = jnp.exp(dt * lr)
    ab_re = mag * jnp.cos(dt * li)
    ab_im = mag * jnp.sin(dt * li)
    den = lr * lr + li * li
    f_re = ((ab_re - 1.0) * lr + ab_im * li) / den
    f_im = (ab_im * lr - (ab_re - 1.0) * li) / den
    br, bi = b_re.astype(F32), b_im.astype(F32)
    bb_re = f_re[..., None] * br - f_im[..., None] * bi
    bb_im = f_re[..., None] * bi + f_im[..., None] * br
    x_re = jnp.einsum('gpc,bsgc->bsgp', bb_re, uf)
    x_im = jnp.einsum('gpc,bsgc->bsgp', bb_im, uf)
    a_re_t = jnp.broadcast_to(ab_re, x_re.shape)
    a_im_t = jnp.broadcast_to(ab_im, x_im.shape)

    def combine(e1, e2):
        a1r, a1i, b1r, b1i = e1
        a2r, a2i, b2r, b2i = e2
        return (a1r * a2r - a1i * a2i, a1r * a2i + a1i * a2r,
                a2r * b1r - a2i * b1i + b2r, a2r * b1i + a2i * b1r + b2i)

    _, _, h_re, h_im = lax.associative_scan(combine, (a_re_t, a_im_t, x_re, x_im), axis=1)
    y = (jnp.einsum('gcp,bsgp->bsgc', c_re.astype(F32), h_re)
         - jnp.einsum('gcp,bsgp->bsgc', c_im.astype(F32), h_im)
         + d_skip.astype(F32) * uf)
    y = jax.nn.gelu(y.reshape(bsz, seq, B_WIDTH))
    return y * jax.nn.sigmoid(y @ w_glu.astype(F32) + b_glu.astype(F32))


def _compress_blocks(t, cmp_pos, w1, w2):
    bsz, seq, groups, dh = t.shape
    n_cmp = (seq - CMP_LEN) // CMP_STRIDE + 1
    idx = jnp.arange(n_cmp)[:, None] * CMP_STRIDE + jnp.arange(CMP_LEN)[None, :]
    blocks = t.astype(F32)[:, idx] + cmp_pos.astype(F32)[:, None, :]
    blocks = blocks.transpose(0, 1, 3, 2, 4).reshape(bsz, n_cmp, groups, CMP_LEN * dh)
    return jax.nn.gelu(blocks @ w1.astype(F32)) @ w2.astype(F32)


def _nsa_attention(q, k_cmp, v_cmp, k_slc, v_slc, k_win, v_win, gate_logits,
                   q_gain, k_gain, cmp_pos, k_w1, k_w2, v_w1, v_w2, pos):
    bsz, seq = q.shape[:2]
    rep = C_HEADS // C_KV_HEADS
    n_cmp = (seq - CMP_LEN) // CMP_STRIDE + 1
    n_slc = seq // SLC_LEN
    top_n = min(SLC_TOPN, n_slc)
    q = _rope(_rms_norm(q, q_gain), pos).astype(F32) * HEAD_DIM ** -0.5
    cmp_end = jnp.arange(n_cmp) * CMP_STRIDE + (CMP_LEN - 1)
    kc = _rope(_rms_norm(_compress_blocks(k_cmp, cmp_pos, k_w1, k_w2), k_gain), cmp_end)
    vc = _compress_blocks(v_cmp, cmp_pos, v_w1, v_w2)
    ks = _rope(_rms_norm(k_slc, k_gain), pos).astype(F32)
    ks = ks.reshape(bsz, n_slc, SLC_LEN, C_KV_HEADS, HEAD_DIM).transpose(0, 3, 1, 2, 4)
    vs = v_slc.astype(F32).reshape(bsz, n_slc, SLC_LEN, C_KV_HEADS, HEAD_DIM).transpose(0, 3, 1, 2, 4)
    pad = ((0, 0), (WIN, 0), (0, 0), (0, 0))
    kw = jnp.pad(_rope(_rms_norm(k_win, k_gain), pos).astype(F32), pad)
    vw = jnp.pad(v_win.astype(F32), pad)
    gates = jax.nn.sigmoid(gate_logits.astype(F32))
    ratio = SLC_LEN // CMP_STRIDE
    span = CMP_LEN // CMP_STRIDE
    off = jnp.arange(n_cmp)[:, None] - ratio * jnp.arange(n_slc)[None, :]
    cmp_to_slc = jnp.maximum(
        jnp.minimum(ratio - 1, off) - jnp.maximum(0, off - span + 1) + 1, 0).astype(F32)
    slc_ids = jnp.arange(n_slc)
    tok_off = jnp.arange(SLC_LEN)
    win_off = jnp.arange(WIN + C_BLOCK)
    gather = jax.vmap(jax.vmap(lambda t, idx: t[idx]))

    def block(args):
        qb, gb, start = args
        tq = start + jnp.arange(C_BLOCK)
        qg = qb.reshape(bsz, C_BLOCK, C_KV_HEADS, rep, HEAD_DIM)
        gb = gb.reshape(bsz, C_BLOCK, C_KV_HEADS, rep, 3)
        s_c = jnp.einsum('bqgrd,bcgd->bqgrc', qg, kc)
        p_c = _masked_softmax(s_c, (cmp_end[None, :] <= tq[:, None])[None, :, None, None, :])
        o_c = jnp.einsum('bqgrc,bcgd->bqgrd', p_c, vc)
        imp = jnp.einsum('bqgrc,cj->bqgj', p_c, cmp_to_slc)
        cur = tq // SLC_LEN
        forced = (slc_ids[None, :] == cur[:, None]) | (slc_ids[None, :] == 0)
        admissible = slc_ids[None, :] <= cur[:, None]
        imp = jnp.where(forced[None, :, None, :], FORCE_SCORE, imp)
        imp = jnp.where(admissible[None, :, None, :], imp, -jnp.inf)
        _, sel = lax.top_k(imp, top_n)
        sel_g = sel.transpose(0, 2, 1, 3)
        k_sel = gather(ks, sel_g)
        v_sel = gather(vs, sel_g)
        s_s = jnp.einsum('bqgrd,bgqnld->bqgrnl', qg, k_sel)
        kpos = sel[..., None] * SLC_LEN + tok_off
        smask = (kpos <= tq[None, :, None, None, None]).reshape(bsz, C_BLOCK, C_KV_HEADS, 1, -1)
        p_s = _masked_softmax(s_s.reshape(bsz, C_BLOCK, C_KV_HEADS, rep, -1), smask)
        o_s = jnp.einsum('bqgrnl,bgqnld->bqgrd', p_s.reshape(s_s.shape), v_sel)
        kwb = lax.dynamic_slice_in_dim(kw, start, WIN + C_BLOCK, axis=1)
        vwb = lax.dynamic_slice_in_dim(vw, start, WIN + C_BLOCK, axis=1)
        wpos = start - WIN + win_off
        wmask = ((wpos[None, :] <= tq[:, None]) & (wpos[None, :] > tq[:, None] - WIN)
                 & (wpos[None, :] >= 0))
        s_w = jnp.einsum('bqgrd,bkgd->bqgrk', qg, kwb)
        p_w = _masked_softmax(s_w, wmask[None, :, None, None, :])
        o_w = jnp.einsum('bqgrk,bkgd->bqgrd', p_w, vwb)
        o = gb[..., 0:1] * o_c + gb[..., 1:2] * o_s + gb[..., 2:3] * o_w
        return o.reshape(bsz, C_BLOCK, C_HEADS * HEAD_DIM)

    starts = jnp.arange(seq // C_BLOCK) * C_BLOCK
    out = lax.map(block, (_to_blocks(q, C_BLOCK), _to_blocks(gates, C_BLOCK), starts))
    return out.swapaxes(0, 1).reshape(bsz, seq, C_HEADS * HEAD_DIM)


def _gla_attention(q, k, v, g_low, r, w_gate, b_gate, o_gain):
    bsz, seq = q.shape[:2]
    n_chunk = seq // GLA_CHUNK
    logit = g_low.astype(F32) @ w_gate.astype(F32) + b_gate.astype(F32)
    log_a = jax.nn.log_sigmoid(logit) / GATE_TAU

    def chunks(t):
        return t.astype(F32).reshape(bsz, n_chunk, GLA_CHUNK, D_HEADS, -1).transpose(0, 3, 1, 2, 4)

    qc = chunks(q) * D_KDIM ** -0.5
    kc, vc, gc = chunks(k), chunks(v), chunks(log_a)
    bcum = jnp.cumsum(gc, axis=3)
    blast = bcum[:, :, :, -1:, :]
    q_dec = qc * jnp.exp(bcum)
    k_inv = kc * jnp.exp(-bcum)
    causal = jnp.tril(jnp.ones((GLA_CHUNK, GLA_CHUNK), dtype=bool))
    att = jnp.where(causal, jnp.einsum('bhncd,bhnsd->bhncs', q_dec, k_inv), 0.0)
    o_intra = jnp.einsum('bhncs,bhnse->bhnce', att, vc)
    upd = jnp.einsum('bhncd,bhnce->bhnde', kc * jnp.exp(blast - bcum), vc)
    decay = jnp.exp(blast[:, :, :, 0, :])

    def step(state, inp):
        dec, u = inp
        return dec[..., None] * state + u, state

    init = jnp.zeros((bsz, D_HEADS, D_KDIM, D_VDIM), F32)
    _, prev = lax.scan(step, init, (jnp.moveaxis(decay, 2, 0), jnp.moveaxis(upd, 2, 0)))
    o_inter = jnp.einsum('bhncd,nbhde->bhnce', q_dec, prev)
    o = (o_intra + o_inter).transpose(0, 2, 3, 1, 4).reshape(bsz, seq, D_HEADS, D_VDIM)
    o = _rms_norm(o, o_gain).reshape(bsz, seq, D_HEADS * D_VDIM)
    return o * jax.nn.silu(r.astype(F32))


PEER_SLOTS = PEER_HEADS * PEER_TOPK
PEER_TT = 128
SUBLANES = 8
ROW_TILE = (SUBLANES, LANES)
NEG_INF = float('-inf')


def _top16_rows(s, row_id, big):
    vals, ids = [], []
    for _ in range(PEER_TOPK):
        m = jnp.max(s, axis=0, keepdims=True)
        pick = jnp.min(jnp.where(s == m, row_id, big), axis=0, keepdims=True)
        s = jnp.where(row_id == pick, NEG_INF, s)
        vals.append(m)
        ids.append(pick)
    return vals, ids


def _peer_route_kernel(x_ref, g_ref, wq_ref, sk_ref, xn_ref, idx_ref, gate_ref, v_scr, i_scr):
    tt = x_ref.shape[0]
    x = x_ref[...]
    xn = x * lax.rsqrt(jnp.mean(x * x, axis=-1, keepdims=True) + NORM_EPS) * g_ref[...]
    xn_ref[...] = xn
    xb = xn.astype(BF16)
    key_id = lax.broadcasted_iota(jnp.int32, (PEER_KEYS, tt), 0)

    def sub_scores(hp, carry):
        q = jnp.dot(xb, wq_ref[hp], preferred_element_type=F32).astype(BF16)
        s = lax.dot_general(sk_ref[hp], q, (((1,), (1,)), ((), ())),
                            preferred_element_type=F32)
        vals, ids = _top16_rows(s, key_id, PEER_KEYS)
        v_scr[hp] = jnp.concatenate(vals, axis=0)
        i_scr[hp] = jnp.concatenate(ids, axis=0)
        return carry

    lax.fori_loop(0, 2 * PEER_HEADS, sub_scores, 0)

    n_blk = 10
    r = lax.broadcasted_iota(jnp.int32, (n_blk * SUBLANES, tt), 0)
    blk, b = r // SUBLANES, r % SUBLANES
    a_of_blk = jnp.where(blk < 2, 0, blk - 1)
    pos = jnp.where(blk == 9, (SUBLANES + b) * PEER_TOPK,
                    jnp.where(blk == 1, SUBLANES + b, a_of_blk * PEER_TOPK + b))
    b_max = jnp.where(blk < 3, 7, jnp.where(blk == 3, 4, jnp.where(blk == 4, 3, jnp.where(
        blk == 5, 2, jnp.where(blk < 9, 1, 7)))))
    live = b <= b_max

    def combine(h, carry):
        v1, v2 = v_scr[2 * h], v_scr[2 * h + 1]
        i1, i2 = i_scr[2 * h], i_scr[2 * h + 1]
        lo_v, hi_v = v2[0:SUBLANES], v2[SUBLANES:]
        lo_i, hi_i = i2[0:SUBLANES], i2[SUBLANES:]
        cand = [v1[0:1] + lo_v, v1[0:1] + hi_v]
        cid = [i1[0:1] * PEER_KEYS + lo_i, i1[0:1] * PEER_KEYS + hi_i]
        for a in range(1, SUBLANES):
            cand.append(v1[a:a + 1] + lo_v)
            cid.append(i1[a:a + 1] * PEER_KEYS + lo_i)
        cand.append(v1[SUBLANES:] + v2[0:1])
        cid.append(i1[SUBLANES:] * PEER_KEYS + i2[0:1])
        cand = jnp.where(live, jnp.concatenate(cand, axis=0), NEG_INF)
        cid = jnp.concatenate(cid, axis=0)
        top_s, experts = [], []
        for _ in range(PEER_TOPK):
            m = jnp.max(cand, axis=0, keepdims=True)
            pick = jnp.min(jnp.where(cand == m, pos, PEER_TOPK * PEER_TOPK), axis=0, keepdims=True)
            hit = pos == pick
            experts.append(jnp.max(jnp.where(hit, cid, -1), axis=0, keepdims=True))
            cand = jnp.where(hit, NEG_INF, cand)
            top_s.append(m)
        top_s = jnp.concatenate(top_s, axis=0)
        ex = jnp.exp(top_s - top_s[0:1])
        gate_ref[0, h] = ex / jnp.sum(ex, axis=0, keepdims=True)
        idx_ref[0, h] = jnp.concatenate(experts, axis=0)
        return carry

    lax.fori_loop(0, PEER_HEADS, combine, 0)


def _peer_route(x2d, g_ffn, wq, subkeys):
    n, dm = x2d.shape
    tt = PEER_TT
    half = PEER_QDIM // 2
    wq_t = wq.astype(BF16).reshape(dm, 2 * PEER_HEADS, half).transpose(1, 0, 2)
    sk = subkeys.astype(BF16).reshape(2 * PEER_HEADS, PEER_KEYS, half)
    tile4 = (1, PEER_HEADS, PEER_TOPK, tt)
    return pl.pallas_call(
        _peer_route_kernel,
        grid=(n // tt,),
        in_specs=[pl.BlockSpec((tt, dm), lambda i: (i, 0)),
                  pl.BlockSpec((1, dm), lambda i: (0, 0)),
                  pl.BlockSpec(wq_t.shape, lambda i: (0, 0, 0)),
                  pl.BlockSpec(sk.shape, lambda i: (0, 0, 0))],
        out_specs=[pl.BlockSpec((tt, dm), lambda i: (i, 0)),
                   pl.BlockSpec(tile4, lambda i: (i, 0, 0, 0)),
                   pl.BlockSpec(tile4, lambda i: (i, 0, 0, 0))],
        out_shape=[jax.ShapeDtypeStruct((n, dm), F32),
                   jax.ShapeDtypeStruct((n // tt,) + tile4[1:], jnp.int32),
                   jax.ShapeDtypeStruct((n // tt,) + tile4[1:], F32)],
        scratch_shapes=[pltpu.VMEM((2 * PEER_HEADS, PEER_TOPK, tt), F32),
                        pltpu.VMEM((2 * PEER_HEADS, PEER_TOPK, tt), jnp.int32)],
        compiler_params=pltpu.CompilerParams(
            dimension_semantics=("arbitrary",), vmem_limit_bytes=VMEM_LIMIT),
        name="peer_route",
    )(x2d, g_ffn.reshape(1, dm), wq_t, sk)


def _gather_rows(idx_smem, tab_ref, slab, t, tt):
    for i in range(PEER_SLOTS):
        slab[i * SUBLANES:(i + 1) * SUBLANES, :] = tab_ref[idx_smem[i * tt + t]]


def _peer_up_kernel(idx_hbm, xn_ref, gate_ref, tab_ref, fold_ref, out_ref, idx_smem, slab, rbuf, sem):
    tt = xn_ref.shape[0]
    cp = pltpu.make_async_copy(idx_hbm.at[pl.program_id(0)], idx_smem, sem)
    cp.start()
    cp.wait()
    eye = (lax.broadcasted_iota(jnp.int32, (LANES, LANES), 0)
           == lax.broadcasted_iota(jnp.int32, (LANES, LANES), 1))

    def tok(t, carry):
        _gather_rows(idx_smem, tab_ref, slab, t, tt)
        x16 = jnp.tile(xn_ref[t], (LANES // SUBLANES, 1)).astype(BF16)
        y = lax.dot_general(slab[...], x16, (((1,), (1,)), ((), ())),
                            preferred_element_type=F32)
        y = y.reshape(PEER_SLOTS * SUBLANES // LANES, LANES, LANES)
        rbuf[t] = jnp.sum(jnp.where(eye[None], y, 0.0), axis=1)
        return carry

    lax.fori_loop(0, tt, tok, 0)
    r = rbuf[...].reshape(tt * SUBLANES, LANES)
    r_hi = r.astype(BF16)
    r_lo = (r - r_hi.astype(F32)).astype(BF16)
    h = (jnp.dot(r_hi, fold_ref[...], preferred_element_type=F32)
         + jnp.dot(r_lo, fold_ref[...], preferred_element_type=F32))
    out_ref[...] = gate_ref[...] * jax.nn.gelu(h)


def _peer_down_kernel(idx_hbm, gh_ref, res_ref, tab_ref, spread_ref, out_ref, idx_smem, slab, ghx, sem):
    tt = gh_ref.shape[0]
    cp = pltpu.make_async_copy(idx_hbm.at[pl.program_id(0)], idx_smem, sem)
    cp.start()
    cp.wait()
    ghx[...] = jnp.dot(gh_ref[...].astype(BF16), spread_ref[...], preferred_element_type=F32)
    width = PEER_SLOTS * SUBLANES
    diag = (lax.broadcasted_iota(jnp.int32, (SUBLANES, width), 0)
            == lax.broadcasted_iota(jnp.int32, (SUBLANES, width), 1) % SUBLANES)

    def tok(t, carry):
        _gather_rows(idx_smem, tab_ref, slab, t, tt)
        row = jnp.broadcast_to(ghx[pl.ds(t, 1), :], (SUBLANES, width))
        coef = jnp.where(diag, row, 0.0).astype(BF16)
        out_ref[t] = res_ref[t] + jnp.dot(coef, slab[...], preferred_element_type=F32)
        return carry

    lax.fori_loop(0, tt, tok, 0)


def _peer_table(tab):
    e, dm = tab.shape
    assert dm == SUBLANES * LANES
    return tab.astype(BF16).reshape(e, SUBLANES, LANES)


def _peer_ffn_residual(x2d, g_ffn, wq, subkeys, u_tab, v_tab):
    n, dm = x2d.shape
    tt = PEER_TT
    n_tiles = n // tt
    xn, idx, gate = _peer_route(x2d, g_ffn, wq, subkeys)
    idx_flat = idx.reshape(n_tiles, PEER_SLOTS * tt)
    gate_rows = gate.transpose(0, 3, 1, 2).reshape(n * PEER_HEADS, PEER_TOPK)
    width = PEER_SLOTS * SUBLANES
    lane = np.arange(LANES)
    fold = jnp.asarray(lane[:, None] // SUBLANES == np.arange(PEER_TOPK)[None, :], BF16)
    spread = jnp.asarray(np.arange(PEER_SLOTS)[:, None] == np.arange(width)[None, :] // SUBLANES, BF16)
    params = pltpu.CompilerParams(dimension_semantics=("arbitrary",),
                                  vmem_limit_bytes=PEER_VMEM_LIMIT)
    table_spec = pl.BlockSpec(memory_space=pltpu.VMEM)
    slab_shape = pltpu.VMEM((PEER_SLOTS * SUBLANES, LANES), BF16)
    idx_scratch = pltpu.SMEM((PEER_SLOTS * tt,), jnp.int32)
    row3 = (tt,) + ROW_TILE

    gh = pl.pallas_call(
        _peer_up_kernel,
        grid=(n_tiles,),
        in_specs=[pl.BlockSpec(memory_space=pl.ANY),
                  pl.BlockSpec(row3, lambda i: (i, 0, 0)),
                  pl.BlockSpec((tt * PEER_HEADS, PEER_TOPK), lambda i: (i, 0)),
                  table_spec,
                  pl.BlockSpec((LANES, PEER_TOPK), lambda i: (0, 0))],
        out_specs=pl.BlockSpec((tt * PEER_HEADS, PEER_TOPK), lambda i: (i, 0)),
        out_shape=jax.ShapeDtypeStruct((n * PEER_HEADS, PEER_TOPK), F32),
        scratch_shapes=[idx_scratch, slab_shape, pltpu.VMEM(row3, F32), pltpu.SemaphoreType.DMA(())],
        compiler_params=params,
        name="peer_up",
    )(idx_flat, xn.reshape((n,) + ROW_TILE), gate_rows, _peer_table(u_tab), fold)

    out = pl.pallas_call(
        _peer_down_kernel,
        grid=(n_tiles,),
        in_specs=[pl.BlockSpec(memory_space=pl.ANY),
                  pl.BlockSpec((tt, PEER_SLOTS), lambda i: (i, 0)),
                  pl.BlockSpec(row3, lambda i: (i, 0, 0)),
                  table_spec,
                  pl.BlockSpec((PEER_SLOTS, width), lambda i: (0, 0))],
        out_specs=pl.BlockSpec(row3, lambda i: (i, 0, 0)),
        out_shape=jax.ShapeDtypeStruct((n,) + ROW_TILE, F32),
        scratch_shapes=[idx_scratch, slab_shape, pltpu.VMEM((tt, width), F32), pltpu.SemaphoreType.DMA(())],
        compiler_params=params,
        name="peer_down",
    )(idx_flat, gh.reshape(n, PEER_SLOTS), x2d.reshape((n,) + ROW_TILE), _peer_table(v_tab), spread)
    return out.reshape(n, dm)


def _even_layer(x, pos, mix_norm, w_in, q_gain, k_gain, log_dt, a_re, a_im, b_re, b_im,
                c_re, c_im, d_skip, w_glu, b_glu, w_out):
    bsz, seq, dm = x.shape
    x2d = x.reshape(bsz * seq, dm)
    h = _norm_proj(x2d, mix_norm, w_in).reshape(bsz, seq, -1)
    q, k, v, iq, ik, iw, u = _split_cols(h, EVEN_COLS)
    heads = lambda t, n: t.reshape(bsz, seq, n, -1)
    o_a = _dsa_attention(heads(q, A_HEADS), heads(k, A_KV_HEADS), heads(v, A_KV_HEADS),
                         heads(iq, IDX_HEADS), ik, iw, q_gain, k_gain, pos)
    o_b = _s5_ssm(u, log_dt, a_re, a_im, b_re, b_im, c_re, c_im, d_skip, w_glu, b_glu)
    mix = jnp.concatenate([o_a, o_b], axis=-1).reshape(bsz * seq, -1)
    return _proj_residual(mix, w_out, x2d).reshape(bsz, seq, dm)


def _odd_layer(x, pos, mix_norm, w_in, q_gain, k_gain, cmp_pos, k_w1, k_w2, v_w1, v_w2,
               w_gate, b_gate, o_gain, w_out):
    bsz, seq, dm = x.shape
    x2d = x.reshape(bsz * seq, dm)
    h = _norm_proj(x2d, mix_norm, w_in).reshape(bsz, seq, -1)
    (q, kc, vc, ks, vs, kw, vw, gl, gq, gk, gv, glow, gr) = _split_cols(h, ODD_COLS)
    heads = lambda t, n: t.reshape(bsz, seq, n, -1)
    o_c = _nsa_attention(heads(q, C_HEADS), heads(kc, C_KV_HEADS), heads(vc, C_KV_HEADS),
                         heads(ks, C_KV_HEADS), heads(vs, C_KV_HEADS), heads(kw, C_KV_HEADS),
                         heads(vw, C_KV_HEADS), heads(gl, C_HEADS), q_gain, k_gain,
                         cmp_pos, k_w1, k_w2, v_w1, v_w2, pos)
    o_d = _gla_attention(gq, gk, gv, glow, gr, w_gate, b_gate, o_gain)
    mix = jnp.concatenate([o_c, o_d], axis=-1).reshape(bsz * seq, -1)
    return _proj_residual(mix, w_out, x2d).reshape(bsz, seq, dm)


def kernel(x, l0_mix_norm, l0_w_in, l0_a_q_gain, l0_a_k_gain, l0_s5_log_dt, l0_s5_a_re, l0_s5_a_im, l0_s5_b_re, l0_s5_b_im, l0_s5_c_re, l0_s5_c_im, l0_s5_d, l0_s5_w_glu, l0_s5_b_glu, l0_w_out, l0_ffn_norm, l0_peer_wq, l0_peer_subkeys, l0_peer_u, l0_peer_v, l1_mix_norm, l1_w_in, l1_c_q_gain, l1_c_k_gain, l1_nsa_cmp_pos, l1_nsa_k_w1, l1_nsa_k_w2, l1_nsa_v_w1, l1_nsa_v_w2, l1_gla_w_gate, l1_gla_b_gate, l1_gla_o_gain, l1_w_out, l1_ffn_norm, l1_peer_wq, l1_peer_subkeys, l1_peer_u, l1_peer_v):
    seq = x.shape[1]
    pos = jnp.arange(seq, dtype=jnp.int32)
    x = _even_layer(x, pos, l0_mix_norm, l0_w_in, l0_a_q_gain, l0_a_k_gain, l0_s5_log_dt,
                    l0_s5_a_re, l0_s5_a_im, l0_s5_b_re, l0_s5_b_im, l0_s5_c_re, l0_s5_c_im,
                    l0_s5_d, l0_s5_w_glu, l0_s5_b_glu, l0_w_out)
    shape = x.shape
    flat = lambda t: t.reshape(-1, shape[-1])
    x = _peer_ffn_residual(flat(x), l0_ffn_norm, l0_peer_wq, l0_peer_subkeys,
                           l0_peer_u, l0_peer_v).reshape(shape)
    x = _odd_layer(x, pos, l1_mix_norm, l1_w_in, l1_c_q_gain, l1_c_k_gain, l1_nsa_cmp_pos,
                   l1_nsa_k_w1, l1_nsa_k_w2, l1_nsa_v_w1, l1_nsa_v_w2, l1_gla_w_gate,
                   l1_gla_b_gate, l1_gla_o_gain, l1_w_out)
    return _peer_ffn_residual(flat(x), l1_ffn_norm, l1_peer_wq, l1_peer_subkeys,
                              l1_peer_u, l1_peer_v).reshape(shape)
```

```python
import functools
import math

import jax
import jax.numpy as jnp
import numpy as np
from jax import lax
from jax.experimental import pallas as pl
from jax.experimental.pallas import tpu as pltpu

F32 = jnp.float32
BF16 = jnp.bfloat16
NORM_EPS = 1e-6
ROPE_THETA = 10000.0
HEAD_DIM = 64
ATTN_BLOCK = 128
A_HEADS = 8
A_KV_HEADS = 2
IDX_HEADS = 8
IDX_DIM = 64
A_TOPK_MAX = 256
B_WIDTH = 512
B_GROUP = 16
B_GROUPS = B_WIDTH // B_GROUP
B_STATE = 64
C_HEADS = 8
C_KV_HEADS = 2
CMP_LEN = 32
CMP_STRIDE = 16
CMP_HIDDEN = 256
SLC_LEN = 64
SLC_TOPN = 16
WIN = 512
C_BLOCK = 64
FORCE_SCORE = 1e9
D_HEADS = 4
D_KDIM = 64
D_VDIM = 128
GATE_RANK = 16
GATE_TAU = 16.0
GLA_CHUNK = 32
PEER_HEADS = 8
PEER_KEYS = 128
PEER_QDIM = 256
PEER_TOPK = 16
PEER_BLOCK = 128

EVEN_COLS = (A_HEADS * HEAD_DIM, A_KV_HEADS * HEAD_DIM, A_KV_HEADS * HEAD_DIM,
             IDX_HEADS * IDX_DIM, IDX_DIM, IDX_HEADS, B_WIDTH)
ODD_COLS = (C_HEADS * HEAD_DIM,) + (C_KV_HEADS * HEAD_DIM,) * 6 + (
    C_HEADS * 3, D_HEADS * D_KDIM, D_HEADS * D_KDIM, D_HEADS * D_VDIM, GATE_RANK, D_HEADS * D_VDIM)

LANES = 128
SUBLANES = 8
MXU_DIM = 256
VMEM_LIMIT = 48 * 1024 * 1024
PEER_VMEM_LIMIT = 56 * 1024 * 1024


def _norm_proj_kernel(x_ref, g_ref, w_ref, o_ref):
    x = x_ref[...]
    y = x * lax.rsqrt(jnp.mean(x * x, axis=-1, keepdims=True) + NORM_EPS) * g_ref[...]
    o_ref[...] = jnp.dot(y.astype(BF16), w_ref[...].astype(BF16), preferred_element_type=F32)


def _norm_proj(x2d, gain, w, tm=512, tn=512):
    m, k = x2d.shape
    n = w.shape[1]
    n_pad = -(-n // tn) * tn
    w_p = jnp.pad(w, ((0, 0), (0, n_pad - n)))
    out = pl.pallas_call(
        _norm_proj_kernel,
        grid=(m // tm, n_pad // tn),
        in_specs=[pl.BlockSpec((tm, k), lambda i, j: (i, 0)),
                  pl.BlockSpec((1, k), lambda i, j: (0, 0)),
                  pl.BlockSpec((k, tn), lambda i, j: (0, j))],
        out_specs=pl.BlockSpec((tm, tn), lambda i, j: (i, j)),
        out_shape=jax.ShapeDtypeStruct((m, n_pad), F32),
        compiler_params=pltpu.CompilerParams(
            dimension_semantics=("arbitrary", "arbitrary"), vmem_limit_bytes=VMEM_LIMIT),
    )(x2d, gain.reshape(1, k), w_p)
    return out[:, :n]


def _proj_res_kernel(a_ref, w_ref, r_ref, o_ref):
    o_ref[...] = r_ref[...] + jnp.dot(a_ref[...].astype(BF16), w_ref[...].astype(BF16),
                                      preferred_element_type=F32)


def _proj_residual(a2d, w, res2d, tm=512):
    m, k = a2d.shape
    n = w.shape[1]
    return pl.pallas_call(
        _proj_res_kernel,
        grid=(m // tm,),
        in_specs=[pl.BlockSpec((tm, k), lambda i: (i, 0)),
                  pl.BlockSpec((k, n), lambda i: (0, 0)),
                  pl.BlockSpec((tm, n), lambda i: (i, 0))],
        out_specs=pl.BlockSpec((tm, n), lambda i: (i, 0)),
        out_shape=jax.ShapeDtypeStruct((m, n), F32),
        compiler_params=pltpu.CompilerParams(
            dimension_semantics=("arbitrary",), vmem_limit_bytes=VMEM_LIMIT),
    )(a2d, w, res2d)


def _rms_norm(x, gain):
    xf = x.astype(F32)
    y = xf * lax.rsqrt(jnp.mean(xf * xf, axis=-1, keepdims=True) + NORM_EPS)
    return (y * gain.astype(F32)).astype(x.dtype)


def _rope(x, pos):
    half = x.shape[-1] // 2
    inv_freq = ROPE_THETA ** (-jnp.arange(half, dtype=F32) / half)
    ang = pos.astype(F32)[:, None] * inv_freq[None, :]
    cos = jnp.cos(ang)[:, None, :]
    sin = jnp.sin(ang)[:, None, :]
    xf = x.astype(F32)
    x1, x2 = xf[..., :half], xf[..., half:]
    return jnp.concatenate([x1 * cos - x2 * sin, x1 * sin + x2 * cos], axis=-1).astype(x.dtype)


def _masked_softmax(s, mask):
    s = jnp.where(mask, s.astype(F32), -1e30)
    m = jnp.max(s, axis=-1, keepdims=True)
    p = jnp.exp(s - m) * mask
    return p / jnp.maximum(jnp.sum(p, axis=-1, keepdims=True), 1e-30)


def _split_cols(h, sizes):
    return jnp.split(h, np.cumsum(sizes)[:-1].tolist(), axis=-1)


def _to_blocks(t, blk):
    return t.reshape(t.shape[0], t.shape[1] // blk, blk, *t.shape[2:]).swapaxes(0, 1)


INT_MIN = -2 ** 31
MASKED = -1e30
DSA_KC = 512
EVEN_LAYOUT = dict(q=(0, 512), iq=(512, 512), u=(1024, 512), k=(1536, 128), v=(1664, 128),
                   ik=(1792, 128), iw=(1920, 128))
EVEN_WIDTH = 2048


def _exact_dot(x, m):
    hi = x.astype(BF16)
    lo = (x - hi.astype(F32)).astype(BF16)
    return (jnp.dot(hi, m, preferred_element_type=F32) + jnp.dot(lo, m, preferred_element_type=F32))


def _head_rms(x, gain, hsum):
    return x * lax.rsqrt(_exact_dot(x * x, hsum) + NORM_EPS) * gain


def _head_rope(x, cos, sin, rot):
    return x * cos + _exact_dot(x, rot) * sin


def _split_pair(x, swap):
    lane = lax.broadcasted_iota(jnp.int32, x.shape, 1)
    low = lane < HEAD_DIM
    xs = jnp.dot(x, swap, preferred_element_type=F32).astype(BF16)
    zero = jnp.zeros_like(x)
    return (jnp.where(low, x, zero), jnp.where(low, zero, xs),
            jnp.where(low, xs, zero), jnp.where(low, zero, x))


def _dsa_prep_kernel(q_ref, iq_ref, k_ref, v_ref, ik_ref, iw_ref, cos_ref, sin_ref, qg_ref, kg_ref,
                     hsum_ref, rot_ref, swap_ref, qn_ref, iqn_ref, k4_ref, v4_ref, ik2_ref, iwn_ref):
    cos1, sin1 = cos_ref[...], sin_ref[...]
    reps = q_ref.shape[1] // LANES
    cos4, sin4 = jnp.tile(cos1, (1, reps)), jnp.tile(sin1, (1, reps))
    hsum, rot = hsum_ref[...], rot_ref[...]
    hsum1, rot1 = hsum[0:LANES, 0:LANES], rot[0:LANES, 0:LANES]
    scale = HEAD_DIM ** -0.5
    q = _head_rope(_head_rms(q_ref[...], qg_ref[...], hsum), cos4, sin4, rot) * scale
    qn_ref[...] = q.astype(BF16)
    iqn_ref[...] = (_head_rope(iq_ref[...], cos4, sin4, rot) * IDX_DIM ** -0.5).astype(BF16)
    k = _head_rope(_head_rms(k_ref[...], kg_ref[...], hsum1), cos1, sin1, rot1).astype(BF16)
    swap = swap_ref[...]
    for n, part in enumerate(_split_pair(k, swap)):
        k4_ref[n] = part
    for n, part in enumerate(_split_pair(v_ref[...].astype(BF16), swap)):
        v4_ref[n] = part
    ik = _head_rope(ik_ref[...], cos1, sin1, rot1).astype(BF16)
    ik_a, ik_b, _, _ = _split_pair(ik, swap)
    ik2_ref[0] = ik_a
    ik2_ref[1] = ik_b
    iwn_ref[...] = iw_ref[...] * IDX_HEADS ** -0.5


def _dot_nt(a, b):
    return lax.dot_general(a, b, (((1,), (1,)), ((), ())), preferred_element_type=F32)


def _dsa_attn_kernel(q_ref, iq_ref, iw_ref, k4_ref, v4_ref, ik2_ref, tri_ref, o_ref,
                     key_scr, m_scr, l_scr, acc_scr, *, topk):
    tb = q_ref.shape[0]
    kc = DSA_KC
    qi = pl.program_id(1)
    n_ch = qi // (kc // tb) + 1
    tq = qi * tb + lax.broadcasted_iota(jnp.int32, (tb, 1), 0)
    lane_k = lax.broadcasted_iota(jnp.int32, (tb, kc), 1)
    iw = iw_ref[...]

    def score_chunk(c, carry):
        ks = pl.multiple_of(c * kc, kc)
        ik_a = ik2_ref[0, pl.ds(ks, kc), :]
        ik_b = ik2_ref[1, pl.ds(ks, kc), :]
        sc = jnp.zeros((tb, kc), F32)
        for j in range(IDX_HEADS // 2):
            iqp = iq_ref[:, j * LANES:(j + 1) * LANES]
            sc = sc + iw[:, 2 * j:2 * j + 1] * jnp.maximum(_dot_nt(iqp, ik_a), 0.0)
            sc = sc + iw[:, 2 * j + 1:2 * j + 2] * jnp.maximum(_dot_nt(iqp, ik_b), 0.0)
        sc = jnp.where(sc == 0.0, 0.0, sc)
        bits = lax.bitcast_convert_type(sc, jnp.int32)
        key = bits ^ (jnp.right_shift(bits, 31) & 0x7FFFFFFF)
        key_scr[c] = jnp.where(ks + lane_k <= tq, key, INT_MIN)
        return carry

    lax.fori_loop(0, n_ch, score_chunk, 0)

    def count(pred_of_key):
        def body(c, acc):
            hit = jnp.where(pred_of_key(key_scr[c]), 1.0, 0.0)
            for s in range(kc // LANES):
                acc = acc + hit[:, s * LANES:(s + 1) * LANES]
            return acc
        acc = lax.fori_loop(0, n_ch, body, jnp.zeros((tb, LANES), F32))
        return jnp.sum(acc, axis=1, keepdims=True)

    def bit_step(i, t_u):
        cand = t_u | jnp.left_shift(jnp.int32(1), 31 - i)
        cand_s = cand ^ INT_MIN
        return jnp.where(count(lambda key: key >= cand_s) >= topk, cand, t_u)

    thr = lax.fori_loop(0, 32, bit_step, jnp.zeros((tb, 1), jnp.int32)) ^ INT_MIN
    need = topk - count(lambda key: key > thr)

    m_scr[...] = jnp.full(m_scr.shape, MASKED, F32)
    l_scr[...] = jnp.zeros(l_scr.shape, F32)
    acc_scr[...] = jnp.zeros(acc_scr.shape, F32)
    low = lax.broadcasted_iota(jnp.int32, (tb, LANES), 1) < HEAD_DIM

    def attn_chunk(c, ties_before):
        ks = pl.multiple_of(c * kc, kc)
        key = key_scr[c]
        eq = jnp.where(key == thr, 1.0, 0.0)
        rank = ties_before + jnp.dot(eq.astype(BF16), tri_ref[...], preferred_element_type=F32)
        tie_ok = jnp.where(rank <= need, eq, 0.0)
        sel = jnp.where(ks + lane_k <= tq, jnp.where(key > thr, 1.0, tie_ok), 0.0)
        chosen = sel > 0.0

        def head(s, h):
            sm = jnp.where(chosen, s, MASKED)
            m_old = m_scr[h]
            m_new = jnp.maximum(m_old, jnp.max(sm, axis=1, keepdims=True))
            p = jnp.exp(sm - m_new) * sel
            alpha = jnp.exp(m_old - m_new)
            l_scr[h] = alpha * l_scr[h] + jnp.sum(p, axis=1, keepdims=True)
            m_scr[h] = m_new
            return p.astype(BF16), alpha

        for j in range(A_HEADS // 2):
            g = (2 * j) // (A_HEADS // A_KV_HEADS)
            qp = q_ref[:, j * LANES:(j + 1) * LANES]
            k_a = k4_ref[2 * g, pl.ds(ks, kc), :]
            k_b = k4_ref[2 * g + 1, pl.ds(ks, kc), :]
            p_a, alpha_a = head(_dot_nt(qp, k_a), 2 * j)
            p_b, alpha_b = head(_dot_nt(qp, k_b), 2 * j + 1)
            pv = (jnp.dot(p_a, v4_ref[2 * g, pl.ds(ks, kc), :], preferred_element_type=F32)
                  + jnp.dot(p_b, v4_ref[2 * g + 1, pl.ds(ks, kc), :], preferred_element_type=F32))
            acc_scr[j] = jnp.where(low, alpha_a, alpha_b) * acc_scr[j] + pv
        return ties_before + jnp.sum(eq, axis=1, keepdims=True)

    lax.fori_loop(0, n_ch, attn_chunk, jnp.zeros((tb, 1), F32))
    for j in range(A_HEADS // 2):
        denom = jnp.maximum(jnp.where(low, l_scr[2 * j], l_scr[2 * j + 1]), 1e-30)
        o_ref[:, j * LANES:(j + 1) * LANES] = acc_scr[j] / denom


def _rope_tables(seq, width):
    half = HEAD_DIM // 2
    inv_freq = ROPE_THETA ** (-jnp.arange(half, dtype=F32) / half)
    ang = jnp.arange(seq, dtype=F32)[:, None] * inv_freq[None, :]
    reps = width // half
    return jnp.tile(jnp.cos(ang), (1, reps)), jnp.tile(jnp.sin(ang), (1, reps))


def _head_matrices(width):
    i = np.arange(width)
    same = (i[:, None] // HEAD_DIM) == (i[None, :] // HEAD_DIM)
    hsum = np.where(same, 1.0 / HEAD_DIM, 0.0)
    half = HEAD_DIM // 2
    src, dst = i[:, None], i[None, :]
    rot = np.where(same & (src == dst + half), -1.0, 0.0) + np.where(same & (src == dst - half), 1.0, 0.0)
    j = np.arange(LANES)
    swap = (j[:, None] == (j[None, :] + HEAD_DIM) % LANES).astype(np.float32)
    return jnp.asarray(hsum, BF16), jnp.asarray(rot, BF16), jnp.asarray(swap, BF16)


def _dsa_mixer(h, bsz, seq, q_gain, k_gain, tm=512):
    n = bsz * seq
    tb = ATTN_BLOCK
    assert seq % DSA_KC == 0 and seq % tm == 0
    topk = min(A_TOPK_MAX, seq // 4)
    cos, sin = _rope_tables(seq, LANES)
    hsum, rot, swap = _head_matrices(A_HEADS * HEAD_DIM)
    wide = A_HEADS * HEAD_DIM
    col = lambda name: EVEN_LAYOUT[name][0] // EVEN_LAYOUT[name][1]
    hspec = lambda name: pl.BlockSpec((tm, EVEN_LAYOUT[name][1]), lambda i, c=col(name): (i, c))
    full = lambda a: pl.BlockSpec(a.shape, lambda i: (0,) * a.ndim)
    pos_spec = pl.BlockSpec((tm, LANES), lambda i: (i % (seq // tm), 0))
    gq = jnp.tile(q_gain.astype(F32), A_HEADS).reshape(1, wide)
    gk = jnp.tile(k_gain.astype(F32), A_KV_HEADS).reshape(1, LANES)
    qn, iqn, k4, v4, ik2, iwn = pl.pallas_call(
        _dsa_prep_kernel,
        grid=(n // tm,),
        in_specs=[hspec('q'), hspec('iq'), hspec('k'), hspec('v'), hspec('ik'), hspec('iw'),
                  pos_spec, pos_spec, full(gq), full(gk), full(hsum), full(rot), full(swap)],
        out_specs=[pl.BlockSpec((tm, wide), lambda i: (i, 0)),
                   pl.BlockSpec((tm, wide), lambda i: (i, 0)),
                   pl.BlockSpec((4, tm, LANES), lambda i: (0, i, 0)),
                   pl.BlockSpec((4, tm, LANES), lambda i: (0, i, 0)),
                   pl.BlockSpec((2, tm, LANES), lambda i: (0, i, 0)),
                   pl.BlockSpec((tm, LANES), lambda i: (i, 0))],
        out_shape=[jax.ShapeDtypeStruct((n, wide), BF16), jax.ShapeDtypeStruct((n, wide), BF16),
                   jax.ShapeDtypeStruct((4, n, LANES), BF16), jax.ShapeDtypeStruct((4, n, LANES), BF16),
                   jax.ShapeDtypeStruct((2, n, LANES), BF16), jax.ShapeDtypeStruct((n, LANES), F32)],
        compiler_params=pltpu.CompilerParams(
            dimension_semantics=("arbitrary",), vmem_limit_bytes=VMEM_LIMIT),
        name="dsa_prep",
    )(h, h, h, h, h, h, cos, sin, gq, gk, hsum, rot, swap)

    r = np.arange(DSA_KC)
    tri = jnp.asarray(r[:, None] <= r[None, :], BF16)
    nq = seq // tb
    qspec = pl.BlockSpec((tb, wide), lambda b, i: (b * nq + i, 0))
    seq_spec = lambda lead: pl.BlockSpec((lead, seq, LANES), lambda b, i: (0, b, 0))
    return pl.pallas_call(
        functools.partial(_dsa_attn_kernel, topk=topk),
        grid=(bsz, nq),
        in_specs=[qspec, qspec, pl.BlockSpec((tb, LANES), lambda b, i: (b * nq + i, 0)),
                  seq_spec(4), seq_spec(4), seq_spec(2),
                  pl.BlockSpec(tri.shape, lambda b, i: (0, 0))],
        out_specs=qspec,
        out_shape=jax.ShapeDtypeStruct((n, wide), F32),
        scratch_shapes=[pltpu.VMEM((seq // DSA_KC, tb, DSA_KC), jnp.int32),
                        pltpu.VMEM((A_HEADS, tb, 1), F32), pltpu.VMEM((A_HEADS, tb, 1), F32),
                        pltpu.VMEM((A_HEADS // 2, tb, LANES), F32)],
        compiler_params=pltpu.CompilerParams(
            dimension_semantics=("arbitrary", "arbitrary"), vmem_limit_bytes=VMEM_LIMIT),
        name="dsa_attn",
    )(qn, iqn, iwn, k4, v4, ik2, tri)


def _dsa_attention(q, k, v, iq, ik, iw, q_gain, k_gain, pos):
    bsz, seq = q.shape[:2]
    topk = min(A_TOPK_MAX, seq // 4)
    rep = A_HEADS // A_KV_HEADS
    q = _rope(_rms_norm(q, q_gain), pos).astype(F32) * HEAD_DIM ** -0.5
    k = _rope(_rms_norm(k, k_gain), pos).astype(F32)
    v = v.astype(F32)
    iq = _rope(iq, pos).astype(F32) * IDX_DIM ** -0.5
    ik = _rope(ik[:, :, None, :], pos)[:, :, 0, :].astype(F32)
    iw = iw.astype(F32) * IDX_HEADS ** -0.5
    key_pos = jnp.arange(seq)
    gather = jax.vmap(lambda t, idx: t[idx])

    def block(args):
        qb, iqb, iwb, start = args
        tq = start + jnp.arange(ATTN_BLOCK)
        rel = jax.nn.relu(jnp.einsum('bqhd,bsd->bqhs', iqb, ik))
        score = jnp.einsum('bqh,bqhs->bqs', iwb, rel)
        causal = key_pos[None, :] <= tq[:, None]
        score = jnp.where(causal[None], score, -jnp.inf)
        _, sel = lax.top_k(score, topk)
        valid = sel <= tq[None, :, None]
        k_sel = gather(k, sel)
        v_sel = gather(v, sel)
        qg = qb.reshape(bsz, ATTN_BLOCK, A_KV_HEADS, rep, HEAD_DIM)
        s = jnp.einsum('bqgrd,bqkgd->bqgrk', qg, k_sel)
        p = _masked_softmax(s, valid[:, :, None, None, :])
        o = jnp.einsum('bqgrk,bqkgd->bqgrd', p, v_sel)
        return o.reshape(bsz, ATTN_BLOCK, A_HEADS * HEAD_DIM)

    starts = jnp.arange(seq // ATTN_BLOCK) * ATTN_BLOCK
    out = lax.map(block, (_to_blocks(q, ATTN_BLOCK), _to_blocks(iq, ATTN_BLOCK),
                          _to_blocks(iw, ATTN_BLOCK), starts))
    return out.swapaxes(0, 1).reshape(bsz, seq, A_HEADS * HEAD_DIM)


S5_STATES = B_GROUPS * B_STATE
S5_CHUNK = 64


def _s5_kernel(u_ref, bmat_ref, cmat_ref, a_re_ref, a_im_ref, d_ref, wg_ref, bg_ref, o_ref, h_scr):
    rows = u_ref.shape[0]
    nb = SUBLANES
    ns = S5_STATES

    @pl.when(pl.program_id(0) == 0)
    def _():
        h_scr[0:nb, :] = jnp.zeros((nb, 2 * ns), F32)

    u = u_ref[...]
    h_scr[nb:, :] = jnp.dot(u.astype(BF16), bmat_ref[...], preferred_element_type=F32)
    a_re = a_re_ref[...]
    a_im = a_im_ref[...]

    def step(t, carry):
        prev = pl.multiple_of(t * nb, nb)
        cur = pl.multiple_of(t * nb + nb, nb)
        p_re = h_scr[pl.ds(prev, nb), 0:ns]
        p_im = h_scr[pl.ds(prev, nb), ns:]
        h_scr[pl.ds(cur, nb), 0:ns] = a_re * p_re - a_im * p_im + h_scr[pl.ds(cur, nb), 0:ns]
        h_scr[pl.ds(cur, nb), ns:] = a_re * p_im + a_im * p_re + h_scr[pl.ds(cur, nb), ns:]
        return carry

    lax.fori_loop(0, rows // nb, step, 0)
    h_all = h_scr[nb:, :]
    h_scr[0:nb, :] = h_scr[rows:, :]
    y = jnp.dot(h_all.astype(BF16), cmat_ref[...], preferred_element_type=F32) + d_ref[...] * u
    y = jax.nn.gelu(y)
    gate = jnp.dot(y.astype(BF16), wg_ref[...], preferred_element_type=F32) + bg_ref[...]
    o_ref[...] = y * jax.nn.sigmoid(gate)


def _s5_operators(log_dt, a_re, a_im, b_re, b_im, c_re, c_im):
    dt = jnp.exp(log_dt.astype(F32))[:, None]
    lr, li = a_re.astype(F32), a_im.astype(F32)
    mag = jnp.exp(dt * lr)
    ab_re = mag * jnp.cos(dt * li)
    ab_im = mag * jnp.sin(dt * li)
    den = lr * lr + li * li
    f_re = ((ab_re - 1.0) * lr + ab_im * li) / den
    f_im = (ab_im * lr - (ab_re - 1.0) * li) / den
    br, bi = b_re.astype(F32), b_im.astype(F32)
    bb_re = f_re[..., None] * br - f_im[..., None] * bi
    bb_im = f_re[..., None] * bi + f_im[..., None] * br
    eye = jnp.eye(B_GROUPS, dtype=F32)
    blk_in = lambda m: jnp.einsum('gpc,gh->gchp', m, eye).reshape(B_WIDTH, S5_STATES)
    bmat = jnp.concatenate([blk_in(bb_re), blk_in(bb_im)], axis=1)
    blk_out = lambda m: jnp.einsum('gcp,gh->gphc', m.astype(F32), eye).reshape(S5_STATES, B_WIDTH)
    cmat = jnp.concatenate([blk_out(c_re), -blk_out(c_im)], axis=0)
    bcast = lambda m: jnp.broadcast_to(m.reshape(1, S5_STATES), (SUBLANES, S5_STATES))
    return bmat.astype(BF16), cmat.astype(BF16), bcast(ab_re), bcast(ab_im)


def _s5_mixer(u, log_dt, a_re, a_im, b_re, b_im, c_re, c_im, d_skip, w_glu, b_glu):
    bsz, seq, width = u.shape
    assert bsz == SUBLANES and width == B_WIDTH and seq % S5_CHUNK == 0
    bmat, cmat, ab_re, ab_im = _s5_operators(log_dt, a_re, a_im, b_re, b_im, c_re, c_im)
    u_tm = u.transpose(1, 0, 2).reshape(seq * bsz, width)
    rows = S5_CHUNK * bsz
    full = lambda shape: pl.BlockSpec(shape, lambda i: (0,) * len(shape))
    out = pl.pallas_call(
        _s5_kernel,
        grid=(seq // S5_CHUNK,),
        in_specs=[pl.BlockSpec((rows, width), lambda i: (i, 0)),
                  full(bmat.shape), full(cmat.shape), full(ab_re.shape), full(ab_im.shape),
                  full((1, width)), full((width, width)), full((1, width))],
        out_specs=pl.BlockSpec((rows, width), lambda i: (i, 0)),
        out_shape=jax.ShapeDtypeStruct((seq * bsz, width), F32),
        scratch_shapes=[pltpu.VMEM((rows + bsz, 2 * S5_STATES), F32)],
        compiler_params=pltpu.CompilerParams(
            dimension_semantics=("arbitrary",), vmem_limit_bytes=VMEM_LIMIT),
        name="s5_scan",
    )(u_tm, bmat, cmat, ab_re, ab_im, d_skip.astype(F32).reshape(1, width),
      w_glu.astype(BF16), b_glu.astype(F32).reshape(1, width))
    return out.reshape(seq, bsz, width).transpose(1, 0, 2)


def _s5_ssm(u, log_dt, a_re, a_im, b_re, b_im, c_re, c_im, d_skip, w_glu, b_glu):
    bsz, seq = u.shape[:2]
    uf = u.astype(F32).reshape(bsz, seq, B_GROUPS, B_GROUP)
    dt = jnp.exp(log_dt.astype(F32))[:, None]
    lr, li = a_re.astype(F32), a_im.astype(F32)
    mag = jnp.exp(dt * lr)
    ab_re = mag * jnp.cos(dt * li)
    ab_im = mag * jnp.sin(dt * li)
    den = lr * lr + li * li
    f_re = ((ab_re - 1.0) * lr + ab_im * li) / den
    f_im = (ab_im * lr - (ab_re - 1.0) * li) / den
    br, bi = b_re.astype(F32), b_im.astype(F32)
    bb_re = f_re[..., None] * br - f_im[..., None] * bi
    bb_im = f_re[..., None] * bi + f_im[..., None] * br
    x_re = jnp.einsum('gpc,bsgc->bsgp', bb_re, uf)
    x_im = jnp.einsum('gpc,bsgc->bsgp', bb_im, uf)
    a_re_t = jnp.broadcast_to(ab_re, x_re.shape)
    a_im_t = jnp.broadcast_to(ab_im, x_im.shape)

    def combine(e1, e2):
        a1r, a1i, b1r, b1i = e1
        a2r, a2i, b2r, b2i = e2
        return (a1r * a2r - a1i * a2i, a1r * a2i + a1i * a2r,
                a2r * b1r - a2i * b1i + b2r, a2r * b1i + a2i * b1r + b2i)

    _, _, h_re, h_im = lax.associative_scan(combine, (a_re_t, a_im_t, x_re, x_im), axis=1)
    y = (jnp.einsum('gcp,bsgp->bsgc', c_re.astype(F32), h_re)
         - jnp.einsum('gcp,bsgp->bsgc', c_im.astype(F32), h_im)
         + d_skip.astype(F32) * uf)
    y = jax.nn.gelu(y.reshape(bsz, seq, B_WIDTH))
    return y * jax.nn.sigmoid(y @ w_glu.astype(F32) + b_glu.astype(F32))


def _compress_blocks(t, cmp_pos, w1, w2):
    bsz, seq, groups, dh = t.shape
    n_cmp = (seq - CMP_LEN) // CMP_STRIDE + 1
    idx = jnp.arange(n_cmp)[:, None] * CMP_STRIDE + jnp.arange(CMP_LEN)[None, :]
    blocks = t.astype(F32)[:, idx] + cmp_pos.astype(F32)[:, None, :]
    blocks = blocks.transpose(0, 1, 3, 2, 4).reshape(bsz, n_cmp, groups, CMP_LEN * dh)
    return jax.nn.gelu(blocks @ w1.astype(F32)) @ w2.astype(F32)


def _nsa_attention(q, k_cmp, v_cmp, k_slc, v_slc, k_win, v_win, gate_logits,
                   q_gain, k_gain, cmp_pos, k_w1, k_w2, v_w1, v_w2, pos):
    bsz, seq = q.shape[:2]
    rep = C_HEADS // C_KV_HEADS
    n_cmp = (seq - CMP_LEN) // CMP_STRIDE + 1
    n_slc = seq // SLC_LEN
    top_n = min(SLC_TOPN, n_slc)
    q = _rope(_rms_norm(q, q_gain), pos).astype(F32) * HEAD_DIM ** -0.5
    cmp_end = jnp.arange(n_cmp) * CMP_STRIDE + (CMP_LEN - 1)
    kc = _rope(_rms_norm(_compress_blocks(k_cmp, cmp_pos, k_w1, k_w2), k_gain), cmp_end)
    vc = _compress_blocks(v_cmp, cmp_pos, v_w1, v_w2)
    ks = _rope(_rms_norm(k_slc, k_gain), pos).astype(F32)
    ks = ks.reshape(bsz, n_slc, SLC_LEN, C_KV_HEADS, HEAD_DIM).transpose(0, 3, 1, 2, 4)
    vs = v_slc.astype(F32).reshape(bsz, n_slc, SLC_LEN, C_KV_HEADS, HEAD_DIM).transpose(0, 3, 1, 2, 4)
    pad = ((0, 0), (WIN, 0), (0, 0), (0, 0))
    kw = jnp.pad(_rope(_rms_norm(k_win, k_gain), pos).astype(F32), pad)
    vw = jnp.pad(v_win.astype(F32), pad)
    gates = jax.nn.sigmoid(gate_logits.astype(F32))
    ratio = SLC_LEN // CMP_STRIDE
    span = CMP_LEN // CMP_STRIDE
    off = jnp.arange(n_cmp)[:, None] - ratio * jnp.arange(n_slc)[None, :]
    cmp_to_slc = jnp.maximum(
        jnp.minimum(ratio - 1, off) - jnp.maximum(0, off - span + 1) + 1, 0).astype(F32)
    slc_ids = jnp.arange(n_slc)
    tok_off = jnp.arange(SLC_LEN)
    win_off = jnp.arange(WIN + C_BLOCK)
    gather = jax.vmap(jax.vmap(lambda t, idx: t[idx]))

    def block(args):
        qb, gb, start = args
        tq = start + jnp.arange(C_BLOCK)
        qg = qb.reshape(bsz, C_BLOCK, C_KV_HEADS, rep, HEAD_DIM)
        gb = gb.reshape(bsz, C_BLOCK, C_KV_HEADS, rep, 3)
        s_c = jnp.einsum('bqgrd,bcgd->bqgrc', qg, kc)
        p_c = _masked_softmax(s_c, (cmp_end[None, :] <= tq[:, None])[None, :, None, None, :])
        o_c = jnp.einsum('bqgrc,bcgd->bqgrd', p_c, vc)
        imp = jnp.einsum('bqgrc,cj->bqgj', p_c, cmp_to_slc)
        cur = tq // SLC_LEN
        forced = (slc_ids[None, :] == cur[:, None]) | (slc_ids[None, :] == 0)
        admissible = slc_ids[None, :] <= cur[:, None]
        imp = jnp.where(forced[None, :, None, :], FORCE_SCORE, imp)
        imp = jnp.where(admissible[None, :, None, :], imp, -jnp.inf)
        _, sel = lax.top_k(imp, top_n)
        sel_g = sel.transpose(0, 2, 1, 3)
        k_sel = gather(ks, sel_g)
        v_sel = gather(vs, sel_g)
        s_s = jnp.einsum('bqgrd,bgqnld->bqgrnl', qg, k_sel)
        kpos = sel[..., None] * SLC_LEN + tok_off
        smask = (kpos <= tq[None, :, None, None, None]).reshape(bsz, C_BLOCK, C_KV_HEADS, 1, -1)
        p_s = _masked_softmax(s_s.reshape(bsz, C_BLOCK, C_KV_HEADS, rep, -1), smask)
        o_s = jnp.einsum('bqgrnl,bgqnld->bqgrd', p_s.reshape(s_s.shape), v_sel)
        kwb = lax.dynamic_slice_in_dim(kw, start, WIN + C_BLOCK, axis=1)
        vwb = lax.dynamic_slice_in_dim(vw, start, WIN + C_BLOCK, axis=1)
        wpos = start - WIN + win_off
        wmask = ((wpos[None, :] <= tq[:, None]) & (wpos[None, :] > tq[:, None] - WIN)
                 & (wpos[None, :] >= 0))
        s_w = jnp.einsum('bqgrd,bkgd->bqgrk', qg, kwb)
        p_w = _masked_softmax(s_w, wmask[None, :, None, None, :])
        o_w = jnp.einsum('bqgrk,bkgd->bqgrd', p_w, vwb)
        o = gb[..., 0:1] * o_c + gb[..., 1:2] * o_s + gb[..., 2:3] * o_w
        return o.reshape(bsz, C_BLOCK, C_HEADS * HEAD_DIM)

    starts = jnp.arange(seq // C_BLOCK) * C_BLOCK
    out = lax.map(block, (_to_blocks(q, C_BLOCK), _to_blocks(gates, C_BLOCK), starts))
    return out.swapaxes(0, 1).reshape(bsz, seq, C_HEADS * HEAD_DIM)


def _gla_attention(q, k, v, g_low, r, w_gate, b_gate, o_gain):
    bsz, seq = q.shape[:2]
    n_chunk = seq // GLA_CHUNK
    logit = g_low.astype(F32) @ w_gate.astype(F32) + b_gate.astype(F32)
    log_a = jax.nn.log_sigmoid(logit) / GATE_TAU

    def chunks(t):
        return t.astype(F32).reshape(bsz, n_chunk, GLA_CHUNK, D_HEADS, -1).transpose(0, 3, 1, 2, 4)

    qc = chunks(q) * D_KDIM ** -0.5
    kc, vc, gc = chunks(k), chunks(v), chunks(log_a)
    bcum = jnp.cumsum(gc, axis=3)
    blast = bcum[:, :, :, -1:, :]
    q_dec = qc * jnp.exp(bcum)
    k_inv = kc * jnp.exp(-bcum)
    causal = jnp.tril(jnp.ones((GLA_CHUNK, GLA_CHUNK), dtype=bool))
    att = jnp.where(causal, jnp.einsum('bhncd,bhnsd->bhncs', q_dec, k_inv), 0.0)
    o_intra = jnp.einsum('bhncs,bhnse->bhnce', att, vc)
    upd = jnp.einsum('bhncd,bhnce->bhnde', kc * jnp.exp(blast - bcum), vc)
    decay = jnp.exp(blast[:, :, :, 0, :])

    def step(state, inp):
        dec, u = inp
        return dec[..., None] * state + u, state

    init = jnp.zeros((bsz, D_HEADS, D_KDIM, D_VDIM), F32)
    _, prev = lax.scan(step, init, (jnp.moveaxis(decay, 2, 0), jnp.moveaxis(upd, 2, 0)))
    o_inter = jnp.einsum('bhncd,nbhde->bhnce', q_dec, prev)
    o = (o_intra + o_inter).transpose(0, 2, 3, 1, 4).reshape(bsz, seq, D_HEADS, D_VDIM)
    o = _rms_norm(o, o_gain).reshape(bsz, seq, D_HEADS * D_VDIM)
    return o * jax.nn.silu(r.astype(F32))


PEER_SLOTS = PEER_HEADS * PEER_TOPK
PEER_TT = 128
ROW_TILE = (SUBLANES, LANES)
NEG_INF = float('-inf')


def _top16_rows(s, row_id, big):
    vals, ids = [], []
    for _ in range(PEER_TOPK):
        m = jnp.max(s, axis=0, keepdims=True)
        pick = jnp.min(jnp.where(s == m, row_id, big), axis=0, keepdims=True)
        s = jnp.where(row_id == pick, NEG_INF, s)
        vals.append(m)
        ids.append(pick)
    return vals, ids


def _peer_route_kernel(x_ref, g_ref, wq_ref, sk_ref, xn_ref, idx_ref, gate_ref, v_scr, i_scr):
    tt = x_ref.shape[0]
    x = x_ref[...]
    xn = x * lax.rsqrt(jnp.mean(x * x, axis=-1, keepdims=True) + NORM_EPS) * g_ref[...]
    xn_ref[...] = xn
    xb = xn.astype(BF16)
    key_id = lax.broadcasted_iota(jnp.int32, (PEER_KEYS, tt), 0)

    def sub_scores(hp, carry):
        q = jnp.dot(xb, wq_ref[hp], preferred_element_type=F32).astype(BF16)
        s = lax.dot_general(sk_ref[hp], q, (((1,), (1,)), ((), ())),
                            preferred_element_type=F32)
        vals, ids = _top16_rows(s, key_id, PEER_KEYS)
        v_scr[hp] = jnp.concatenate(vals, axis=0)
        i_scr[hp] = jnp.concatenate(ids, axis=0)
        return carry

    lax.fori_loop(0, 2 * PEER_HEADS, sub_scores, 0)

    n_blk = 10
    r = lax.broadcasted_iota(jnp.int32, (n_blk * SUBLANES, tt), 0)
    blk, b = r // SUBLANES, r % SUBLANES
    a_of_blk = jnp.where(blk < 2, 0, blk - 1)
    pos = jnp.where(blk == 9, (SUBLANES + b) * PEER_TOPK,
                    jnp.where(blk == 1, SUBLANES + b, a_of_blk * PEER_TOPK + b))
    b_max = jnp.where(blk < 3, 7, jnp.where(blk == 3, 4, jnp.where(blk == 4, 3, jnp.where(
        blk == 5, 2, jnp.where(blk < 9, 1, 7)))))
    live = b <= b_max

    def combine(h, carry):
        v1, v2 = v_scr[2 * h], v_scr[2 * h + 1]
        i1, i2 = i_scr[2 * h], i_scr[2 * h + 1]
        lo_v, hi_v = v2[0:SUBLANES], v2[SUBLANES:]
        lo_i, hi_i = i2[0:SUBLANES], i2[SUBLANES:]
        cand = [v1[0:1] + lo_v, v1[0:1] + hi_v]
        cid = [i1[0:1] * PEER_KEYS + lo_i, i1[0:1] * PEER_KEYS + hi_i]
        for a in range(1, SUBLANES):
            cand.append(v1[a:a + 1] + lo_v)
            cid.append(i1[a:a + 1] * PEER_KEYS + lo_i)
        cand.append(v1[SUBLANES:] + v2[0:1])
        cid.append(i1[SUBLANES:] * PEER_KEYS + i2[0:1])
        cand = jnp.where(live, jnp.concatenate(cand, axis=0), NEG_INF)
        cid = jnp.concatenate(cid, axis=0)
        top_s, experts = [], []
        for _ in range(PEER_TOPK):
            m = jnp.max(cand, axis=0, keepdims=True)
            pick = jnp.min(jnp.where(cand == m, pos, PEER_TOPK * PEER_TOPK), axis=0, keepdims=True)
            hit = pos == pick
            experts.append(jnp.max(jnp.where(hit, cid, -1), axis=0, keepdims=True))
            cand = jnp.where(hit, NEG_INF, cand)
            top_s.append(m)
        top_s = jnp.concatenate(top_s, axis=0)
        ex = jnp.exp(top_s - top_s[0:1])
        gate_ref[0, h] = ex / jnp.sum(ex, axis=0, keepdims=True)
        idx_ref[0, h] = jnp.concatenate(experts, axis=0)
        return carry

    lax.fori_loop(0, PEER_HEADS, combine, 0)


def _peer_route(x2d, g_ffn, wq, subkeys):
    n, dm = x2d.shape
    tt = PEER_TT
    half = PEER_QDIM // 2
    wq_t = wq.astype(BF16).reshape(dm, 2 * PEER_HEADS, half).transpose(1, 0, 2)
    sk = subkeys.astype(BF16).reshape(2 * PEER_HEADS, PEER_KEYS, half)
    tile4 = (1, PEER_HEADS, PEER_TOPK, tt)
    return pl.pallas_call(
        _peer_route_kernel,
        grid=(n // tt,),
        in_specs=[pl.BlockSpec((tt, dm), lambda i: (i, 0)),
                  pl.BlockSpec((1, dm), lambda i: (0, 0)),
                  pl.BlockSpec(wq_t.shape, lambda i: (0, 0, 0)),
                  pl.BlockSpec(sk.shape, lambda i: (0, 0, 0))],
        out_specs=[pl.BlockSpec((tt, dm), lambda i: (i, 0)),
                   pl.BlockSpec(tile4, lambda i: (i, 0, 0, 0)),
                   pl.BlockSpec(tile4, lambda i: (i, 0, 0, 0))],
        out_shape=[jax.ShapeDtypeStruct((n, dm), F32),
                   jax.ShapeDtypeStruct((n // tt,) + tile4[1:], jnp.int32),
                   jax.ShapeDtypeStruct((n // tt,) + tile4[1:], F32)],
        scratch_shapes=[pltpu.VMEM((2 * PEER_HEADS, PEER_TOPK, tt), F32),
                        pltpu.VMEM((2 * PEER_HEADS, PEER_TOPK, tt), jnp.int32)],
        compiler_params=pltpu.CompilerParams(
            dimension_semantics=("arbitrary",), vmem_limit_bytes=VMEM_LIMIT),
        name="peer_route",
    )(x2d, g_ffn.reshape(1, dm), wq_t, sk)


def _gather_rows(idx_smem, tab_ref, slab, t, tt):
    for i in range(PEER_SLOTS):
        slab[i * SUBLANES:(i + 1) * SUBLANES, :] = tab_ref[idx_smem[i * tt + t]]


def _peer_up_kernel(idx_hbm, xn_ref, gate_ref, tab_ref, fold_ref, out_ref, idx_smem, slab, rbuf, sem):
    tt = xn_ref.shape[0]
    cp = pltpu.make_async_copy(idx_hbm.at[pl.program_id(0)], idx_smem, sem)
    cp.start()
    cp.wait()
    eye = (lax.broadcasted_iota(jnp.int32, (LANES, LANES), 0)
           == lax.broadcasted_iota(jnp.int32, (LANES, LANES), 1))

    def tok(t, carry):
        _gather_rows(idx_smem, tab_ref, slab, t, tt)
        x16 = jnp.tile(xn_ref[t], (LANES // SUBLANES, 1)).astype(BF16)
        y = lax.dot_general(slab[...], x16, (((1,), (1,)), ((), ())),
                            preferred_element_type=F32)
        y = y.reshape(PEER_SLOTS * SUBLANES // LANES, LANES, LANES)
        rbuf[t] = jnp.sum(jnp.where(eye[None], y, 0.0), axis=1)
        return carry

    lax.fori_loop(0, tt, tok, 0)
    r = rbuf[...].reshape(tt * SUBLANES, LANES)
    r_hi = r.astype(BF16)
    r_lo = (r - r_hi.astype(F32)).astype(BF16)
    h = (jnp.dot(r_hi, fold_ref[...], preferred_element_type=F32)
         + jnp.dot(r_lo, fold_ref[...], preferred_element_type=F32))
    out_ref[...] = gate_ref[...] * jax.nn.gelu(h)


def _peer_down_kernel(idx_hbm, gh_ref, res_ref, tab_ref, spread_ref, out_ref, idx_smem, slab, ghx, sem):
    tt = gh_ref.shape[0]
    cp = pltpu.make_async_copy(idx_hbm.at[pl.program_id(0)], idx_smem, sem)
    cp.start()
    cp.wait()
    ghx[...] = jnp.dot(gh_ref[...].astype(BF16), spread_ref[...], preferred_element_type=F32)
    width = PEER_SLOTS * SUBLANES
    diag = (lax.broadcasted_iota(jnp.int32, (SUBLANES, width), 0)
            == lax.broadcasted_iota(jnp.int32, (SUBLANES, width), 1) % SUBLANES)

    def tok(t, carry):
        _gather_rows(idx_smem, tab_ref, slab, t, tt)
        row = jnp.broadcast_to(ghx[pl.ds(t, 1), :], (SUBLANES, width))
        coef = jnp.where(diag, row, 0.0).astype(BF16)
        out_ref[t] = res_ref[t] + jnp.dot(coef, slab[...], preferred_element_type=F32)
        return carry

    lax.fori_loop(0, tt, tok, 0)


def _peer_table(tab):
    e, dm = tab.shape
    assert dm == SUBLANES * LANES
    return tab.astype(BF16).reshape(e, SUBLANES, LANES)


def _peer_ffn_residual(x2d, g_ffn, wq, subkeys, u_tab, v_tab):
    n, dm = x2d.shape
    tt = PEER_TT
    n_tiles = n // tt
    xn, idx, gate = _peer_route(x2d, g_ffn, wq, subkeys)
    idx_flat = idx.reshape(n_tiles, PEER_SLOTS * tt)
    gate_rows = gate.transpose(0, 3, 1, 2).reshape(n * PEER_HEADS, PEER_TOPK)
    width = PEER_SLOTS * SUBLANES
    lane = np.arange(LANES)
    fold = jnp.asarray(lane[:, None] // SUBLANES == np.arange(PEER_TOPK)[None, :], BF16)
    spread = jnp.asarray(np.arange(PEER_SLOTS)[:, None] == np.arange(width)[None, :] // SUBLANES, BF16)
    params = pltpu.CompilerParams(dimension_semantics=("arbitrary",),
                                  vmem_limit_bytes=PEER_VMEM_LIMIT)
    table_spec = pl.BlockSpec(memory_space=pltpu.VMEM)
    slab_shape = pltpu.VMEM((PEER_SLOTS * SUBLANES, LANES), BF16)
    idx_scratch = pltpu.SMEM((PEER_SLOTS * tt,), jnp.int32)
    row3 = (tt,) + ROW_TILE

    gh = pl.pallas_call(
        _peer_up_kernel,
        grid=(n_tiles,),
        in_specs=[pl.BlockSpec(memory_space=pl.ANY),
                  pl.BlockSpec(row3, lambda i: (i, 0, 0)),
                  pl.BlockSpec((tt * PEER_HEADS, PEER_TOPK), lambda i: (i, 0)),
                  table_spec,
                  pl.BlockSpec((LANES, PEER_TOPK), lambda i: (0, 0))],
        out_specs=pl.BlockSpec((tt * PEER_HEADS, PEER_TOPK), lambda i: (i, 0)),
        out_shape=jax.ShapeDtypeStruct((n * PEER_HEADS, PEER_TOPK), F32),
        scratch_shapes=[idx_scratch, slab_shape, pltpu.VMEM(row3, F32), pltpu.SemaphoreType.DMA(())],
        compiler_params=params,
        name="peer_up",
    )(idx_flat, xn.reshape((n,) + ROW_TILE), gate_rows, _peer_table(u_tab), fold)

    out = pl.pallas_call(
        _peer_down_kernel,
        grid=(n_tiles,),
        in_specs=[pl.BlockSpec(memory_space=pl.ANY),
                  pl.BlockSpec((tt, PEER_SLOTS), lambda i: (i, 0)),
                  pl.BlockSpec(row3, lambda i: (i, 0, 0)),
                  table_spec,
                  pl.BlockSpec((PEER_SLOTS, width), lambda i: (0, 0))],
        out_specs=pl.BlockSpec(row3, lambda i: (i, 0, 0)),
        out_shape=jax.ShapeDtypeStruct((n,) + ROW_TILE, F32),
        scratch_shapes=[idx_scratch, slab_shape, pltpu.VMEM((tt, width), F32), pltpu.SemaphoreType.DMA(())],
        compiler_params=params,
        name="peer_down",
    )(idx_flat, gh.reshape(n, PEER_SLOTS), x2d.reshape((n,) + ROW_TILE), _peer_table(v_tab), spread)
    return out.reshape(n, dm)


def _even_layer(x, pos, mix_norm, w_in, q_gain, k_gain, log_dt, a_re, a_im, b_re, b_im,
                c_re, c_im, d_skip, w_glu, b_glu, w_out):
    bsz, seq, dm = x.shape
    x2d = x.reshape(bsz * seq, dm)
    pieces = dict(zip(('q', 'k', 'v', 'iq', 'ik', 'iw', 'u'), _split_cols(w_in, EVEN_COLS)))
    w_lay = jnp.concatenate(
        [jnp.pad(pieces[name], ((0, 0), (0, EVEN_LAYOUT[name][1] - pieces[name].shape[1])))
         for name in sorted(EVEN_LAYOUT, key=lambda s: EVEN_LAYOUT[s][0])], axis=1)
    h = _norm_proj(x2d, mix_norm, w_lay)
    o_a = _dsa_mixer(h, bsz, seq, q_gain, k_gain)
    u0, uw = EVEN_LAYOUT['u']
    u = h[:, u0:u0 + uw].reshape(bsz, seq, uw)
    o_b = _s5_mixer(u, log_dt, a_re, a_im, b_re, b_im, c_re, c_im, d_skip, w_glu, b_glu)
    mix = jnp.concatenate([o_a, o_b.reshape(bsz * seq, uw)], axis=-1)
    return _proj_residual(mix, w_out, x2d).reshape(bsz, seq, dm)


def _odd_layer(x, pos, mix_norm, w_in, q_gain, k_gain, cmp_pos, k_w1, k_w2, v_w1, v_w2,
               w_gate, b_gate, o_gain, w_out):
    bsz, seq, dm = x.shape
    x2d = x.reshape(bsz * seq, dm)
    h = _norm_proj(x2d, mix_norm, w_in).reshape(bsz, seq, -1)
    (q, kc, vc, ks, vs, kw, vw, gl, gq, gk, gv, glow, gr) = _split_cols(h, ODD_COLS)
    heads = lambda t, n: t.reshape(bsz, seq, n, -1)
    o_c = _nsa_attention(heads(q, C_HEADS), heads(kc, C_KV_HEADS), heads(vc, C_KV_HEADS),
                         heads(ks, C_KV_HEADS), heads(vs, C_KV_HEADS), heads(kw, C_KV_HEADS),
                         heads(vw, C_KV_HEADS), heads(gl, C_HEADS), q_gain, k_gain,
                         cmp_pos, k_w1, k_w2, v_w1, v_w2, pos)
    o_d = _gla_attention(gq, gk, gv, glow, gr, w_gate, b_gate, o_gain)
    mix = jnp.concatenate([o_c, o_d], axis=-1).reshape(bsz * seq, -1)
    return _proj_residual(mix, w_out, x2d).reshape(bsz, seq, dm)


def kernel(x, l0_mix_norm, l0_w_in, l0_a_q_gain, l0_a_k_gain, l0_s5_log_dt, l0_s5_a_re, l0_s5_a_im, l0_s5_b_re, l0_s5_b_im, l0_s5_c_re, l0_s5_c_im, l0_s5_d, l0_s5_w_glu, l0_s5_b_glu, l0_w_out, l0_ffn_norm, l0_peer_wq, l0_peer_subkeys, l0_peer_u, l0_peer_v, l1_mix_norm, l1_w_in, l1_c_q_gain, l1_c_k_gain, l1_nsa_cmp_pos, l1_nsa_k_w1, l1_nsa_k_w2, l1_nsa_v_w1, l1_nsa_v_w2, l1_gla_w_gate, l1_gla_b_gate, l1_gla_o_gain, l1_w_out, l1_ffn_norm, l1_peer_wq, l1_peer_subkeys, l1_peer_u, l1_peer_v):
    seq = x.shape[1]
    pos = jnp.arange(seq, dtype=jnp.int32)
    x = _even_layer(x, pos, l0_mix_norm, l0_w_in, l0_a_q_gain, l0_a_k_gain, l0_s5_log_dt,
                    l0_s5_a_re, l0_s5_a_im, l0_s5_b_re, l0_s5_b_im, l0_s5_c_re, l0_s5_c_im,
                    l0_s5_d, l0_s5_w_glu, l0_s5_b_glu, l0_w_out)
    shape = x.shape
    flat = lambda t: t.reshape(-1, shape[-1])
    x = _peer_ffn_residual(flat(x), l0_ffn_norm, l0_peer_wq, l0_peer_subkeys,
                           l0_peer_u, l0_peer_v).reshape(shape)
    x = _odd_layer(x, pos, l1_mix_norm, l1_w_in, l1_c_q_gain, l1_c_k_gain, l1_nsa_cmp_pos,
                   l1_nsa_k_w1, l1_nsa_k_w2, l1_nsa_v_w1, l1_nsa_v_w2, l1_gla_w_gate,
                   l1_gla_b_gate, l1_gla_o_gain, l1_w_out)
    return _peer_ffn_residual(flat(x), l1_ffn_norm, l1_peer_wq, l1_peer_subkeys,
                              l1_peer_u, l1_peer_v).reshape(shape)
```

```python
import functools
import math

import jax
import jax.numpy as jnp
import numpy as np
from jax import lax
from jax.experimental import pallas as pl
from jax.experimental.pallas import tpu as pltpu

F32 = jnp.float32
BF16 = jnp.bfloat16
NORM_EPS = 1e-6
ROPE_THETA = 10000.0
HEAD_DIM = 64
ATTN_BLOCK = 128
A_HEADS = 8
A_KV_HEADS = 2
IDX_HEADS = 8
IDX_DIM = 64
A_TOPK_MAX = 256
B_WIDTH = 512
B_GROUP = 16
B_GROUPS = B_WIDTH // B_GROUP
B_STATE = 64
C_HEADS = 8
C_KV_HEADS = 2
CMP_LEN = 32
CMP_STRIDE = 16
CMP_HIDDEN = 256
SLC_LEN = 64
SLC_TOPN = 16
WIN = 512
C_BLOCK = 64
FORCE_SCORE = 1e9
D_HEADS = 4
D_KDIM = 64
D_VDIM = 128
GATE_RANK = 16
GATE_TAU = 16.0
GLA_CHUNK = 32
PEER_HEADS = 8
PEER_KEYS = 128
PEER_QDIM = 256
PEER_TOPK = 16
PEER_BLOCK = 128

EVEN_COLS = (A_HEADS * HEAD_DIM, A_KV_HEADS * HEAD_DIM, A_KV_HEADS * HEAD_DIM,
             IDX_HEADS * IDX_DIM, IDX_DIM, IDX_HEADS, B_WIDTH)
ODD_COLS = (C_HEADS * HEAD_DIM,) + (C_KV_HEADS * HEAD_DIM,) * 6 + (
    C_HEADS * 3, D_HEADS * D_KDIM, D_HEADS * D_KDIM, D_HEADS * D_VDIM, GATE_RANK, D_HEADS * D_VDIM)

LANES = 128
SUBLANES = 8
MXU_DIM = 256
VMEM_LIMIT = 48 * 1024 * 1024
PEER_VMEM_LIMIT = 56 * 1024 * 1024


def _norm_proj_kernel(x_ref, g_ref, w_ref, o_ref):
    x = x_ref[...]
    y = x * lax.rsqrt(jnp.mean(x * x, axis=-1, keepdims=True) + NORM_EPS) * g_ref[...]
    o_ref[...] = jnp.dot(y.astype(BF16), w_ref[...].astype(BF16), preferred_element_type=F32)


def _norm_proj(x2d, gain, w, tm=512, tn=512):
    m, k = x2d.shape
    n = w.shape[1]
    n_pad = -(-n // tn) * tn
    w_p = jnp.pad(w, ((0, 0), (0, n_pad - n)))
    out = pl.pallas_call(
        _norm_proj_kernel,
        grid=(m // tm, n_pad // tn),
        in_specs=[pl.BlockSpec((tm, k), lambda i, j: (i, 0)),
                  pl.BlockSpec((1, k), lambda i, j: (0, 0)),
                  pl.BlockSpec((k, tn), lambda i, j: (0, j))],
        out_specs=pl.BlockSpec((tm, tn), lambda i, j: (i, j)),
        out_shape=jax.ShapeDtypeStruct((m, n_pad), F32),
        compiler_params=pltpu.CompilerParams(
            dimension_semantics=("arbitrary", "arbitrary"), vmem_limit_bytes=VMEM_LIMIT),
    )(x2d, gain.reshape(1, k), w_p)
    return out[:, :n]


def _proj_res_kernel(a_ref, w_ref, r_ref, o_ref):
    o_ref[...] = r_ref[...] + jnp.dot(a_ref[...].astype(BF16), w_ref[...].astype(BF16),
                                      preferred_element_type=F32)


def _proj_residual(a2d, w, res2d, tm=512):
    m, k = a2d.shape
    n = w.shape[1]
    return pl.pallas_call(
        _proj_res_kernel,
        grid=(m // tm,),
        in_specs=[pl.BlockSpec((tm, k), lambda i: (i, 0)),
                  pl.BlockSpec((k, n), lambda i: (0, 0)),
                  pl.BlockSpec((tm, n), lambda i: (i, 0))],
        out_specs=pl.BlockSpec((tm, n), lambda i: (i, 0)),
        out_shape=jax.ShapeDtypeStruct((m, n), F32),
        compiler_params=pltpu.CompilerParams(
            dimension_semantics=("arbitrary",), vmem_limit_bytes=VMEM_LIMIT),
    )(a2d, w, res2d)


def _rms_norm(x, gain):
    xf = x.astype(F32)
    y = xf * lax.rsqrt(jnp.mean(xf * xf, axis=-1, keepdims=True) + NORM_EPS)
    return (y * gain.astype(F32)).astype(x.dtype)


def _rope(x, pos):
    half = x.shape[-1] // 2
    inv_freq = ROPE_THETA ** (-jnp.arange(half, dtype=F32) / half)
    ang = pos.astype(F32)[:, None] * inv_freq[None, :]
    cos = jnp.cos(ang)[:, None, :]
    sin = jnp.sin(ang)[:, None, :]
    xf = x.astype(F32)
    x1, x2 = xf[..., :half], xf[..., half:]
    return jnp.concatenate([x1 * cos - x2 * sin, x1 * sin + x2 * cos], axis=-1).astype(x.dtype)


def _masked_softmax(s, mask):
    s = jnp.where(mask, s.astype(F32), -1e30)
    m = jnp.max(s, axis=-1, keepdims=True)
    p = jnp.exp(s - m) * mask
    return p / jnp.maximum(jnp.sum(p, axis=-1, keepdims=True), 1e-30)


def _split_cols(h, sizes):
    return jnp.split(h, np.cumsum(sizes)[:-1].tolist(), axis=-1)


def _to_blocks(t, blk):
    return t.reshape(t.shape[0], t.shape[1] // blk, blk, *t.shape[2:]).swapaxes(0, 1)


INT_MIN = -2 ** 31
MASKED = -1e30
DSA_KC = 512
EVEN_LAYOUT = dict(q=(0, 512), iq=(512, 512), u=(1024, 512), k=(1536, 128), v=(1664, 128),
                   ik=(1792, 128), iw=(1920, 128))
EVEN_WIDTH = 2048


def _exact_dot(x, m):
    hi = x.astype(BF16)
    lo = (x - hi.astype(F32)).astype(BF16)
    return (jnp.dot(hi, m, preferred_element_type=F32) + jnp.dot(lo, m, preferred_element_type=F32))


def _head_rms(x, gain, hsum):
    return x * lax.rsqrt(_exact_dot(x * x, hsum) + NORM_EPS) * gain


def _head_rope(x, cos, sin, rot):
    return x * cos + _exact_dot(x, rot) * sin


def _split_pair(x, swap):
    lane = lax.broadcasted_iota(jnp.int32, x.shape, 1)
    low = lane < HEAD_DIM
    xs = jnp.dot(x, swap, preferred_element_type=F32).astype(BF16)
    zero = jnp.zeros_like(x)
    return (jnp.where(low, x, zero), jnp.where(low, zero, xs),
            jnp.where(low, xs, zero), jnp.where(low, zero, x))


def _dsa_prep_kernel(q_ref, iq_ref, k_ref, v_ref, ik_ref, iw_ref, cos_ref, sin_ref, qg_ref, kg_ref,
                     hsum_ref, rot_ref, swap_ref, qn_ref, iqn_ref, k4_ref, v4_ref, ik2_ref, iwn_ref):
    cos1, sin1 = cos_ref[...], sin_ref[...]
    reps = q_ref.shape[1] // LANES
    cos4, sin4 = jnp.tile(cos1, (1, reps)), jnp.tile(sin1, (1, reps))
    hsum, rot = hsum_ref[...], rot_ref[...]
    hsum1, rot1 = hsum[0:LANES, 0:LANES], rot[0:LANES, 0:LANES]
    scale = HEAD_DIM ** -0.5
    q = _head_rope(_head_rms(q_ref[...], qg_ref[...], hsum), cos4, sin4, rot) * scale
    qn_ref[...] = q.astype(BF16)
    iqn_ref[...] = (_head_rope(iq_ref[...], cos4, sin4, rot) * IDX_DIM ** -0.5).astype(BF16)
    k = _head_rope(_head_rms(k_ref[...], kg_ref[...], hsum1), cos1, sin1, rot1).astype(BF16)
    swap = swap_ref[...]
    for n, part in enumerate(_split_pair(k, swap)):
        k4_ref[n] = part
    for n, part in enumerate(_split_pair(v_ref[...].astype(BF16), swap)):
        v4_ref[n] = part
    ik = _head_rope(ik_ref[...], cos1, sin1, rot1).astype(BF16)
    ik_a, ik_b, _, _ = _split_pair(ik, swap)
    ik2_ref[0] = ik_a
    ik2_ref[1] = ik_b
    iwn_ref[...] = iw_ref[...] * IDX_HEADS ** -0.5


def _dot_nt(a, b):
    return lax.dot_general(a, b, (((1,), (1,)), ((), ())), preferred_element_type=F32)


def _dsa_attn_kernel(q_ref, iq_ref, iw_ref, k4_ref, v4_ref, ik2_ref, tri_ref, o_ref,
                     key_scr, m_scr, l_scr, acc_scr, *, topk):
    tb = q_ref.shape[0]
    kc = DSA_KC
    qi = pl.program_id(1)
    n_ch = qi // (kc // tb) + 1
    tq = qi * tb + lax.broadcasted_iota(jnp.int32, (tb, 1), 0)
    lane_k = lax.broadcasted_iota(jnp.int32, (tb, kc), 1)
    iw = iw_ref[...]

    def score_chunk(c, carry):
        ks = pl.multiple_of(c * kc, kc)
        ik_a = ik2_ref[0, pl.ds(ks, kc), :]
        ik_b = ik2_ref[1, pl.ds(ks, kc), :]
        sc = jnp.zeros((tb, kc), F32)
        for j in range(IDX_HEADS // 2):
            iqp = iq_ref[:, j * LANES:(j + 1) * LANES]
            sc = sc + iw[:, 2 * j:2 * j + 1] * jnp.maximum(_dot_nt(iqp, ik_a), 0.0)
            sc = sc + iw[:, 2 * j + 1:2 * j + 2] * jnp.maximum(_dot_nt(iqp, ik_b), 0.0)
        sc = jnp.where(sc == 0.0, 0.0, sc)
        bits = lax.bitcast_convert_type(sc, jnp.int32)
        key = bits ^ (jnp.right_shift(bits, 31) & 0x7FFFFFFF)
        key_scr[c] = jnp.where(ks + lane_k <= tq, key, INT_MIN)
        return carry

    lax.fori_loop(0, n_ch, score_chunk, 0)

    def count(pred_of_key):
        def body(c, acc):
            hit = jnp.where(pred_of_key(key_scr[c]), 1.0, 0.0)
            for s in range(kc // LANES):
                acc = acc + hit[:, s * LANES:(s + 1) * LANES]
            return acc
        acc = lax.fori_loop(0, n_ch, body, jnp.zeros((tb, LANES), F32))
        return jnp.sum(acc, axis=1, keepdims=True)

    def bit_step(i, t_u):
        cand = t_u | jnp.left_shift(jnp.int32(1), 31 - i)
        cand_s = cand ^ INT_MIN
        return jnp.where(count(lambda key: key >= cand_s) >= topk, cand, t_u)

    thr = lax.fori_loop(0, 32, bit_step, jnp.zeros((tb, 1), jnp.int32)) ^ INT_MIN
    need = topk - count(lambda key: key > thr)

    m_scr[...] = jnp.full(m_scr.shape, MASKED, F32)
    l_scr[...] = jnp.zeros(l_scr.shape, F32)
    acc_scr[...] = jnp.zeros(acc_scr.shape, F32)
    low = lax.broadcasted_iota(jnp.int32, (tb, LANES), 1) < HEAD_DIM

    def attn_chunk(c, ties_before):
        ks = pl.multiple_of(c * kc, kc)
        key = key_scr[c]
        eq = jnp.where(key == thr, 1.0, 0.0)
        rank = ties_before + jnp.dot(eq.astype(BF16), tri_ref[...], preferred_element_type=F32)
        tie_ok = jnp.where(rank <= need, eq, 0.0)
        sel = jnp.where(ks + lane_k <= tq, jnp.where(key > thr, 1.0, tie_ok), 0.0)
        chosen = sel > 0.0

        def head(s, h):
            sm = jnp.where(chosen, s, MASKED)
            m_old = m_scr[h]
            m_new = jnp.maximum(m_old, jnp.max(sm, axis=1, keepdims=True))
            p = jnp.exp(sm - m_new) * sel
            alpha = jnp.exp(m_old - m_new)
            l_scr[h] = alpha * l_scr[h] + jnp.sum(p, axis=1, keepdims=True)
            m_scr[h] = m_new
            return p.astype(BF16), alpha

        for j in range(A_HEADS // 2):
            g = (2 * j) // (A_HEADS // A_KV_HEADS)
            qp = q_ref[:, j * LANES:(j + 1) * LANES]
            k_a = k4_ref[2 * g, pl.ds(ks, kc), :]
            k_b = k4_ref[2 * g + 1, pl.ds(ks, kc), :]
            p_a, alpha_a = head(_dot_nt(qp, k_a), 2 * j)
            p_b, alpha_b = head(_dot_nt(qp, k_b), 2 * j + 1)
            pv = (jnp.dot(p_a, v4_ref[2 * g, pl.ds(ks, kc), :], preferred_element_type=F32)
                  + jnp.dot(p_b, v4_ref[2 * g + 1, pl.ds(ks, kc), :], preferred_element_type=F32))
            acc_scr[j] = jnp.where(low, alpha_a, alpha_b) * acc_scr[j] + pv
        return ties_before + jnp.sum(eq, axis=1, keepdims=True)

    lax.fori_loop(0, n_ch, attn_chunk, jnp.zeros((tb, 1), F32))
    for j in range(A_HEADS // 2):
        denom = jnp.maximum(jnp.where(low, l_scr[2 * j], l_scr[2 * j + 1]), 1e-30)
        o_ref[:, j * LANES:(j + 1) * LANES] = acc_scr[j] / denom


def _rope_tables(seq, width):
    half = HEAD_DIM // 2
    inv_freq = ROPE_THETA ** (-jnp.arange(half, dtype=F32) / half)
    ang = jnp.arange(seq, dtype=F32)[:, None] * inv_freq[None, :]
    reps = width // half
    return jnp.tile(jnp.cos(ang), (1, reps)), jnp.tile(jnp.sin(ang), (1, reps))


def _head_matrices(width):
    i = np.arange(width)
    same = (i[:, None] // HEAD_DIM) == (i[None, :] // HEAD_DIM)
    hsum = np.where(same, 1.0 / HEAD_DIM, 0.0)
    half = HEAD_DIM // 2
    src, dst = i[:, None], i[None, :]
    rot = np.where(same & (src == dst + half), -1.0, 0.0) + np.where(same & (src == dst - half), 1.0, 0.0)
    j = np.arange(LANES)
    swap = (j[:, None] == (j[None, :] + HEAD_DIM) % LANES).astype(np.float32)
    return jnp.asarray(hsum, BF16), jnp.asarray(rot, BF16), jnp.asarray(swap, BF16)


def _dsa_mixer(h, bsz, seq, q_gain, k_gain, tm=512):
    n = bsz * seq
    tb = ATTN_BLOCK
    assert seq % DSA_KC == 0 and seq % tm == 0
    topk = min(A_TOPK_MAX, seq // 4)
    cos, sin = _rope_tables(seq, LANES)
    hsum, rot, swap = _head_matrices(A_HEADS * HEAD_DIM)
    wide = A_HEADS * HEAD_DIM
    col = lambda name: EVEN_LAYOUT[name][0] // EVEN_LAYOUT[name][1]
    hspec = lambda name: pl.BlockSpec((tm, EVEN_LAYOUT[name][1]), lambda i, c=col(name): (i, c))
    full = lambda a: pl.BlockSpec(a.shape, lambda i: (0,) * a.ndim)
    pos_spec = pl.BlockSpec((tm, LANES), lambda i: (i % (seq // tm), 0))
    gq = jnp.tile(q_gain.astype(F32), A_HEADS).reshape(1, wide)
    gk = jnp.tile(k_gain.astype(F32), A_KV_HEADS).reshape(1, LANES)
    qn, iqn, k4, v4, ik2, iwn = pl.pallas_call(
        _dsa_prep_kernel,
        grid=(n // tm,),
        in_specs=[hspec('q'), hspec('iq'), hspec('k'), hspec('v'), hspec('ik'), hspec('iw'),
                  pos_spec, pos_spec, full(gq), full(gk), full(hsum), full(rot), full(swap)],
        out_specs=[pl.BlockSpec((tm, wide), lambda i: (i, 0)),
                   pl.BlockSpec((tm, wide), lambda i: (i, 0)),
                   pl.BlockSpec((4, tm, LANES), lambda i: (0, i, 0)),
                   pl.BlockSpec((4, tm, LANES), lambda i: (0, i, 0)),
                   pl.BlockSpec((2, tm, LANES), lambda i: (0, i, 0)),
                   pl.BlockSpec((tm, LANES), lambda i: (i, 0))],
        out_shape=[jax.ShapeDtypeStruct((n, wide), BF16), jax.ShapeDtypeStruct((n, wide), BF16),
                   jax.ShapeDtypeStruct((4, n, LANES), BF16), jax.ShapeDtypeStruct((4, n, LANES), BF16),
                   jax.ShapeDtypeStruct((2, n, LANES), BF16), jax.ShapeDtypeStruct((n, LANES), F32)],
        compiler_params=pltpu.CompilerParams(
            dimension_semantics=("arbitrary",), vmem_limit_bytes=VMEM_LIMIT),
        name="dsa_prep",
    )(h, h, h, h, h, h, cos, sin, gq, gk, hsum, rot, swap)

    r = np.arange(DSA_KC)
    tri = jnp.asarray(r[:, None] <= r[None, :], BF16)
    nq = seq // tb
    qspec = pl.BlockSpec((tb, wide), lambda b, i: (b * nq + i, 0))
    seq_spec = lambda lead: pl.BlockSpec((lead, seq, LANES), lambda b, i: (0, b, 0))
    return pl.pallas_call(
        functools.partial(_dsa_attn_kernel, topk=topk),
        grid=(bsz, nq),
        in_specs=[qspec, qspec, pl.BlockSpec((tb, LANES), lambda b, i: (b * nq + i, 0)),
                  seq_spec(4), seq_spec(4), seq_spec(2),
                  pl.BlockSpec(tri.shape, lambda b, i: (0, 0))],
        out_specs=qspec,
        out_shape=jax.ShapeDtypeStruct((n, wide), F32),
        scratch_shapes=[pltpu.VMEM((seq // DSA_KC, tb, DSA_KC), jnp.int32),
                        pltpu.VMEM((A_HEADS, tb, 1), F32), pltpu.VMEM((A_HEADS, tb, 1), F32),
                        pltpu.VMEM((A_HEADS // 2, tb, LANES), F32)],
        compiler_params=pltpu.CompilerParams(
            dimension_semantics=("arbitrary", "arbitrary"), vmem_limit_bytes=VMEM_LIMIT),
        name="dsa_attn",
    )(qn, iqn, iwn, k4, v4, ik2, tri)


def _dsa_attention(q, k, v, iq, ik, iw, q_gain, k_gain, pos):
    bsz, seq = q.shape[:2]
    topk = min(A_TOPK_MAX, seq // 4)
    rep = A_HEADS // A_KV_HEADS
    q = _rope(_rms_norm(q, q_gain), pos).astype(F32) * HEAD_DIM ** -0.5
    k = _rope(_rms_norm(k, k_gain), pos).astype(F32)
    v = v.astype(F32)
    iq = _rope(iq, pos).astype(F32) * IDX_DIM ** -0.5
    ik = _rope(ik[:, :, None, :], pos)[:, :, 0, :].astype(F32)
    iw = iw.astype(F32) * IDX_HEADS ** -0.5
    key_pos = jnp.arange(seq)
    gather = jax.vmap(lambda t, idx: t[idx])

    def block(args):
        qb, iqb, iwb, start = args
        tq = start + jnp.arange(ATTN_BLOCK)
        rel = jax.nn.relu(jnp.einsum('bqhd,bsd->bqhs', iqb, ik))
        score = jnp.einsum('bqh,bqhs->bqs', iwb, rel)
        causal = key_pos[None, :] <= tq[:, None]
        score = jnp.where(causal[None], score, -jnp.inf)
        _, sel = lax.top_k(score, topk)
        valid = sel <= tq[None, :, None]
        k_sel = gather(k, sel)
        v_sel = gather(v, sel)
        qg = qb.reshape(bsz, ATTN_BLOCK, A_KV_HEADS, rep, HEAD_DIM)
        s = jnp.einsum('bqgrd,bqkgd->bqgrk', qg, k_sel)
        p = _masked_softmax(s, valid[:, :, None, None, :])
        o = jnp.einsum('bqgrk,bqkgd->bqgrd', p, v_sel)
        return o.reshape(bsz, ATTN_BLOCK, A_HEADS * HEAD_DIM)

    starts = jnp.arange(seq // ATTN_BLOCK) * ATTN_BLOCK
    out = lax.map(block, (_to_blocks(q, ATTN_BLOCK), _to_blocks(iq, ATTN_BLOCK),
                          _to_blocks(iw, ATTN_BLOCK), starts))
    return out.swapaxes(0, 1).reshape(bsz, seq, A_HEADS * HEAD_DIM)


S5_STATES = B_GROUPS * B_STATE
S5_CHUNK = 64


def _s5_kernel(u_ref, bmat_ref, cmat_ref, a_re_ref, a_im_ref, d_ref, wg_ref, bg_ref, o_ref, h_scr):
    rows = u_ref.shape[0]
    nb = SUBLANES
    ns = S5_STATES

    @pl.when(pl.program_id(0) == 0)
    def _():
        h_scr[0:nb, :] = jnp.zeros((nb, 2 * ns), F32)

    u = u_ref[...]
    h_scr[nb:, :] = jnp.dot(u.astype(BF16), bmat_ref[...], preferred_element_type=F32)
    a_re = a_re_ref[...]
    a_im = a_im_ref[...]

    def step(t, carry):
        prev = pl.multiple_of(t * nb, nb)
        cur = pl.multiple_of(t * nb + nb, nb)
        p_re = h_scr[pl.ds(prev, nb), 0:ns]
        p_im = h_scr[pl.ds(prev, nb), ns:]
        h_scr[pl.ds(cur, nb), 0:ns] = a_re * p_re - a_im * p_im + h_scr[pl.ds(cur, nb), 0:ns]
        h_scr[pl.ds(cur, nb), ns:] = a_re * p_im + a_im * p_re + h_scr[pl.ds(cur, nb), ns:]
        return carry

    lax.fori_loop(0, rows // nb, step, 0)
    h_all = h_scr[nb:, :]
    h_scr[0:nb, :] = h_scr[rows:, :]
    y = jnp.dot(h_all.astype(BF16), cmat_ref[...], preferred_element_type=F32) + d_ref[...] * u
    y = jax.nn.gelu(y)
    gate = jnp.dot(y.astype(BF16), wg_ref[...], preferred_element_type=F32) + bg_ref[...]
    o_ref[...] = y * jax.nn.sigmoid(gate)


def _s5_operators(log_dt, a_re, a_im, b_re, b_im, c_re, c_im):
    dt = jnp.exp(log_dt.astype(F32))[:, None]
    lr, li = a_re.astype(F32), a_im.astype(F32)
    mag = jnp.exp(dt * lr)
    ab_re = mag * jnp.cos(dt * li)
    ab_im = mag * jnp.sin(dt * li)
    den = lr * lr + li * li
    f_re = ((ab_re - 1.0) * lr + ab_im * li) / den
    f_im = (ab_im * lr - (ab_re - 1.0) * li) / den
    br, bi = b_re.astype(F32), b_im.astype(F32)
    bb_re = f_re[..., None] * br - f_im[..., None] * bi
    bb_im = f_re[..., None] * bi + f_im[..., None] * br
    eye = jnp.eye(B_GROUPS, dtype=F32)
    blk_in = lambda m: jnp.einsum('gpc,gh->gchp', m, eye).reshape(B_WIDTH, S5_STATES)
    bmat = jnp.concatenate([blk_in(bb_re), blk_in(bb_im)], axis=1)
    blk_out = lambda m: jnp.einsum('gcp,gh->gphc', m.astype(F32), eye).reshape(S5_STATES, B_WIDTH)
    cmat = jnp.concatenate([blk_out(c_re), -blk_out(c_im)], axis=0)
    bcast = lambda m: jnp.broadcast_to(m.reshape(1, S5_STATES), (SUBLANES, S5_STATES))
    return bmat.astype(BF16), cmat.astype(BF16), bcast(ab_re), bcast(ab_im)


def _s5_mixer(u, log_dt, a_re, a_im, b_re, b_im, c_re, c_im, d_skip, w_glu, b_glu):
    bsz, seq, width = u.shape
    assert bsz == SUBLANES and width == B_WIDTH and seq % S5_CHUNK == 0
    bmat, cmat, ab_re, ab_im = _s5_operators(log_dt, a_re, a_im, b_re, b_im, c_re, c_im)
    u_tm = u.transpose(1, 0, 2).reshape(seq * bsz, width)
    rows = S5_CHUNK * bsz
    full = lambda shape: pl.BlockSpec(shape, lambda i: (0,) * len(shape))
    out = pl.pallas_call(
        _s5_kernel,
        grid=(seq // S5_CHUNK,),
        in_specs=[pl.BlockSpec((rows, width), lambda i: (i, 0)),
                  full(bmat.shape), full(cmat.shape), full(ab_re.shape), full(ab_im.shape),
                  full((1, width)), full((width, width)), full((1, width))],
        out_specs=pl.BlockSpec((rows, width), lambda i: (i, 0)),
        out_shape=jax.ShapeDtypeStruct((seq * bsz, width), F32),
        scratch_shapes=[pltpu.VMEM((rows + bsz, 2 * S5_STATES), F32)],
        compiler_params=pltpu.CompilerParams(
            dimension_semantics=("arbitrary",), vmem_limit_bytes=VMEM_LIMIT),
        name="s5_scan",
    )(u_tm, bmat, cmat, ab_re, ab_im, d_skip.astype(F32).reshape(1, width),
      w_glu.astype(BF16), b_glu.astype(F32).reshape(1, width))
    return out.reshape(seq, bsz, width).transpose(1, 0, 2)


def _s5_ssm(u, log_dt, a_re, a_im, b_re, b_im, c_re, c_im, d_skip, w_glu, b_glu):
    bsz, seq = u.shape[:2]
    uf = u.astype(F32).reshape(bsz, seq, B_GROUPS, B_GROUP)
    dt = jnp.exp(log_dt.astype(F32))[:, None]
    lr, li = a_re.astype(F32), a_im.astype(F32)
    mag = jnp.exp(dt * lr)
    ab_re = mag * jnp.cos(dt * li)
    ab_im = mag * jnp.sin(dt * li)
    den = lr * lr + li * li
    f_re = ((ab_re - 1.0) * lr + ab_im * li) / den
    f_im = (ab_im * lr - (ab_re - 1.0) * li) / den
    br, bi = b_re.astype(F32), b_im.astype(F32)
    bb_re = f_re[..., None] * br - f_im[..., None] * bi
    bb_im = f_re[..., None] * bi + f_im[..., None] * br
    x_re = jnp.einsum('gpc,bsgc->bsgp', bb_re, uf)
    x_im = jnp.einsum('gpc,bsgc->bsgp', bb_im, uf)
    a_re_t = jnp.broadcast_to(ab_re, x_re.shape)
    a_im_t = jnp.broadcast_to(ab_im, x_im.shape)

    def combine(e1, e2):
        a1r, a1i, b1r, b1i = e1
        a2r, a2i, b2r, b2i = e2
        return (a1r * a2r - a1i * a2i, a1r * a2i + a1i * a2r,
                a2r * b1r - a2i * b1i + b2r, a2r * b1i + a2i * b1r + b2i)

    _, _, h_re, h_im = lax.associative_scan(combine, (a_re_t, a_im_t, x_re, x_im), axis=1)
    y = (jnp.einsum('gcp,bsgp->bsgc', c_re.astype(F32), h_re)
         - jnp.einsum('gcp,bsgp->bsgc', c_im.astype(F32), h_im)
         + d_skip.astype(F32) * uf)
    y = jax.nn.gelu(y.reshape(bsz, seq, B_WIDTH))
    return y * jax.nn.sigmoid(y @ w_glu.astype(F32) + b_glu.astype(F32))


NSA_KC = 512
NSA_WIN_SPAN = WIN + ATTN_BLOCK


def _nsa_prep_kernel(q_ref, ks_ref, vs_ref, kw_ref, vw_ref, gl_ref, cos_ref, sin_ref, qg_ref, kg_ref,
                     hsum_ref, rot_ref, swap_ref, qn_ref, ks4_ref, vs4_ref, kw4_ref, vw4_ref, gate_ref):
    cos1, sin1 = cos_ref[...], sin_ref[...]
    reps = q_ref.shape[1] // LANES
    cos4, sin4 = jnp.tile(cos1, (1, reps)), jnp.tile(sin1, (1, reps))
    hsum, rot = hsum_ref[...], rot_ref[...]
    hsum1, rot1 = hsum[0:LANES, 0:LANES], rot[0:LANES, 0:LANES]
    swap = swap_ref[...]
    q = _head_rope(_head_rms(q_ref[...], qg_ref[...], hsum), cos4, sin4, rot) * HEAD_DIM ** -0.5
    qn_ref[...] = q.astype(BF16)
    for src, dst, is_key in ((ks_ref, ks4_ref, True), (vs_ref, vs4_ref, False),
                             (kw_ref, kw4_ref, True), (vw_ref, vw4_ref, False)):
        t = src[...]
        if is_key:
            t = _head_rope(_head_rms(t, kg_ref[...], hsum1), cos1, sin1, rot1)
        for n, part in enumerate(_split_pair(t.astype(BF16), swap)):
            dst[n] = part
    gate_ref[...] = jax.nn.sigmoid(gl_ref[...])


def _nsa_cmp_kernel(hk_ref, hv_ref, pos_ref, kw1_ref, kw2_ref, vw1_ref, vw2_ref, cos_ref, sin_ref,
                    kg_ref, hsum_ref, rot_ref, swap_ref, kc_ref, vc_ref):
    def mlp(h_ref, w1_ref, w2_ref):
        h = h_ref[0]
        first = jnp.dot((h + pos_ref[0:1, :]).astype(BF16), w1_ref[0], preferred_element_type=F32)
        second = jnp.dot((h + pos_ref[1:2, :]).astype(BF16), w1_ref[1], preferred_element_type=F32)
        z = jax.nn.gelu(first + pltpu.roll(second, second.shape[0] - 1, axis=0))
        return jnp.dot(z.astype(BF16), w2_ref[...], preferred_element_type=F32)

    swap = swap_ref[...]
    kc = _head_rope(_head_rms(mlp(hk_ref, kw1_ref, kw2_ref), kg_ref[...], hsum_ref[...]),
                    cos_ref[...], sin_ref[...], rot_ref[...])
    kc_a, kc_b, _, _ = _split_pair(kc.astype(BF16), swap)
    kc_ref[0, 0] = kc_a
    kc_ref[0, 1] = kc_b
    vc_a, vc_b, _, _ = _split_pair(mlp(hv_ref, vw1_ref, vw2_ref).astype(BF16), swap)
    vc_ref[0, 0] = vc_a
    vc_ref[0, 1] = vc_b


def _softmax_rows(s, mask):
    sm = jnp.where(mask > 0.0, s, MASKED)
    p = jnp.exp(sm - jnp.max(sm, axis=1, keepdims=True)) * mask
    return p / jnp.maximum(jnp.sum(p, axis=1, keepdims=True), 1e-30)


def _nsa_attn_kernel(q_ref, gate_ref, kc_ref, vc_ref, ks4_ref, vs4_ref, kw4_ref, vw4_ref,
                     m2s_ref, blk_ref, gx_ref, o_ref, m_scr, l_scr, acc_scr, oc_scr, ow_scr, *, top_n):
    tb = q_ref.shape[0]
    kc = NSA_KC
    qi = pl.program_id(1)
    start = qi * tb
    tq = start + lax.broadcasted_iota(jnp.int32, (tb, 1), 0)
    low = lax.broadcasted_iota(jnp.int32, (tb, LANES), 1) < HEAD_DIM
    heads_per_group = C_HEADS // C_KV_HEADS
    pairs = C_HEADS // 2

    n_cmp = kc_ref.shape[3]
    cmp_end = lax.broadcasted_iota(jnp.int32, (tb, n_cmp), 1) * CMP_STRIDE + (CMP_LEN - 1)
    cmask = jnp.where(cmp_end <= tq, 1.0, 0.0)
    p_sum = [jnp.zeros((tb, n_cmp), F32) for _ in range(C_KV_HEADS)]
    for j in range(pairs):
        g = (2 * j) // heads_per_group
        qp = q_ref[:, j * LANES:(j + 1) * LANES]
        p_a = _softmax_rows(_dot_nt(qp, kc_ref[0, g, 0]), cmask)
        p_b = _softmax_rows(_dot_nt(qp, kc_ref[0, g, 1]), cmask)
        p_sum[g] = p_sum[g] + p_a + p_b
        oc_scr[j] = (jnp.dot(p_a.astype(BF16), vc_ref[0, g, 0], preferred_element_type=F32)
                     + jnp.dot(p_b.astype(BF16), vc_ref[0, g, 1], preferred_element_type=F32))

    n_slc = m2s_ref.shape[0]
    slc_id = lax.broadcasted_iota(jnp.int32, (n_slc, tb), 0)
    cur = (start + lax.broadcasted_iota(jnp.int32, (n_slc, tb), 1)) // SLC_LEN
    forced = jnp.where(slc_id == cur, 1.0, jnp.where(slc_id == 0, 1.0, 0.0))
    chosen = []
    for g in range(C_KV_HEADS):
        hi = p_sum[g].astype(BF16)
        lo = (p_sum[g] - hi.astype(F32)).astype(BF16)
        imp = _dot_nt(m2s_ref[...], hi) + _dot_nt(m2s_ref[...], lo)
        imp = jnp.where(forced > 0.0, FORCE_SCORE, imp)
        imp = jnp.where(slc_id <= cur, imp, NEG_INF)
        picked = jnp.zeros((n_slc, tb), F32)
        for _ in range(top_n):
            m = jnp.max(imp, axis=0, keepdims=True)
            first = jnp.min(jnp.where(imp == m, slc_id, n_slc), axis=0, keepdims=True)
            hit = slc_id == first
            picked = jnp.where(hit, 1.0, picked)
            imp = jnp.where(hit, NEG_INF, imp)
        chosen.append(picked.T.astype(BF16))

    m_scr[...] = jnp.full(m_scr.shape, MASKED, F32)
    l_scr[...] = jnp.zeros(l_scr.shape, F32)
    acc_scr[...] = jnp.zeros(acc_scr.shape, F32)
    lane_k = lax.broadcasted_iota(jnp.int32, (tb, kc), 1)

    def slc_chunk(c, carry):
        ks = pl.multiple_of(c * kc, kc)
        causal = ks + lane_k <= tq
        sel = [jnp.where(causal, jnp.dot(chosen[g], blk_ref[c], preferred_element_type=F32), 0.0)
               for g in range(C_KV_HEADS)]

        def head(s, h, g):
            sm = jnp.where(sel[g] > 0.0, s, MASKED)
            m_old = m_scr[h]
            m_new = jnp.maximum(m_old, jnp.max(sm, axis=1, keepdims=True))
            p = jnp.exp(sm - m_new) * sel[g]
            alpha = jnp.exp(m_old - m_new)
            l_scr[h] = alpha * l_scr[h] + jnp.sum(p, axis=1, keepdims=True)
            m_scr[h] = m_new
            return p.astype(BF16), alpha

        for j in range(pairs):
            g = (2 * j) // heads_per_group
            qp = q_ref[:, j * LANES:(j + 1) * LANES]
            p_a, alpha_a = head(_dot_nt(qp, ks4_ref[2 * g, pl.ds(ks, kc), :]), 2 * j, g)
            p_b, alpha_b = head(_dot_nt(qp, ks4_ref[2 * g + 1, pl.ds(ks, kc), :]), 2 * j + 1, g)
            pv = (jnp.dot(p_a, vs4_ref[2 * g, pl.ds(ks, kc), :], preferred_element_type=F32)
                  + jnp.dot(p_b, vs4_ref[2 * g + 1, pl.ds(ks, kc), :], preferred_element_type=F32))
            acc_scr[j] = jnp.where(low, alpha_a, alpha_b) * acc_scr[j] + pv
        return carry

    lax.fori_loop(0, qi // (kc // tb) + 1, slc_chunk, 0)

    ws = pl.multiple_of(jnp.maximum(start - WIN, 0), tb)
    wpos = ws + lax.broadcasted_iota(jnp.int32, (tb, NSA_WIN_SPAN), 1)
    wmask = jnp.where(wpos <= tq, jnp.where(wpos > tq - WIN, 1.0, 0.0), 0.0)
    for j in range(pairs):
        g = (2 * j) // heads_per_group
        qp = q_ref[:, j * LANES:(j + 1) * LANES]
        p_a = _softmax_rows(_dot_nt(qp, kw4_ref[2 * g, pl.ds(ws, NSA_WIN_SPAN), :]), wmask)
        p_b = _softmax_rows(_dot_nt(qp, kw4_ref[2 * g + 1, pl.ds(ws, NSA_WIN_SPAN), :]), wmask)
        ow_scr[j] = (jnp.dot(p_a.astype(BF16), vw4_ref[2 * g, pl.ds(ws, NSA_WIN_SPAN), :],
                             preferred_element_type=F32)
                     + jnp.dot(p_b.astype(BF16), vw4_ref[2 * g + 1, pl.ds(ws, NSA_WIN_SPAN), :],
                               preferred_element_type=F32))

    gates = gate_ref[...]
    g_cmp, g_slc, g_win = (_exact_dot(gates, gx_ref[n]) for n in range(3))
    for j in range(pairs):
        cols = slice(j * LANES, (j + 1) * LANES)
        denom = jnp.maximum(jnp.where(low, l_scr[2 * j], l_scr[2 * j + 1]), 1e-30)
        o_ref[:, cols] = (g_cmp[:, cols] * oc_scr[j] + g_slc[:, cols] * (acc_scr[j] / denom)
                          + g_win[:, cols] * ow_scr[j])


def _nsa_mixer(q, kcmp, vcmp, kslc, vslc, kwin, vwin, gl, bsz, seq,
               q_gain, k_gain, cmp_pos, k_w1, k_w2, v_w1, v_w2, tm=512):
    n = bsz * seq
    tb = ATTN_BLOCK
    assert seq % NSA_KC == 0 and seq % tm == 0
    n_cmp = seq // CMP_STRIDE
    wide = C_HEADS * HEAD_DIM
    cos, sin = _rope_tables(seq, LANES)
    hsum, rot, swap = _head_matrices(wide)
    hsum1, rot1 = hsum[:LANES, :LANES], rot[:LANES, :LANES]
    full = lambda a: pl.BlockSpec(a.shape, lambda *i: (0,) * a.ndim)
    rows = lambda w: pl.BlockSpec((tm, w), lambda i: (i, 0))
    quad = pl.BlockSpec((4, tm, LANES), lambda i: (0, i, 0))
    pos_spec = pl.BlockSpec((tm, LANES), lambda i: (i % (seq // tm), 0))
    gq = jnp.tile(q_gain.astype(F32), C_HEADS).reshape(1, wide)
    gk = jnp.tile(k_gain.astype(F32), C_KV_HEADS).reshape(1, LANES)
    gl_pad = jnp.pad(gl, ((0, 0), (0, LANES - gl.shape[1])))
    params1 = pltpu.CompilerParams(dimension_semantics=("arbitrary",), vmem_limit_bytes=VMEM_LIMIT)
    quad_shape = jax.ShapeDtypeStruct((4, n, LANES), BF16)
    qn, ks4, vs4, kw4, vw4, gates = pl.pallas_call(
        _nsa_prep_kernel,
        grid=(n // tm,),
        in_specs=[rows(wide)] + [rows(LANES)] * 5 + [pos_spec, pos_spec, full(gq), full(gk),
                                                     full(hsum), full(rot), full(swap)],
        out_specs=[rows(wide), quad, quad, quad, quad, rows(LANES)],
        out_shape=[jax.ShapeDtypeStruct((n, wide), BF16), quad_shape, quad_shape, quad_shape,
                   quad_shape, jax.ShapeDtypeStruct((n, LANES), F32)],
        compiler_params=params1,
        name="nsa_prep",
    )(q, kslc, vslc, kwin, vwin, gl_pad, cos, sin, gq, gk, hsum, rot, swap)

    row_w = CMP_STRIDE * HEAD_DIM
    to_rows = lambda t: (t.reshape(bsz, seq, C_KV_HEADS, HEAD_DIM).transpose(0, 2, 1, 3)
                         .reshape(bsz * C_KV_HEADS, n_cmp, row_w))
    pos2 = cmp_pos.astype(F32).reshape(2, row_w)
    w1_halves = lambda w: w.astype(BF16).reshape(2, row_w, CMP_HIDDEN)
    w2_pad = lambda w: jnp.pad(w.astype(BF16), ((0, 0), (0, LANES - HEAD_DIM)))
    half = HEAD_DIM // 2
    cmp_end = jnp.arange(n_cmp, dtype=F32) * CMP_STRIDE + (CMP_LEN - 1)
    ang = cmp_end[:, None] * (ROPE_THETA ** (-jnp.arange(half, dtype=F32) / half))[None, :]
    cos_c, sin_c = jnp.tile(jnp.cos(ang), (1, LANES // half)), jnp.tile(jnp.sin(ang), (1, LANES // half))
    seq_rows = pl.BlockSpec((1, n_cmp, row_w), lambda i: (i, 0, 0))
    pair_out = pl.BlockSpec((1, 2, n_cmp, LANES), lambda i: (i, 0, 0, 0))
    pair_shape = jax.ShapeDtypeStruct((bsz * C_KV_HEADS, 2, n_cmp, LANES), BF16)
    cmp_in = (to_rows(kcmp), to_rows(vcmp), pos2, w1_halves(k_w1), w2_pad(k_w2), w1_halves(v_w1),
              w2_pad(v_w2), cos_c, sin_c, gk, hsum1, rot1, swap)
    kc2, vc2 = pl.pallas_call(
        _nsa_cmp_kernel,
        grid=(bsz * C_KV_HEADS,),
        in_specs=[seq_rows, seq_rows] + [full(a) for a in cmp_in[2:]],
        out_specs=[pair_out, pair_out],
        out_shape=[pair_shape, pair_shape],
        compiler_params=params1,
        name="nsa_compress",
    )(*cmp_in)
    kc2 = kc2.reshape(bsz, C_KV_HEADS, 2, n_cmp, LANES)
    vc2 = vc2.reshape(bsz, C_KV_HEADS, 2, n_cmp, LANES)

    n_slc = seq // SLC_LEN
    ratio, span = SLC_LEN // CMP_STRIDE, CMP_LEN // CMP_STRIDE
    off = np.arange(n_cmp)[None, :] - ratio * np.arange(n_slc)[:, None]
    m2s = np.maximum(np.minimum(ratio - 1, off) - np.maximum(0, off - span + 1) + 1, 0)
    m2s = np.pad(m2s, ((0, LANES - n_slc), (0, 0)))
    key_blk = np.arange(seq) // SLC_LEN
    blk = (np.arange(LANES)[:, None] == key_blk[None, :]).reshape(LANES, seq // NSA_KC, NSA_KC)
    blk = jnp.asarray(blk.transpose(1, 0, 2), BF16)
    col = np.arange(wide) // HEAD_DIM
    gx = np.stack([np.arange(LANES)[:, None] == (3 * col + br)[None, :] for br in range(3)])
    gx = jnp.asarray(gx, BF16)
    m2s = jnp.asarray(m2s, BF16)
    nq = seq // tb
    qspec = pl.BlockSpec((tb, wide), lambda b, i: (b * nq + i, 0))
    cmp_spec = pl.BlockSpec((1, C_KV_HEADS, 2, n_cmp, LANES), lambda b, i: (b, 0, 0, 0, 0))
    seq_spec = pl.BlockSpec((4, seq, LANES), lambda b, i: (0, b, 0))
    assert n_slc <= LANES
    return pl.pallas_call(
        functools.partial(_nsa_attn_kernel, top_n=min(SLC_TOPN, n_slc)),
        grid=(bsz, nq),
        in_specs=[qspec, pl.BlockSpec((tb, LANES), lambda b, i: (b * nq + i, 0)), cmp_spec, cmp_spec,
                  seq_spec, seq_spec, seq_spec, seq_spec, full(m2s), full(blk), full(gx)],
        out_specs=qspec,
        out_shape=jax.ShapeDtypeStruct((n, wide), F32),
        scratch_shapes=[pltpu.VMEM((C_HEADS, tb, 1), F32), pltpu.VMEM((C_HEADS, tb, 1), F32),
                        pltpu.VMEM((C_HEADS // 2, tb, LANES), F32),
                        pltpu.VMEM((C_HEADS // 2, tb, LANES), F32),
                        pltpu.VMEM((C_HEADS // 2, tb, LANES), F32)],
        compiler_params=pltpu.CompilerParams(
            dimension_semantics=("arbitrary", "arbitrary"), vmem_limit_bytes=PEER_VMEM_LIMIT),
        name="nsa_attn",
    )(qn, gates, kc2, vc2, ks4, vs4, kw4, vw4, m2s, blk, gx)


def _compress_blocks(t, cmp_pos, w1, w2):
    bsz, seq, groups, dh = t.shape
    n_cmp = (seq - CMP_LEN) // CMP_STRIDE + 1
    idx = jnp.arange(n_cmp)[:, None] * CMP_STRIDE + jnp.arange(CMP_LEN)[None, :]
    blocks = t.astype(F32)[:, idx] + cmp_pos.astype(F32)[:, None, :]
    blocks = blocks.transpose(0, 1, 3, 2, 4).reshape(bsz, n_cmp, groups, CMP_LEN * dh)
    return jax.nn.gelu(blocks @ w1.astype(F32)) @ w2.astype(F32)


def _nsa_attention(q, k_cmp, v_cmp, k_slc, v_slc, k_win, v_win, gate_logits,
                   q_gain, k_gain, cmp_pos, k_w1, k_w2, v_w1, v_w2, pos):
    bsz, seq = q.shape[:2]
    rep = C_HEADS // C_KV_HEADS
    n_cmp = (seq - CMP_LEN) // CMP_STRIDE + 1
    n_slc = seq // SLC_LEN
    top_n = min(SLC_TOPN, n_slc)
    q = _rope(_rms_norm(q, q_gain), pos).astype(F32) * HEAD_DIM ** -0.5
    cmp_end = jnp.arange(n_cmp) * CMP_STRIDE + (CMP_LEN - 1)
    kc = _rope(_rms_norm(_compress_blocks(k_cmp, cmp_pos, k_w1, k_w2), k_gain), cmp_end)
    vc = _compress_blocks(v_cmp, cmp_pos, v_w1, v_w2)
    ks = _rope(_rms_norm(k_slc, k_gain), pos).astype(F32)
    ks = ks.reshape(bsz, n_slc, SLC_LEN, C_KV_HEADS, HEAD_DIM).transpose(0, 3, 1, 2, 4)
    vs = v_slc.astype(F32).reshape(bsz, n_slc, SLC_LEN, C_KV_HEADS, HEAD_DIM).transpose(0, 3, 1, 2, 4)
    pad = ((0, 0), (WIN, 0), (0, 0), (0, 0))
    kw = jnp.pad(_rope(_rms_norm(k_win, k_gain), pos).astype(F32), pad)
    vw = jnp.pad(v_win.astype(F32), pad)
    gates = jax.nn.sigmoid(gate_logits.astype(F32))
    ratio = SLC_LEN // CMP_STRIDE
    span = CMP_LEN // CMP_STRIDE
    off = jnp.arange(n_cmp)[:, None] - ratio * jnp.arange(n_slc)[None, :]
    cmp_to_slc = jnp.maximum(
        jnp.minimum(ratio - 1, off) - jnp.maximum(0, off - span + 1) + 1, 0).astype(F32)
    slc_ids = jnp.arange(n_slc)
    tok_off = jnp.arange(SLC_LEN)
    win_off = jnp.arange(WIN + C_BLOCK)
    gather = jax.vmap(jax.vmap(lambda t, idx: t[idx]))

    def block(args):
        qb, gb, start = args
        tq = start + jnp.arange(C_BLOCK)
        qg = qb.reshape(bsz, C_BLOCK, C_KV_HEADS, rep, HEAD_DIM)
        gb = gb.reshape(bsz, C_BLOCK, C_KV_HEADS, rep, 3)
        s_c = jnp.einsum('bqgrd,bcgd->bqgrc', qg, kc)
        p_c = _masked_softmax(s_c, (cmp_end[None, :] <= tq[:, None])[None, :, None, None, :])
        o_c = jnp.einsum('bqgrc,bcgd->bqgrd', p_c, vc)
        imp = jnp.einsum('bqgrc,cj->bqgj', p_c, cmp_to_slc)
        cur = tq // SLC_LEN
        forced = (slc_ids[None, :] == cur[:, None]) | (slc_ids[None, :] == 0)
        admissible = slc_ids[None, :] <= cur[:, None]
        imp = jnp.where(forced[None, :, None, :], FORCE_SCORE, imp)
        imp = jnp.where(admissible[None, :, None, :], imp, -jnp.inf)
        _, sel = lax.top_k(imp, top_n)
        sel_g = sel.transpose(0, 2, 1, 3)
        k_sel = gather(ks, sel_g)
        v_sel = gather(vs, sel_g)
        s_s = jnp.einsum('bqgrd,bgqnld->bqgrnl', qg, k_sel)
        kpos = sel[..., None] * SLC_LEN + tok_off
        smask = (kpos <= tq[None, :, None, None, None]).reshape(bsz, C_BLOCK, C_KV_HEADS, 1, -1)
        p_s = _masked_softmax(s_s.reshape(bsz, C_BLOCK, C_KV_HEADS, rep, -1), smask)
        o_s = jnp.einsum('bqgrnl,bgqnld->bqgrd', p_s.reshape(s_s.shape), v_sel)
        kwb = lax.dynamic_slice_in_dim(kw, start, WIN + C_BLOCK, axis=1)
        vwb = lax.dynamic_slice_in_dim(vw, start, WIN + C_BLOCK, axis=1)
        wpos = start - WIN + win_off
        wmask = ((wpos[None, :] <= tq[:, None]) & (wpos[None, :] > tq[:, None] - WIN)
                 & (wpos[None, :] >= 0))
        s_w = jnp.einsum('bqgrd,bkgd->bqgrk', qg, kwb)
        p_w = _masked_softmax(s_w, wmask[None, :, None, None, :])
        o_w = jnp.einsum('bqgrk,bkgd->bqgrd', p_w, vwb)
        o = gb[..., 0:1] * o_c + gb[..., 1:2] * o_s + gb[..., 2:3] * o_w
        return o.reshape(bsz, C_BLOCK, C_HEADS * HEAD_DIM)

    starts = jnp.arange(seq // C_BLOCK) * C_BLOCK
    out = lax.map(block, (_to_blocks(q, C_BLOCK), _to_blocks(gates, C_BLOCK), starts))
    return out.swapaxes(0, 1).reshape(bsz, seq, C_HEADS * HEAD_DIM)


def _gla_attention(q, k, v, g_low, r, w_gate, b_gate, o_gain):
    bsz, seq = q.shape[:2]
    n_chunk = seq // GLA_CHUNK
    logit = g_low.astype(F32) @ w_gate.astype(F32) + b_gate.astype(F32)
    log_a = jax.nn.log_sigmoid(logit) / GATE_TAU

    def chunks(t):
        return t.astype(F32).reshape(bsz, n_chunk, GLA_CHUNK, D_HEADS, -1).transpose(0, 3, 1, 2, 4)

    qc = chunks(q) * D_KDIM ** -0.5
    kc, vc, gc = chunks(k), chunks(v), chunks(log_a)
    bcum = jnp.cumsum(gc, axis=3)
    blast = bcum[:, :, :, -1:, :]
    q_dec = qc * jnp.exp(bcum)
    k_inv = kc * jnp.exp(-bcum)
    causal = jnp.tril(jnp.ones((GLA_CHUNK, GLA_CHUNK), dtype=bool))
    att = jnp.where(causal, jnp.einsum('bhncd,bhnsd->bhncs', q_dec, k_inv), 0.0)
    o_intra = jnp.einsum('bhncs,bhnse->bhnce', att, vc)
    upd = jnp.einsum('bhncd,bhnce->bhnde', kc * jnp.exp(blast - bcum), vc)
    decay = jnp.exp(blast[:, :, :, 0, :])

    def step(state, inp):
        dec, u = inp
        return dec[..., None] * state + u, state

    init = jnp.zeros((bsz, D_HEADS, D_KDIM, D_VDIM), F32)
    _, prev = lax.scan(step, init, (jnp.moveaxis(decay, 2, 0), jnp.moveaxis(upd, 2, 0)))
    o_inter = jnp.einsum('bhncd,nbhde->bhnce', q_dec, prev)
    o = (o_intra + o_inter).transpose(0, 2, 3, 1, 4).reshape(bsz, seq, D_HEADS, D_VDIM)
    o = _rms_norm(o, o_gain).reshape(bsz, seq, D_HEADS * D_VDIM)
    return o * jax.nn.silu(r.astype(F32))


PEER_SLOTS = PEER_HEADS * PEER_TOPK
PEER_TT = 128
ROW_TILE = (SUBLANES, LANES)
NEG_INF = float('-inf')


def _top16_rows(s, row_id, big):
    vals, ids = [], []
    for _ in range(PEER_TOPK):
        m = jnp.max(s, axis=0, keepdims=True)
        pick = jnp.min(jnp.where(s == m, row_id, big), axis=0, keepdims=True)
        s = jnp.where(row_id == pick, NEG_INF, s)
        vals.append(m)
        ids.append(pick)
    return vals, ids


def _peer_route_kernel(x_ref, g_ref, wq_ref, sk_ref, xn_ref, idx_ref, gate_ref, v_scr, i_scr):
    tt = x_ref.shape[0]
    x = x_ref[...]
    xn = x * lax.rsqrt(jnp.mean(x * x, axis=-1, keepdims=True) + NORM_EPS) * g_ref[...]
    xn_ref[...] = xn
    xb = xn.astype(BF16)
    key_id = lax.broadcasted_iota(jnp.int32, (PEER_KEYS, tt), 0)

    def sub_scores(hp, carry):
        q = jnp.dot(xb, wq_ref[hp], preferred_element_type=F32).astype(BF16)
        s = lax.dot_general(sk_ref[hp], q, (((1,), (1,)), ((), ())),
                            preferred_element_type=F32)
        vals, ids = _top16_rows(s, key_id, PEER_KEYS)
        v_scr[hp] = jnp.concatenate(vals, axis=0)
        i_scr[hp] = jnp.concatenate(ids, axis=0)
        return carry

    lax.fori_loop(0, 2 * PEER_HEADS, sub_scores, 0)

    n_blk = 10
    r = lax.broadcasted_iota(jnp.int32, (n_blk * SUBLANES, tt), 0)
    blk, b = r // SUBLANES, r % SUBLANES
    a_of_blk = jnp.where(blk < 2, 0, blk - 1)
    pos = jnp.where(blk == 9, (SUBLANES + b) * PEER_TOPK,
                    jnp.where(blk == 1, SUBLANES + b, a_of_blk * PEER_TOPK + b))
    b_max = jnp.where(blk < 3, 7, jnp.where(blk == 3, 4, jnp.where(blk == 4, 3, jnp.where(
        blk == 5, 2, jnp.where(blk < 9, 1, 7)))))
    live = b <= b_max

    def combine(h, carry):
        v1, v2 = v_scr[2 * h], v_scr[2 * h + 1]
        i1, i2 = i_scr[2 * h], i_scr[2 * h + 1]
        lo_v, hi_v = v2[0:SUBLANES], v2[SUBLANES:]
        lo_i, hi_i = i2[0:SUBLANES], i2[SUBLANES:]
        cand = [v1[0:1] + lo_v, v1[0:1] + hi_v]
        cid = [i1[0:1] * PEER_KEYS + lo_i, i1[0:1] * PEER_KEYS + hi_i]
        for a in range(1, SUBLANES):
            cand.append(v1[a:a + 1] + lo_v)
            cid.append(i1[a:a + 1] * PEER_KEYS + lo_i)
        cand.append(v1[SUBLANES:] + v2[0:1])
        cid.append(i1[SUBLANES:] * PEER_KEYS + i2[0:1])
        cand = jnp.where(live, jnp.concatenate(cand, axis=0), NEG_INF)
        cid = jnp.concatenate(cid, axis=0)
        top_s, experts = [], []
        for _ in range(PEER_TOPK):
            m = jnp.max(cand, axis=0, keepdims=True)
            pick = jnp.min(jnp.where(cand == m, pos, PEER_TOPK * PEER_TOPK), axis=0, keepdims=True)
            hit = pos == pick
            experts.append(jnp.max(jnp.where(hit, cid, -1), axis=0, keepdims=True))
            cand = jnp.where(hit, NEG_INF, cand)
            top_s.append(m)
        top_s = jnp.concatenate(top_s, axis=0)
        ex = jnp.exp(top_s - top_s[0:1])
        gate_ref[0, h] = ex / jnp.sum(ex, axis=0, keepdims=True)
        idx_ref[0, h] = jnp.concatenate(experts, axis=0)
        return carry

    lax.fori_loop(0, PEER_HEADS, combine, 0)


def _peer_route(x2d, g_ffn, wq, subkeys):
    n, dm = x2d.shape
    tt = PEER_TT
    half = PEER_QDIM // 2
    wq_t = wq.astype(BF16).reshape(dm, 2 * PEER_HEADS, half).transpose(1, 0, 2)
    sk = subkeys.astype(BF16).reshape(2 * PEER_HEADS, PEER_KEYS, half)
    tile4 = (1, PEER_HEADS, PEER_TOPK, tt)
    return pl.pallas_call(
        _peer_route_kernel,
        grid=(n // tt,),
        in_specs=[pl.BlockSpec((tt, dm), lambda i: (i, 0)),
                  pl.BlockSpec((1, dm), lambda i: (0, 0)),
                  pl.BlockSpec(wq_t.shape, lambda i: (0, 0, 0)),
                  pl.BlockSpec(sk.shape, lambda i: (0, 0, 0))],
        out_specs=[pl.BlockSpec((tt, dm), lambda i: (i, 0)),
                   pl.BlockSpec(tile4, lambda i: (i, 0, 0, 0)),
                   pl.BlockSpec(tile4, lambda i: (i, 0, 0, 0))],
        out_shape=[jax.ShapeDtypeStruct((n, dm), F32),
                   jax.ShapeDtypeStruct((n // tt,) + tile4[1:], jnp.int32),
                   jax.ShapeDtypeStruct((n // tt,) + tile4[1:], F32)],
        scratch_shapes=[pltpu.VMEM((2 * PEER_HEADS, PEER_TOPK, tt), F32),
                        pltpu.VMEM((2 * PEER_HEADS, PEER_TOPK, tt), jnp.int32)],
        compiler_params=pltpu.CompilerParams(
            dimension_semantics=("arbitrary",), vmem_limit_bytes=VMEM_LIMIT),
        name="peer_route",
    )(x2d, g_ffn.reshape(1, dm), wq_t, sk)


PEER_UNROLL = 2


def _gather_rows(idx_smem, tab_ref, slab, t):
    for i in range(PEER_SLOTS):
        slab[i * SUBLANES:(i + 1) * SUBLANES, :] = tab_ref[idx_smem[i, t]]


def _peer_up_kernel(idx_hbm, xn_ref, gate_ref, tab_ref, fold_ref, out_ref, idx_smem, slab, rbuf, sem):
    tt = xn_ref.shape[0]
    cp = pltpu.make_async_copy(idx_hbm.at[pl.program_id(0)], idx_smem, sem)
    cp.start()
    cp.wait()
    eye = (lax.broadcasted_iota(jnp.int32, (LANES, LANES), 0)
           == lax.broadcasted_iota(jnp.int32, (LANES, LANES), 1))

    def step(p, carry):
        for u in range(PEER_UNROLL):
            t = p * PEER_UNROLL + u
            _gather_rows(idx_smem, tab_ref, slab.at[u], t)
            x16 = jnp.tile(xn_ref[t], (LANES // SUBLANES, 1)).astype(BF16)
            y = lax.dot_general(slab[u], x16, (((1,), (1,)), ((), ())),
                                preferred_element_type=F32)
            y = y.reshape(PEER_SLOTS * SUBLANES // LANES, LANES, LANES)
            rbuf[t] = jnp.sum(jnp.where(eye[None], y, 0.0), axis=1)
        return carry

    lax.fori_loop(0, tt // PEER_UNROLL, step, 0)
    r = rbuf[...].reshape(tt * SUBLANES, LANES)
    r_hi = r.astype(BF16)
    r_lo = (r - r_hi.astype(F32)).astype(BF16)
    h = (jnp.dot(r_hi, fold_ref[...], preferred_element_type=F32)
         + jnp.dot(r_lo, fold_ref[...], preferred_element_type=F32))
    out_ref[...] = gate_ref[...] * jax.nn.gelu(h)


def _peer_down_kernel(idx_hbm, gh_ref, res_ref, tab_ref, spread_ref, out_ref, idx_smem, slab, ghx, sem):
    tt = gh_ref.shape[0]
    cp = pltpu.make_async_copy(idx_hbm.at[pl.program_id(0)], idx_smem, sem)
    cp.start()
    cp.wait()
    ghx[...] = jnp.dot(gh_ref[...].astype(BF16), spread_ref[...], preferred_element_type=F32)
    width = PEER_SLOTS * SUBLANES
    diag = (lax.broadcasted_iota(jnp.int32, (SUBLANES, width), 0)
            == lax.broadcasted_iota(jnp.int32, (SUBLANES, width), 1) % SUBLANES)

    def step(p, carry):
        for u in range(PEER_UNROLL):
            t = p * PEER_UNROLL + u
            _gather_rows(idx_smem, tab_ref, slab.at[u], t)
            row = jnp.broadcast_to(ghx[pl.ds(t, 1), :], (SUBLANES, width))
            coef = jnp.where(diag, row, 0.0).astype(BF16)
            out_ref[t] = res_ref[t] + jnp.dot(coef, slab[u], preferred_element_type=F32)
        return carry

    lax.fori_loop(0, tt // PEER_UNROLL, step, 0)


def _peer_table(tab):
    e, dm = tab.shape
    assert dm == SUBLANES * LANES
    return tab.astype(BF16).reshape(e, SUBLANES, LANES)


def _peer_ffn_residual(x2d, g_ffn, wq, subkeys, u_tab, v_tab):
    n, dm = x2d.shape
    tt = PEER_TT
    n_tiles = n // tt
    xn, idx, gate = _peer_route(x2d, g_ffn, wq, subkeys)
    idx_tiles = idx.reshape(n_tiles, PEER_SLOTS, tt)
    gate_rows = gate.transpose(0, 3, 1, 2).reshape(n * PEER_HEADS, PEER_TOPK)
    width = PEER_SLOTS * SUBLANES
    lane = np.arange(LANES)
    fold = jnp.asarray(lane[:, None] // SUBLANES == np.arange(PEER_TOPK)[None, :], BF16)
    spread = jnp.asarray(np.arange(PEER_SLOTS)[:, None] == np.arange(width)[None, :] // SUBLANES, BF16)
    params = pltpu.CompilerParams(dimension_semantics=("arbitrary",),
                                  vmem_limit_bytes=PEER_VMEM_LIMIT)
    table_spec = pl.BlockSpec(memory_space=pltpu.VMEM)
    slab_shape = pltpu.VMEM((PEER_UNROLL, PEER_SLOTS * SUBLANES, LANES), BF16)
    idx_scratch = pltpu.SMEM((PEER_SLOTS, tt), jnp.int32)
    row3 = (tt,) + ROW_TILE

    gh = pl.pallas_call(
        _peer_up_kernel,
        grid=(n_tiles,),
        in_specs=[pl.BlockSpec(memory_space=pl.ANY),
                  pl.BlockSpec(row3, lambda i: (i, 0, 0)),
                  pl.BlockSpec((tt * PEER_HEADS, PEER_TOPK), lambda i: (i, 0)),
                  table_spec,
                  pl.BlockSpec((LANES, PEER_TOPK), lambda i: (0, 0))],
        out_specs=pl.BlockSpec((tt * PEER_HEADS, PEER_TOPK), lambda i: (i, 0)),
        out_shape=jax.ShapeDtypeStruct((n * PEER_HEADS, PEER_TOPK), F32),
        scratch_shapes=[idx_scratch, slab_shape, pltpu.VMEM(row3, F32), pltpu.SemaphoreType.DMA(())],
        compiler_params=params,
        name="peer_up",
    )(idx_tiles,xn.reshape((n,) + ROW_TILE), gate_rows, _peer_table(u_tab), fold)

    out = pl.pallas_call(
        _peer_down_kernel,
        grid=(n_tiles,),
        in_specs=[pl.BlockSpec(memory_space=pl.ANY),
                  pl.BlockSpec((tt, PEER_SLOTS), lambda i: (i, 0)),
                  pl.BlockSpec(row3, lambda i: (i, 0, 0)),
                  table_spec,
                  pl.BlockSpec((PEER_SLOTS, width), lambda i: (0, 0))],
        out_specs=pl.BlockSpec(row3, lambda i: (i, 0, 0)),
        out_shape=jax.ShapeDtypeStruct((n,) + ROW_TILE, F32),
        scratch_shapes=[idx_scratch, slab_shape, pltpu.VMEM((tt, width), F32), pltpu.SemaphoreType.DMA(())],
        compiler_params=params,
        name="peer_down",
    )(idx_tiles,gh.reshape(n, PEER_SLOTS), x2d.reshape((n,) + ROW_TILE), _peer_table(v_tab), spread)
    return out.reshape(n, dm)


def _even_layer(x, pos, mix_norm, w_in, q_gain, k_gain, log_dt, a_re, a_im, b_re, b_im,
                c_re, c_im, d_skip, w_glu, b_glu, w_out):
    bsz, seq, dm = x.shape
    x2d = x.reshape(bsz * seq, dm)
    pieces = dict(zip(('q', 'k', 'v', 'iq', 'ik', 'iw', 'u'), _split_cols(w_in, EVEN_COLS)))
    w_lay = jnp.concatenate(
        [jnp.pad(pieces[name], ((0, 0), (0, EVEN_LAYOUT[name][1] - pieces[name].shape[1])))
         for name in sorted(EVEN_LAYOUT, key=lambda s: EVEN_LAYOUT[s][0])], axis=1)
    h = _norm_proj(x2d, mix_norm, w_lay)
    o_a = _dsa_mixer(h, bsz, seq, q_gain, k_gain)
    u0, uw = EVEN_LAYOUT['u']
    u = h[:, u0:u0 + uw].reshape(bsz, seq, uw)
    o_b = _s5_mixer(u, log_dt, a_re, a_im, b_re, b_im, c_re, c_im, d_skip, w_glu, b_glu)
    mix = jnp.concatenate([o_a, o_b.reshape(bsz * seq, uw)], axis=-1)
    return _proj_residual(mix, w_out, x2d).reshape(bsz, seq, dm)


def _odd_layer(x, pos, mix_norm, w_in, q_gain, k_gain, cmp_pos, k_w1, k_w2, v_w1, v_w2,
               w_gate, b_gate, o_gain, w_out):
    bsz, seq, dm = x.shape
    x2d = x.reshape(bsz * seq, dm)
    h = _norm_proj(x2d, mix_norm, w_in)
    (q, kc, vc, ks, vs, kw, vw, gl, gq, gk, gv, glow, gr) = _split_cols(h, ODD_COLS)
    o_c = _nsa_mixer(q, kc, vc, ks, vs, kw, vw, gl, bsz, seq, q_gain, k_gain,
                     cmp_pos, k_w1, k_w2, v_w1, v_w2)
    seq3 = lambda t: t.reshape(bsz, seq, -1)
    o_d = _gla_attention(seq3(gq), seq3(gk), seq3(gv), seq3(glow), seq3(gr), w_gate, b_gate, o_gain)
    mix = jnp.concatenate([o_c, o_d.reshape(bsz * seq, -1)], axis=-1)
    return _proj_residual(mix, w_out, x2d).reshape(bsz, seq, dm)


def kernel(x, l0_mix_norm, l0_w_in, l0_a_q_gain, l0_a_k_gain, l0_s5_log_dt, l0_s5_a_re, l0_s5_a_im, l0_s5_b_re, l0_s5_b_im, l0_s5_c_re, l0_s5_c_im, l0_s5_d, l0_s5_w_glu, l0_s5_b_glu, l0_w_out, l0_ffn_norm, l0_peer_wq, l0_peer_subkeys, l0_peer_u, l0_peer_v, l1_mix_norm, l1_w_in, l1_c_q_gain, l1_c_k_gain, l1_nsa_cmp_pos, l1_nsa_k_w1, l1_nsa_k_w2, l1_nsa_v_w1, l1_nsa_v_w2, l1_gla_w_gate, l1_gla_b_gate, l1_gla_o_gain, l1_w_out, l1_ffn_norm, l1_peer_wq, l1_peer_subkeys, l1_peer_u, l1_peer_v):
    seq = x.shape[1]
    pos = jnp.arange(seq, dtype=jnp.int32)
    x = _even_layer(x, pos, l0_mix_norm, l0_w_in, l0_a_q_gain, l0_a_k_gain, l0_s5_log_dt,
                    l0_s5_a_re, l0_s5_a_im, l0_s5_b_re, l0_s5_b_im, l0_s5_c_re, l0_s5_c_im,
                    l0_s5_d, l0_s5_w_glu, l0_s5_b_glu, l0_w_out)
    shape = x.shape
    flat = lambda t: t.reshape(-1, shape[-1])
    x = _peer_ffn_residual(flat(x), l0_ffn_norm, l0_peer_wq, l0_peer_subkeys,
                           l0_peer_u, l0_peer_v).reshape(shape)
    x = _odd_layer(x, pos, l1_mix_norm, l1_w_in, l1_c_q_gain, l1_c_k_gain, l1_nsa_cmp_pos,
                   l1_nsa_k_w1, l1_nsa_k_w2, l1_nsa_v_w1, l1_nsa_v_w2, l1_gla_w_gate,
                   l1_gla_b_gate, l1_gla_o_gain, l1_w_out)
    return _peer_ffn_residual(flat(x), l1_ffn_norm, l1_peer_wq, l1_peer_subkeys,
                              l1_peer_u, l1_peer_v).reshape(shape)
```

```python
import functools
import math

import jax
import jax.numpy as jnp
import numpy as np
from jax import lax
from jax.experimental import pallas as pl
from jax.experimental.pallas import tpu as pltpu

F32 = jnp.float32
BF16 = jnp.bfloat16
NORM_EPS = 1e-6
ROPE_THETA = 10000.0
HEAD_DIM = 64
ATTN_BLOCK = 128
A_HEADS = 8
A_KV_HEADS = 2
IDX_HEADS = 8
IDX_DIM = 64
A_TOPK_MAX = 256
B_WIDTH = 512
B_GROUP = 16
B_GROUPS = B_WIDTH // B_GROUP
B_STATE = 64
C_HEADS = 8
C_KV_HEADS = 2
CMP_LEN = 32
CMP_STRIDE = 16
CMP_HIDDEN = 256
SLC_LEN = 64
SLC_TOPN = 16
WIN = 512
C_BLOCK = 64
FORCE_SCORE = 1e9
D_HEADS = 4
D_KDIM = 64
D_VDIM = 128
GATE_RANK = 16
GATE_TAU = 16.0
GLA_CHUNK = 32
PEER_HEADS = 8
PEER_KEYS = 128
PEER_QDIM = 256
PEER_TOPK = 16
PEER_BLOCK = 128

EVEN_COLS = (A_HEADS * HEAD_DIM, A_KV_HEADS * HEAD_DIM, A_KV_HEADS * HEAD_DIM,
             IDX_HEADS * IDX_DIM, IDX_DIM, IDX_HEADS, B_WIDTH)
ODD_COLS = (C_HEADS * HEAD_DIM,) + (C_KV_HEADS * HEAD_DIM,) * 6 + (
    C_HEADS * 3, D_HEADS * D_KDIM, D_HEADS * D_KDIM, D_HEADS * D_VDIM, GATE_RANK, D_HEADS * D_VDIM)

LANES = 128
SUBLANES = 8
MXU_DIM = 256
VMEM_LIMIT = 48 * 1024 * 1024
PEER_VMEM_LIMIT = 56 * 1024 * 1024


def _norm_proj_kernel(x_ref, g_ref, w_ref, o_ref):
    x = x_ref[...]
    y = x * lax.rsqrt(jnp.mean(x * x, axis=-1, keepdims=True) + NORM_EPS) * g_ref[...]
    o_ref[...] = jnp.dot(y.astype(BF16), w_ref[...].astype(BF16), preferred_element_type=F32)


def _norm_proj(x2d, gain, w, tm=512, tn=512):
    m, k = x2d.shape
    n = w.shape[1]
    n_pad = -(-n // tn) * tn
    w_p = jnp.pad(w, ((0, 0), (0, n_pad - n)))
    out = pl.pallas_call(
        _norm_proj_kernel,
        grid=(m // tm, n_pad // tn),
        in_specs=[pl.BlockSpec((tm, k), lambda i, j: (i, 0)),
                  pl.BlockSpec((1, k), lambda i, j: (0, 0)),
                  pl.BlockSpec((k, tn), lambda i, j: (0, j))],
        out_specs=pl.BlockSpec((tm, tn), lambda i, j: (i, j)),
        out_shape=jax.ShapeDtypeStruct((m, n_pad), F32),
        compiler_params=pltpu.CompilerParams(
            dimension_semantics=("arbitrary", "arbitrary"), vmem_limit_bytes=VMEM_LIMIT),
    )(x2d, gain.reshape(1, k), w_p)
    return out[:, :n]


def _proj_res_kernel(a_ref, w_ref, r_ref, o_ref):
    o_ref[...] = r_ref[...] + jnp.dot(a_ref[...].astype(BF16), w_ref[...].astype(BF16),
                                      preferred_element_type=F32)


def _proj_residual(a2d, w, res2d, tm=512):
    m, k = a2d.shape
    n = w.shape[1]
    return pl.pallas_call(
        _proj_res_kernel,
        grid=(m // tm,),
        in_specs=[pl.BlockSpec((tm, k), lambda i: (i, 0)),
                  pl.BlockSpec((k, n), lambda i: (0, 0)),
                  pl.BlockSpec((tm, n), lambda i: (i, 0))],
        out_specs=pl.BlockSpec((tm, n), lambda i: (i, 0)),
        out_shape=jax.ShapeDtypeStruct((m, n), F32),
        compiler_params=pltpu.CompilerParams(
            dimension_semantics=("arbitrary",), vmem_limit_bytes=VMEM_LIMIT),
    )(a2d, w, res2d)


def _rms_norm(x, gain):
    xf = x.astype(F32)
    y = xf * lax.rsqrt(jnp.mean(xf * xf, axis=-1, keepdims=True) + NORM_EPS)
    return (y * gain.astype(F32)).astype(x.dtype)


def _rope(x, pos):
    half = x.shape[-1] // 2
    inv_freq = ROPE_THETA ** (-jnp.arange(half, dtype=F32) / half)
    ang = pos.astype(F32)[:, None] * inv_freq[None, :]
    cos = jnp.cos(ang)[:, None, :]
    sin = jnp.sin(ang)[:, None, :]
    xf = x.astype(F32)
    x1, x2 = xf[..., :half], xf[..., half:]
    return jnp.concatenate([x1 * cos - x2 * sin, x1 * sin + x2 * cos], axis=-1).astype(x.dtype)


def _masked_softmax(s, mask):
    s = jnp.where(mask, s.astype(F32), -1e30)
    m = jnp.max(s, axis=-1, keepdims=True)
    p = jnp.exp(s - m) * mask
    return p / jnp.maximum(jnp.sum(p, axis=-1, keepdims=True), 1e-30)


def _split_cols(h, sizes):
    return jnp.split(h, np.cumsum(sizes)[:-1].tolist(), axis=-1)


def _to_blocks(t, blk):
    return t.reshape(t.shape[0], t.shape[1] // blk, blk, *t.shape[2:]).swapaxes(0, 1)


INT_MIN = -2 ** 31
MASKED = -1e30
DSA_KC = 512
EVEN_LAYOUT = dict(q=(0, 512), iq=(512, 512), u=(1024, 512), k=(1536, 128), v=(1664, 128),
                   ik=(1792, 128), iw=(1920, 128))
EVEN_WIDTH = 2048


def _exact_dot(x, m):
    hi = x.astype(BF16)
    lo = (x - hi.astype(F32)).astype(BF16)
    return (jnp.dot(hi, m, preferred_element_type=F32) + jnp.dot(lo, m, preferred_element_type=F32))


def _head_rms(x, gain, hsum):
    return x * lax.rsqrt(_exact_dot(x * x, hsum) + NORM_EPS) * gain


def _head_rope(x, cos, sin, rot):
    return x * cos + _exact_dot(x, rot) * sin


def _split_pair(x, swap):
    lane = lax.broadcasted_iota(jnp.int32, x.shape, 1)
    low = lane < HEAD_DIM
    xs = jnp.dot(x, swap, preferred_element_type=F32).astype(BF16)
    zero = jnp.zeros_like(x)
    return (jnp.where(low, x, zero), jnp.where(low, zero, xs),
            jnp.where(low, xs, zero), jnp.where(low, zero, x))


def _dsa_prep_kernel(q_ref, iq_ref, k_ref, v_ref, ik_ref, iw_ref, cos_ref, sin_ref, qg_ref, kg_ref,
                     hsum_ref, rot_ref, swap_ref, qn_ref, iqn_ref, k4_ref, v4_ref, ik2_ref, iwn_ref):
    cos1, sin1 = cos_ref[...], sin_ref[...]
    reps = q_ref.shape[1] // LANES
    cos4, sin4 = jnp.tile(cos1, (1, reps)), jnp.tile(sin1, (1, reps))
    hsum, rot = hsum_ref[...], rot_ref[...]
    hsum1, rot1 = hsum[0:LANES, 0:LANES], rot[0:LANES, 0:LANES]
    scale = HEAD_DIM ** -0.5
    q = _head_rope(_head_rms(q_ref[...], qg_ref[...], hsum), cos4, sin4, rot) * scale
    qn_ref[...] = q.astype(BF16)
    iqn_ref[...] = (_head_rope(iq_ref[...], cos4, sin4, rot) * IDX_DIM ** -0.5).astype(BF16)
    k = _head_rope(_head_rms(k_ref[...], kg_ref[...], hsum1), cos1, sin1, rot1).astype(BF16)
    swap = swap_ref[...]
    for n, part in enumerate(_split_pair(k, swap)):
        k4_ref[n] = part
    for n, part in enumerate(_split_pair(v_ref[...].astype(BF16), swap)):
        v4_ref[n] = part
    ik = _head_rope(ik_ref[...], cos1, sin1, rot1).astype(BF16)
    ik_a, ik_b, _, _ = _split_pair(ik, swap)
    ik2_ref[0] = ik_a
    ik2_ref[1] = ik_b
    iwn_ref[...] = iw_ref[...] * IDX_HEADS ** -0.5


def _dot_nt(a, b):
    return lax.dot_general(a, b, (((1,), (1,)), ((), ())), preferred_element_type=F32)


def _dsa_attn_kernel(q_ref, iq_ref, iw_ref, k4_ref, v4_ref, ik2_ref, tri_ref, o_ref,
                     key_scr, m_scr, l_scr, acc_scr, *, topk):
    tb = q_ref.shape[0]
    kc = DSA_KC
    qi = pl.program_id(1)
    n_ch = qi // (kc // tb) + 1
    tq = qi * tb + lax.broadcasted_iota(jnp.int32, (tb, 1), 0)
    lane_k = lax.broadcasted_iota(jnp.int32, (tb, kc), 1)
    iw = iw_ref[...]

    def score_chunk(c, carry):
        ks = pl.multiple_of(c * kc, kc)
        ik_a = ik2_ref[0, pl.ds(ks, kc), :]
        ik_b = ik2_ref[1, pl.ds(ks, kc), :]
        sc = jnp.zeros((tb, kc), F32)
        for j in range(IDX_HEADS // 2):
            iqp = iq_ref[:, j * LANES:(j + 1) * LANES]
            sc = sc + iw[:, 2 * j:2 * j + 1] * jnp.maximum(_dot_nt(iqp, ik_a), 0.0)
            sc = sc + iw[:, 2 * j + 1:2 * j + 2] * jnp.maximum(_dot_nt(iqp, ik_b), 0.0)
        sc = jnp.where(sc == 0.0, 0.0, sc)
        bits = lax.bitcast_convert_type(sc, jnp.int32)
        key = bits ^ (jnp.right_shift(bits, 31) & 0x7FFFFFFF)
        key_scr[c] = jnp.where(ks + lane_k <= tq, key, INT_MIN)
        return carry

    lax.fori_loop(0, n_ch, score_chunk, 0)

    def count(pred_of_key):
        def body(c, acc):
            hit = jnp.where(pred_of_key(key_scr[c]), 1.0, 0.0)
            for s in range(kc // LANES):
                acc = acc + hit[:, s * LANES:(s + 1) * LANES]
            return acc
        acc = lax.fori_loop(0, n_ch, body, jnp.zeros((tb, LANES), F32))
        return jnp.sum(acc, axis=1, keepdims=True)

    def bit_step(i, t_u):
        cand = t_u | jnp.left_shift(jnp.int32(1), 31 - i)
        cand_s = cand ^ INT_MIN
        return jnp.where(count(lambda key: key >= cand_s) >= topk, cand, t_u)

    thr = lax.fori_loop(0, 32, bit_step, jnp.zeros((tb, 1), jnp.int32)) ^ INT_MIN
    need = topk - count(lambda key: key > thr)

    m_scr[...] = jnp.full(m_scr.shape, MASKED, F32)
    l_scr[...] = jnp.zeros(l_scr.shape, F32)
    acc_scr[...] = jnp.zeros(acc_scr.shape, F32)
    low = lax.broadcasted_iota(jnp.int32, (tb, LANES), 1) < HEAD_DIM

    def attn_chunk(c, ties_before):
        ks = pl.multiple_of(c * kc, kc)
        key = key_scr[c]
        eq = jnp.where(key == thr, 1.0, 0.0)
        rank = ties_before + jnp.dot(eq.astype(BF16), tri_ref[...], preferred_element_type=F32)
        tie_ok = jnp.where(rank <= need, eq, 0.0)
        sel = jnp.where(ks + lane_k <= tq, jnp.where(key > thr, 1.0, tie_ok), 0.0)
        chosen = sel > 0.0

        def head(s, h):
            sm = jnp.where(chosen, s, MASKED)
            m_old = m_scr[h]
            m_new = jnp.maximum(m_old, jnp.max(sm, axis=1, keepdims=True))
            p = jnp.exp(sm - m_new) * sel
            alpha = jnp.exp(m_old - m_new)
            l_scr[h] = alpha * l_scr[h] + jnp.sum(p, axis=1, keepdims=True)
            m_scr[h] = m_new
            return p.astype(BF16), alpha

        for j in range(A_HEADS // 2):
            g = (2 * j) // (A_HEADS // A_KV_HEADS)
            qp = q_ref[:, j * LANES:(j + 1) * LANES]
            k_a = k4_ref[2 * g, pl.ds(ks, kc), :]
            k_b = k4_ref[2 * g + 1, pl.ds(ks, kc), :]
            p_a, alpha_a = head(_dot_nt(qp, k_a), 2 * j)
            p_b, alpha_b = head(_dot_nt(qp, k_b), 2 * j + 1)
            pv = (jnp.dot(p_a, v4_ref[2 * g, pl.ds(ks, kc), :], preferred_element_type=F32)
                  + jnp.dot(p_b, v4_ref[2 * g + 1, pl.ds(ks, kc), :], preferred_element_type=F32))
            acc_scr[j] = jnp.where(low, alpha_a, alpha_b) * acc_scr[j] + pv
        return ties_before + jnp.sum(eq, axis=1, keepdims=True)

    lax.fori_loop(0, n_ch, attn_chunk, jnp.zeros((tb, 1), F32))
    for j in range(A_HEADS // 2):
        denom = jnp.maximum(jnp.where(low, l_scr[2 * j], l_scr[2 * j + 1]), 1e-30)
        o_ref[:, j * LANES:(j + 1) * LANES] = acc_scr[j] / denom


def _rope_tables(seq, width):
    half = HEAD_DIM // 2
    inv_freq = ROPE_THETA ** (-jnp.arange(half, dtype=F32) / half)
    ang = jnp.arange(seq, dtype=F32)[:, None] * inv_freq[None, :]
    reps = width // half
    return jnp.tile(jnp.cos(ang), (1, reps)), jnp.tile(jnp.sin(ang), (1, reps))


def _head_matrices(width):
    i = np.arange(width)
    same = (i[:, None] // HEAD_DIM) == (i[None, :] // HEAD_DIM)
    hsum = np.where(same, 1.0 / HEAD_DIM, 0.0)
    half = HEAD_DIM // 2
    src, dst = i[:, None], i[None, :]
    rot = np.where(same & (src == dst + half), -1.0, 0.0) + np.where(same & (src == dst - half), 1.0, 0.0)
    j = np.arange(LANES)
    swap = (j[:, None] == (j[None, :] + HEAD_DIM) % LANES).astype(np.float32)
    return jnp.asarray(hsum, BF16), jnp.asarray(rot, BF16), jnp.asarray(swap, BF16)


def _dsa_mixer(h, bsz, seq, q_gain, k_gain, tm=512):
    n = bsz * seq
    tb = ATTN_BLOCK
    assert seq % DSA_KC == 0 and seq % tm == 0
    topk = min(A_TOPK_MAX, seq // 4)
    cos, sin = _rope_tables(seq, LANES)
    hsum, rot, swap = _head_matrices(A_HEADS * HEAD_DIM)
    wide = A_HEADS * HEAD_DIM
    col = lambda name: EVEN_LAYOUT[name][0] // EVEN_LAYOUT[name][1]
    hspec = lambda name: pl.BlockSpec((tm, EVEN_LAYOUT[name][1]), lambda i, c=col(name): (i, c))
    full = lambda a: pl.BlockSpec(a.shape, lambda i: (0,) * a.ndim)
    pos_spec = pl.BlockSpec((tm, LANES), lambda i: (i % (seq // tm), 0))
    gq = jnp.tile(q_gain.astype(F32), A_HEADS).reshape(1, wide)
    gk = jnp.tile(k_gain.astype(F32), A_KV_HEADS).reshape(1, LANES)
    qn, iqn, k4, v4, ik2, iwn = pl.pallas_call(
        _dsa_prep_kernel,
        grid=(n // tm,),
        in_specs=[hspec('q'), hspec('iq'), hspec('k'), hspec('v'), hspec('ik'), hspec('iw'),
                  pos_spec, pos_spec, full(gq), full(gk), full(hsum), full(rot), full(swap)],
        out_specs=[pl.BlockSpec((tm, wide), lambda i: (i, 0)),
                   pl.BlockSpec((tm, wide), lambda i: (i, 0)),
                   pl.BlockSpec((4, tm, LANES), lambda i: (0, i, 0)),
                   pl.BlockSpec((4, tm, LANES), lambda i: (0, i, 0)),
                   pl.BlockSpec((2, tm, LANES), lambda i: (0, i, 0)),
                   pl.BlockSpec((tm, LANES), lambda i: (i, 0))],
        out_shape=[jax.ShapeDtypeStruct((n, wide), BF16), jax.ShapeDtypeStruct((n, wide), BF16),
                   jax.ShapeDtypeStruct((4, n, LANES), BF16), jax.ShapeDtypeStruct((4, n, LANES), BF16),
                   jax.ShapeDtypeStruct((2, n, LANES), BF16), jax.ShapeDtypeStruct((n, LANES), F32)],
        compiler_params=pltpu.CompilerParams(
            dimension_semantics=("arbitrary",), vmem_limit_bytes=VMEM_LIMIT),
        name="dsa_prep",
    )(h, h, h, h, h, h, cos, sin, gq, gk, hsum, rot, swap)

    r = np.arange(DSA_KC)
    tri = jnp.asarray(r[:, None] <= r[None, :], BF16)
    nq = seq // tb
    qspec = pl.BlockSpec((tb, wide), lambda b, i: (b * nq + i, 0))
    seq_spec = lambda lead: pl.BlockSpec((lead, seq, LANES), lambda b, i: (0, b, 0))
    return pl.pallas_call(
        functools.partial(_dsa_attn_kernel, topk=topk),
        grid=(bsz, nq),
        in_specs=[qspec, qspec, pl.BlockSpec((tb, LANES), lambda b, i: (b * nq + i, 0)),
                  seq_spec(4), seq_spec(4), seq_spec(2),
                  pl.BlockSpec(tri.shape, lambda b, i: (0, 0))],
        out_specs=qspec,
        out_shape=jax.ShapeDtypeStruct((n, wide), F32),
        scratch_shapes=[pltpu.VMEM((seq // DSA_KC, tb, DSA_KC), jnp.int32),
                        pltpu.VMEM((A_HEADS, tb, 1), F32), pltpu.VMEM((A_HEADS, tb, 1), F32),
                        pltpu.VMEM((A_HEADS // 2, tb, LANES), F32)],
        compiler_params=pltpu.CompilerParams(
            dimension_semantics=("arbitrary", "arbitrary"), vmem_limit_bytes=VMEM_LIMIT),
        name="dsa_attn",
    )(qn, iqn, iwn, k4, v4, ik2, tri)


def _dsa_attention(q, k, v, iq, ik, iw, q_gain, k_gain, pos):
    bsz, seq = q.shape[:2]
    topk = min(A_TOPK_MAX, seq // 4)
    rep = A_HEADS // A_KV_HEADS
    q = _rope(_rms_norm(q, q_gain), pos).astype(F32) * HEAD_DIM ** -0.5
    k = _rope(_rms_norm(k, k_gain), pos).astype(F32)
    v = v.astype(F32)
    iq = _rope(iq, pos).astype(F32) * IDX_DIM ** -0.5
    ik = _rope(ik[:, :, None, :], pos)[:, :, 0, :].astype(F32)
    iw = iw.astype(F32) * IDX_HEADS ** -0.5
    key_pos = jnp.arange(seq)
    gather = jax.vmap(lambda t, idx: t[idx])

    def block(args):
        qb, iqb, iwb, start = args
        tq = start + jnp.arange(ATTN_BLOCK)
        rel = jax.nn.relu(jnp.einsum('bqhd,bsd->bqhs', iqb, ik))
        score = jnp.einsum('bqh,bqhs->bqs', iwb, rel)
        causal = key_pos[None, :] <= tq[:, None]
        score = jnp.where(causal[None], score, -jnp.inf)
        _, sel = lax.top_k(score, topk)
        valid = sel <= tq[None, :, None]
        k_sel = gather(k, sel)
        v_sel = gather(v, sel)
        qg = qb.reshape(bsz, ATTN_BLOCK, A_KV_HEADS, rep, HEAD_DIM)
        s = jnp.einsum('bqgrd,bqkgd->bqgrk', qg, k_sel)
        p = _masked_softmax(s, valid[:, :, None, None, :])
        o = jnp.einsum('bqgrk,bqkgd->bqgrd', p, v_sel)
        return o.reshape(bsz, ATTN_BLOCK, A_HEADS * HEAD_DIM)

    starts = jnp.arange(seq // ATTN_BLOCK) * ATTN_BLOCK
    out = lax.map(block, (_to_blocks(q, ATTN_BLOCK), _to_blocks(iq, ATTN_BLOCK),
                          _to_blocks(iw, ATTN_BLOCK), starts))
    return out.swapaxes(0, 1).reshape(bsz, seq, A_HEADS * HEAD_DIM)


S5_STATES = B_GROUPS * B_STATE
S5_CHUNK = 64


def _s5_kernel(u_ref, bmat_ref, cmat_ref, a_re_ref, a_im_ref, d_ref, wg_ref, bg_ref, o_ref, h_scr):
    rows = u_ref.shape[0]
    nb = SUBLANES
    ns = S5_STATES

    @pl.when(pl.program_id(0) == 0)
    def _():
        h_scr[0:nb, :] = jnp.zeros((nb, 2 * ns), F32)

    u = u_ref[...]
    h_scr[nb:, :] = jnp.dot(u.astype(BF16), bmat_ref[...], preferred_element_type=F32)
    a_re = a_re_ref[...]
    a_im = a_im_ref[...]

    def step(t, carry):
        prev = pl.multiple_of(t * nb, nb)
        cur = pl.multiple_of(t * nb + nb, nb)
        p_re = h_scr[pl.ds(prev, nb), 0:ns]
        p_im = h_scr[pl.ds(prev, nb), ns:]
        h_scr[pl.ds(cur, nb), 0:ns] = a_re * p_re - a_im * p_im + h_scr[pl.ds(cur, nb), 0:ns]
        h_scr[pl.ds(cur, nb), ns:] = a_re * p_im + a_im * p_re + h_scr[pl.ds(cur, nb), ns:]
        return carry

    lax.fori_loop(0, rows // nb, step, 0)
    h_all = h_scr[nb:, :]
    h_scr[0:nb, :] = h_scr[rows:, :]
    y = jnp.dot(h_all.astype(BF16), cmat_ref[...], preferred_element_type=F32) + d_ref[...] * u
    y = jax.nn.gelu(y)
    gate = jnp.dot(y.astype(BF16), wg_ref[...], preferred_element_type=F32) + bg_ref[...]
    o_ref[...] = y * jax.nn.sigmoid(gate)


def _s5_operators(log_dt, a_re, a_im, b_re, b_im, c_re, c_im):
    dt = jnp.exp(log_dt.astype(F32))[:, None]
    lr, li = a_re.astype(F32), a_im.astype(F32)
    mag = jnp.exp(dt * lr)
    ab_re = mag * jnp.cos(dt * li)
    ab_im = mag * jnp.sin(dt * li)
    den = lr * lr + li * li
    f_re = ((ab_re - 1.0) * lr + ab_im * li) / den
    f_im = (ab_im * lr - (ab_re - 1.0) * li) / den
    br, bi = b_re.astype(F32), b_im.astype(F32)
    bb_re = f_re[..., None] * br - f_im[..., None] * bi
    bb_im = f_re[..., None] * bi + f_im[..., None] * br
    eye = jnp.eye(B_GROUPS, dtype=F32)
    blk_in = lambda m: jnp.einsum('gpc,gh->gchp', m, eye).reshape(B_WIDTH, S5_STATES)
    bmat = jnp.concatenate([blk_in(bb_re), blk_in(bb_im)], axis=1)
    blk_out = lambda m: jnp.einsum('gcp,gh->gphc', m.astype(F32), eye).reshape(S5_STATES, B_WIDTH)
    cmat = jnp.concatenate([blk_out(c_re), -blk_out(c_im)], axis=0)
    bcast = lambda m: jnp.broadcast_to(m.reshape(1, S5_STATES), (SUBLANES, S5_STATES))
    return bmat.astype(BF16), cmat.astype(BF16), bcast(ab_re), bcast(ab_im)


def _s5_mixer(u, log_dt, a_re, a_im, b_re, b_im, c_re, c_im, d_skip, w_glu, b_glu):
    bsz, seq, width = u.shape
    assert bsz == SUBLANES and width == B_WIDTH and seq % S5_CHUNK == 0
    bmat, cmat, ab_re, ab_im = _s5_operators(log_dt, a_re, a_im, b_re, b_im, c_re, c_im)
    u_tm = u.transpose(1, 0, 2).reshape(seq * bsz, width)
    rows = S5_CHUNK * bsz
    full = lambda shape: pl.BlockSpec(shape, lambda i: (0,) * len(shape))
    out = pl.pallas_call(
        _s5_kernel,
        grid=(seq // S5_CHUNK,),
        in_specs=[pl.BlockSpec((rows, width), lambda i: (i, 0)),
                  full(bmat.shape), full(cmat.shape), full(ab_re.shape), full(ab_im.shape),
                  full((1, width)), full((width, width)), full((1, width))],
        out_specs=pl.BlockSpec((rows, width), lambda i: (i, 0)),
        out_shape=jax.ShapeDtypeStruct((seq * bsz, width), F32),
        scratch_shapes=[pltpu.VMEM((rows + bsz, 2 * S5_STATES), F32)],
        compiler_params=pltpu.CompilerParams(
            dimension_semantics=("arbitrary",), vmem_limit_bytes=VMEM_LIMIT),
        name="s5_scan",
    )(u_tm, bmat, cmat, ab_re, ab_im, d_skip.astype(F32).reshape(1, width),
      w_glu.astype(BF16), b_glu.astype(F32).reshape(1, width))
    return out.reshape(seq, bsz, width).transpose(1, 0, 2)


def _s5_ssm(u, log_dt, a_re, a_im, b_re, b_im, c_re, c_im, d_skip, w_glu, b_glu):
    bsz, seq = u.shape[:2]
    uf = u.astype(F32).reshape(bsz, seq, B_GROUPS, B_GROUP)
    dt = jnp.exp(log_dt.astype(F32))[:, None]
    lr, li = a_re.astype(F32), a_im.astype(F32)
    mag = jnp.exp(dt * lr)
    ab_re = mag * jnp.cos(dt * li)
    ab_im = mag * jnp.sin(dt * li)
    den = lr * lr + li * li
    f_re = ((ab_re - 1.0) * lr + ab_im * li) / den
    f_im = (ab_im * lr - (ab_re - 1.0) * li) / den
    br, bi = b_re.astype(F32), b_im.astype(F32)
    bb_re = f_re[..., None] * br - f_im[..., None] * bi
    bb_im = f_re[..., None] * bi + f_im[..., None] * br
    x_re = jnp.einsum('gpc,bsgc->bsgp', bb_re, uf)
    x_im = jnp.einsum('gpc,bsgc->bsgp', bb_im, uf)
    a_re_t = jnp.broadcast_to(ab_re, x_re.shape)
    a_im_t = jnp.broadcast_to(ab_im, x_im.shape)

    def combine(e1, e2):
        a1r, a1i, b1r, b1i = e1
        a2r, a2i, b2r, b2i = e2
        return (a1r * a2r - a1i * a2i, a1r * a2i + a1i * a2r,
                a2r * b1r - a2i * b1i + b2r, a2r * b1i + a2i * b1r + b2i)

    _, _, h_re, h_im = lax.associative_scan(combine, (a_re_t, a_im_t, x_re, x_im), axis=1)
    y = (jnp.einsum('gcp,bsgp->bsgc', c_re.astype(F32), h_re)
         - jnp.einsum('gcp,bsgp->bsgc', c_im.astype(F32), h_im)
         + d_skip.astype(F32) * uf)
    y = jax.nn.gelu(y.reshape(bsz, seq, B_WIDTH))
    return y * jax.nn.sigmoid(y @ w_glu.astype(F32) + b_glu.astype(F32))


NSA_KC = 512
NSA_WIN_SPAN = WIN + ATTN_BLOCK


def _nsa_prep_kernel(q_ref, ks_ref, vs_ref, kw_ref, vw_ref, gl_ref, cos_ref, sin_ref, qg_ref, kg_ref,
                     hsum_ref, rot_ref, swap_ref, qn_ref, ks4_ref, vs4_ref, kw4_ref, vw4_ref, gate_ref):
    cos1, sin1 = cos_ref[...], sin_ref[...]
    reps = q_ref.shape[1] // LANES
    cos4, sin4 = jnp.tile(cos1, (1, reps)), jnp.tile(sin1, (1, reps))
    hsum, rot = hsum_ref[...], rot_ref[...]
    hsum1, rot1 = hsum[0:LANES, 0:LANES], rot[0:LANES, 0:LANES]
    swap = swap_ref[...]
    q = _head_rope(_head_rms(q_ref[...], qg_ref[...], hsum), cos4, sin4, rot) * HEAD_DIM ** -0.5
    qn_ref[...] = q.astype(BF16)
    for src, dst, is_key in ((ks_ref, ks4_ref, True), (vs_ref, vs4_ref, False),
                             (kw_ref, kw4_ref, True), (vw_ref, vw4_ref, False)):
        t = src[...]
        if is_key:
            t = _head_rope(_head_rms(t, kg_ref[...], hsum1), cos1, sin1, rot1)
        for n, part in enumerate(_split_pair(t.astype(BF16), swap)):
            dst[n] = part
    gate_ref[...] = jax.nn.sigmoid(gl_ref[...])


def _nsa_cmp_kernel(hk_ref, hv_ref, pos_ref, kw1_ref, kw2_ref, vw1_ref, vw2_ref, cos_ref, sin_ref,
                    kg_ref, hsum_ref, rot_ref, swap_ref, kc_ref, vc_ref):
    def mlp(h_ref, w1_ref, w2_ref):
        h = h_ref[0]
        first = jnp.dot((h + pos_ref[0:1, :]).astype(BF16), w1_ref[0], preferred_element_type=F32)
        second = jnp.dot((h + pos_ref[1:2, :]).astype(BF16), w1_ref[1], preferred_element_type=F32)
        z = jax.nn.gelu(first + pltpu.roll(second, second.shape[0] - 1, axis=0))
        return jnp.dot(z.astype(BF16), w2_ref[...], preferred_element_type=F32)

    swap = swap_ref[...]
    kc = _head_rope(_head_rms(mlp(hk_ref, kw1_ref, kw2_ref), kg_ref[...], hsum_ref[...]),
                    cos_ref[...], sin_ref[...], rot_ref[...])
    kc_a, kc_b, _, _ = _split_pair(kc.astype(BF16), swap)
    kc_ref[0, 0] = kc_a
    kc_ref[0, 1] = kc_b
    vc_a, vc_b, _, _ = _split_pair(mlp(hv_ref, vw1_ref, vw2_ref).astype(BF16), swap)
    vc_ref[0, 0] = vc_a
    vc_ref[0, 1] = vc_b


def _softmax_rows(s, mask):
    sm = jnp.where(mask > 0.0, s, MASKED)
    p = jnp.exp(sm - jnp.max(sm, axis=1, keepdims=True)) * mask
    return p / jnp.maximum(jnp.sum(p, axis=1, keepdims=True), 1e-30)


def _nsa_attn_kernel(q_ref, gate_ref, kc_ref, vc_ref, ks4_ref, vs4_ref, kw4_ref, vw4_ref,
                     m2s_ref, blk_ref, gx_ref, o_ref, m_scr, l_scr, acc_scr, oc_scr, ow_scr, *, top_n):
    tb = q_ref.shape[0]
    kc = NSA_KC
    qi = pl.program_id(1)
    start = qi * tb
    tq = start + lax.broadcasted_iota(jnp.int32, (tb, 1), 0)
    low = lax.broadcasted_iota(jnp.int32, (tb, LANES), 1) < HEAD_DIM
    heads_per_group = C_HEADS // C_KV_HEADS
    pairs = C_HEADS // 2

    n_cmp = kc_ref.shape[3]
    cmp_end = lax.broadcasted_iota(jnp.int32, (tb, n_cmp), 1) * CMP_STRIDE + (CMP_LEN - 1)
    cmask = jnp.where(cmp_end <= tq, 1.0, 0.0)
    p_sum = [jnp.zeros((tb, n_cmp), F32) for _ in range(C_KV_HEADS)]
    for j in range(pairs):
        g = (2 * j) // heads_per_group
        qp = q_ref[:, j * LANES:(j + 1) * LANES]
        p_a = _softmax_rows(_dot_nt(qp, kc_ref[0, g, 0]), cmask)
        p_b = _softmax_rows(_dot_nt(qp, kc_ref[0, g, 1]), cmask)
        p_sum[g] = p_sum[g] + p_a + p_b
        oc_scr[j] = (jnp.dot(p_a.astype(BF16), vc_ref[0, g, 0], preferred_element_type=F32)
                     + jnp.dot(p_b.astype(BF16), vc_ref[0, g, 1], preferred_element_type=F32))

    n_slc = m2s_ref.shape[0]
    slc_id = lax.broadcasted_iota(jnp.int32, (n_slc, tb), 0)
    cur = (start + lax.broadcasted_iota(jnp.int32, (n_slc, tb), 1)) // SLC_LEN
    forced = jnp.where(slc_id == cur, 1.0, jnp.where(slc_id == 0, 1.0, 0.0))
    chosen = []
    for g in range(C_KV_HEADS):
        hi = p_sum[g].astype(BF16)
        lo = (p_sum[g] - hi.astype(F32)).astype(BF16)
        imp = _dot_nt(m2s_ref[...], hi) + _dot_nt(m2s_ref[...], lo)
        imp = jnp.where(forced > 0.0, FORCE_SCORE, imp)
        imp = jnp.where(slc_id <= cur, imp, NEG_INF)
        picked = jnp.zeros((n_slc, tb), F32)
        for _ in range(top_n):
            m = jnp.max(imp, axis=0, keepdims=True)
            first = jnp.min(jnp.where(imp == m, slc_id, n_slc), axis=0, keepdims=True)
            hit = slc_id == first
            picked = jnp.where(hit, 1.0, picked)
            imp = jnp.where(hit, NEG_INF, imp)
        chosen.append(picked.T.astype(BF16))

    m_scr[...] = jnp.full(m_scr.shape, MASKED, F32)
    l_scr[...] = jnp.zeros(l_scr.shape, F32)
    acc_scr[...] = jnp.zeros(acc_scr.shape, F32)
    lane_k = lax.broadcasted_iota(jnp.int32, (tb, kc), 1)

    def slc_chunk(c, carry):
        ks = pl.multiple_of(c * kc, kc)
        causal = ks + lane_k <= tq
        sel = [jnp.where(causal, jnp.dot(chosen[g], blk_ref[c], preferred_element_type=F32), 0.0)
               for g in range(C_KV_HEADS)]

        def head(s, h, g):
            sm = jnp.where(sel[g] > 0.0, s, MASKED)
            m_old = m_scr[h]
            m_new = jnp.maximum(m_old, jnp.max(sm, axis=1, keepdims=True))
            p = jnp.exp(sm - m_new) * sel[g]
            alpha = jnp.exp(m_old - m_new)
            l_scr[h] = alpha * l_scr[h] + jnp.sum(p, axis=1, keepdims=True)
            m_scr[h] = m_new
            return p.astype(BF16), alpha

        for j in range(pairs):
            g = (2 * j) // heads_per_group
            qp = q_ref[:, j * LANES:(j + 1) * LANES]
            p_a, alpha_a = head(_dot_nt(qp, ks4_ref[2 * g, pl.ds(ks, kc), :]), 2 * j, g)
            p_b, alpha_b = head(_dot_nt(qp, ks4_ref[2 * g + 1, pl.ds(ks, kc), :]), 2 * j + 1, g)
            pv = (jnp.dot(p_a, vs4_ref[2 * g, pl.ds(ks, kc), :], preferred_element_type=F32)
                  + jnp.dot(p_b, vs4_ref[2 * g + 1, pl.ds(ks, kc), :], preferred_element_type=F32))
            acc_scr[j] = jnp.where(low, alpha_a, alpha_b) * acc_scr[j] + pv
        return carry

    lax.fori_loop(0, qi // (kc // tb) + 1, slc_chunk, 0)

    ws = pl.multiple_of(jnp.maximum(start - WIN, 0), tb)
    wpos = ws + lax.broadcasted_iota(jnp.int32, (tb, NSA_WIN_SPAN), 1)
    wmask = jnp.where(wpos <= tq, jnp.where(wpos > tq - WIN, 1.0, 0.0), 0.0)
    for j in range(pairs):
        g = (2 * j) // heads_per_group
        qp = q_ref[:, j * LANES:(j + 1) * LANES]
        p_a = _softmax_rows(_dot_nt(qp, kw4_ref[2 * g, pl.ds(ws, NSA_WIN_SPAN), :]), wmask)
        p_b = _softmax_rows(_dot_nt(qp, kw4_ref[2 * g + 1, pl.ds(ws, NSA_WIN_SPAN), :]), wmask)
        ow_scr[j] = (jnp.dot(p_a.astype(BF16), vw4_ref[2 * g, pl.ds(ws, NSA_WIN_SPAN), :],
                             preferred_element_type=F32)
                     + jnp.dot(p_b.astype(BF16), vw4_ref[2 * g + 1, pl.ds(ws, NSA_WIN_SPAN), :],
                               preferred_element_type=F32))

    gates = gate_ref[...]
    g_cmp, g_slc, g_win = (_exact_dot(gates, gx_ref[n]) for n in range(3))
    for j in range(pairs):
        cols = slice(j * LANES, (j + 1) * LANES)
        denom = jnp.maximum(jnp.where(low, l_scr[2 * j], l_scr[2 * j + 1]), 1e-30)
        o_ref[:, cols] = (g_cmp[:, cols] * oc_scr[j] + g_slc[:, cols] * (acc_scr[j] / denom)
                          + g_win[:, cols] * ow_scr[j])


def _nsa_mixer(q, kcmp, vcmp, kslc, vslc, kwin, vwin, gl, bsz, seq,
               q_gain, k_gain, cmp_pos, k_w1, k_w2, v_w1, v_w2, tm=512):
    n = bsz * seq
    tb = ATTN_BLOCK
    assert seq % NSA_KC == 0 and seq % tm == 0
    n_cmp = seq // CMP_STRIDE
    wide = C_HEADS * HEAD_DIM
    cos, sin = _rope_tables(seq, LANES)
    hsum, rot, swap = _head_matrices(wide)
    hsum1, rot1 = hsum[:LANES, :LANES], rot[:LANES, :LANES]
    full = lambda a: pl.BlockSpec(a.shape, lambda *i: (0,) * a.ndim)
    rows = lambda w: pl.BlockSpec((tm, w), lambda i: (i, 0))
    quad = pl.BlockSpec((4, tm, LANES), lambda i: (0, i, 0))
    pos_spec = pl.BlockSpec((tm, LANES), lambda i: (i % (seq // tm), 0))
    gq = jnp.tile(q_gain.astype(F32), C_HEADS).reshape(1, wide)
    gk = jnp.tile(k_gain.astype(F32), C_KV_HEADS).reshape(1, LANES)
    gl_pad = jnp.pad(gl, ((0, 0), (0, LANES - gl.shape[1])))
    params1 = pltpu.CompilerParams(dimension_semantics=("arbitrary",), vmem_limit_bytes=VMEM_LIMIT)
    quad_shape = jax.ShapeDtypeStruct((4, n, LANES), BF16)
    qn, ks4, vs4, kw4, vw4, gates = pl.pallas_call(
        _nsa_prep_kernel,
        grid=(n // tm,),
        in_specs=[rows(wide)] + [rows(LANES)] * 5 + [pos_spec, pos_spec, full(gq), full(gk),
                                                     full(hsum), full(rot), full(swap)],
        out_specs=[rows(wide), quad, quad, quad, quad, rows(LANES)],
        out_shape=[jax.ShapeDtypeStruct((n, wide), BF16), quad_shape, quad_shape, quad_shape,
                   quad_shape, jax.ShapeDtypeStruct((n, LANES), F32)],
        compiler_params=params1,
        name="nsa_prep",
    )(q, kslc, vslc, kwin, vwin, gl_pad, cos, sin, gq, gk, hsum, rot, swap)

    row_w = CMP_STRIDE * HEAD_DIM
    to_rows = lambda t: (t.reshape(bsz, seq, C_KV_HEADS, HEAD_DIM).transpose(0, 2, 1, 3)
                         .reshape(bsz * C_KV_HEADS, n_cmp, row_w))
    pos2 = cmp_pos.astype(F32).reshape(2, row_w)
    w1_halves = lambda w: w.astype(BF16).reshape(2, row_w, CMP_HIDDEN)
    w2_pad = lambda w: jnp.pad(w.astype(BF16), ((0, 0), (0, LANES - HEAD_DIM)))
    half = HEAD_DIM // 2
    cmp_end = jnp.arange(n_cmp, dtype=F32) * CMP_STRIDE + (CMP_LEN - 1)
    ang = cmp_end[:, None] * (ROPE_THETA ** (-jnp.arange(half, dtype=F32) / half))[None, :]
    cos_c, sin_c = jnp.tile(jnp.cos(ang), (1, LANES // half)), jnp.tile(jnp.sin(ang), (1, LANES // half))
    seq_rows = pl.BlockSpec((1, n_cmp, row_w), lambda i: (i, 0, 0))
    pair_out = pl.BlockSpec((1, 2, n_cmp, LANES), lambda i: (i, 0, 0, 0))
    pair_shape = jax.ShapeDtypeStruct((bsz * C_KV_HEADS, 2, n_cmp, LANES), BF16)
    cmp_in = (to_rows(kcmp), to_rows(vcmp), pos2, w1_halves(k_w1), w2_pad(k_w2), w1_halves(v_w1),
              w2_pad(v_w2), cos_c, sin_c, gk, hsum1, rot1, swap)
    kc2, vc2 = pl.pallas_call(
        _nsa_cmp_kernel,
        grid=(bsz * C_KV_HEADS,),
        in_specs=[seq_rows, seq_rows] + [full(a) for a in cmp_in[2:]],
        out_specs=[pair_out, pair_out],
        out_shape=[pair_shape, pair_shape],
        compiler_params=params1,
        name="nsa_compress",
    )(*cmp_in)
    kc2 = kc2.reshape(bsz, C_KV_HEADS, 2, n_cmp, LANES)
    vc2 = vc2.reshape(bsz, C_KV_HEADS, 2, n_cmp, LANES)

    n_slc = seq // SLC_LEN
    ratio, span = SLC_LEN // CMP_STRIDE, CMP_LEN // CMP_STRIDE
    off = np.arange(n_cmp)[None, :] - ratio * np.arange(n_slc)[:, None]
    m2s = np.maximum(np.minimum(ratio - 1, off) - np.maximum(0, off - span + 1) + 1, 0)
    m2s = np.pad(m2s, ((0, LANES - n_slc), (0, 0)))
    key_blk = np.arange(seq) // SLC_LEN
    blk = (np.arange(LANES)[:, None] == key_blk[None, :]).reshape(LANES, seq // NSA_KC, NSA_KC)
    blk = jnp.asarray(blk.transpose(1, 0, 2), BF16)
    col = np.arange(wide) // HEAD_DIM
    gx = np.stack([np.arange(LANES)[:, None] == (3 * col + br)[None, :] for br in range(3)])
    gx = jnp.asarray(gx, BF16)
    m2s = jnp.asarray(m2s, BF16)
    nq = seq // tb
    qspec = pl.BlockSpec((tb, wide), lambda b, i: (b * nq + i, 0))
    cmp_spec = pl.BlockSpec((1, C_KV_HEADS, 2, n_cmp, LANES), lambda b, i: (b, 0, 0, 0, 0))
    seq_spec = pl.BlockSpec((4, seq, LANES), lambda b, i: (0, b, 0))
    assert n_slc <= LANES
    return pl.pallas_call(
        functools.partial(_nsa_attn_kernel, top_n=min(SLC_TOPN, n_slc)),
        grid=(bsz, nq),
        in_specs=[qspec, pl.BlockSpec((tb, LANES), lambda b, i: (b * nq + i, 0)), cmp_spec, cmp_spec,
                  seq_spec, seq_spec, seq_spec, seq_spec, full(m2s), full(blk), full(gx)],
        out_specs=qspec,
        out_shape=jax.ShapeDtypeStruct((n, wide), F32),
        scratch_shapes=[pltpu.VMEM((C_HEADS, tb, 1), F32), pltpu.VMEM((C_HEADS, tb, 1), F32),
                        pltpu.VMEM((C_HEADS // 2, tb, LANES), F32),
                        pltpu.VMEM((C_HEADS // 2, tb, LANES), F32),
                        pltpu.VMEM((C_HEADS // 2, tb, LANES), F32)],
        compiler_params=pltpu.CompilerParams(
            dimension_semantics=("arbitrary", "arbitrary"), vmem_limit_bytes=PEER_VMEM_LIMIT),
        name="nsa_attn",
    )(qn, gates, kc2, vc2, ks4, vs4, kw4, vw4, m2s, blk, gx)


def _compress_blocks(t, cmp_pos, w1, w2):
    bsz, seq, groups, dh = t.shape
    n_cmp = (seq - CMP_LEN) // CMP_STRIDE + 1
    idx = jnp.arange(n_cmp)[:, None] * CMP_STRIDE + jnp.arange(CMP_LEN)[None, :]
    blocks = t.astype(F32)[:, idx] + cmp_pos.astype(F32)[:, None, :]
    blocks = blocks.transpose(0, 1, 3, 2, 4).reshape(bsz, n_cmp, groups, CMP_LEN * dh)
    return jax.nn.gelu(blocks @ w1.astype(F32)) @ w2.astype(F32)


def _nsa_attention(q, k_cmp, v_cmp, k_slc, v_slc, k_win, v_win, gate_logits,
                   q_gain, k_gain, cmp_pos, k_w1, k_w2, v_w1, v_w2, pos):
    bsz, seq = q.shape[:2]
    rep = C_HEADS // C_KV_HEADS
    n_cmp = (seq - CMP_LEN) // CMP_STRIDE + 1
    n_slc = seq // SLC_LEN
    top_n = min(SLC_TOPN, n_slc)
    q = _rope(_rms_norm(q, q_gain), pos).astype(F32) * HEAD_DIM ** -0.5
    cmp_end = jnp.arange(n_cmp) * CMP_STRIDE + (CMP_LEN - 1)
    kc = _rope(_rms_norm(_compress_blocks(k_cmp, cmp_pos, k_w1, k_w2), k_gain), cmp_end)
    vc = _compress_blocks(v_cmp, cmp_pos, v_w1, v_w2)
    ks = _rope(_rms_norm(k_slc, k_gain), pos).astype(F32)
    ks = ks.reshape(bsz, n_slc, SLC_LEN, C_KV_HEADS, HEAD_DIM).transpose(0, 3, 1, 2, 4)
    vs = v_slc.astype(F32).reshape(bsz, n_slc, SLC_LEN, C_KV_HEADS, HEAD_DIM).transpose(0, 3, 1, 2, 4)
    pad = ((0, 0), (WIN, 0), (0, 0), (0, 0))
    kw = jnp.pad(_rope(_rms_norm(k_win, k_gain), pos).astype(F32), pad)
    vw = jnp.pad(v_win.astype(F32), pad)
    gates = jax.nn.sigmoid(gate_logits.astype(F32))
    ratio = SLC_LEN // CMP_STRIDE
    span = CMP_LEN // CMP_STRIDE
    off = jnp.arange(n_cmp)[:, None] - ratio * jnp.arange(n_slc)[None, :]
    cmp_to_slc = jnp.maximum(
        jnp.minimum(ratio - 1, off) - jnp.maximum(0, off - span + 1) + 1, 0).astype(F32)
    slc_ids = jnp.arange(n_slc)
    tok_off = jnp.arange(SLC_LEN)
    win_off = jnp.arange(WIN + C_BLOCK)
    gather = jax.vmap(jax.vmap(lambda t, idx: t[idx]))

    def block(args):
        qb, gb, start = args
        tq = start + jnp.arange(C_BLOCK)
        qg = qb.reshape(bsz, C_BLOCK, C_KV_HEADS, rep, HEAD_DIM)
        gb = gb.reshape(bsz, C_BLOCK, C_KV_HEADS, rep, 3)
        s_c = jnp.einsum('bqgrd,bcgd->bqgrc', qg, kc)
        p_c = _masked_softmax(s_c, (cmp_end[None, :] <= tq[:, None])[None, :, None, None, :])
        o_c = jnp.einsum('bqgrc,bcgd->bqgrd', p_c, vc)
        imp = jnp.einsum('bqgrc,cj->bqgj', p_c, cmp_to_slc)
        cur = tq // SLC_LEN
        forced = (slc_ids[None, :] == cur[:, None]) | (slc_ids[None, :] == 0)
        admissible = slc_ids[None, :] <= cur[:, None]
        imp = jnp.where(forced[None, :, None, :], FORCE_SCORE, imp)
        imp = jnp.where(admissible[None, :, None, :], imp, -jnp.inf)
        _, sel = lax.top_k(imp, top_n)
        sel_g = sel.transpose(0, 2, 1, 3)
        k_sel = gather(ks, sel_g)
        v_sel = gather(vs, sel_g)
        s_s = jnp.einsum('bqgrd,bgqnld->bqgrnl', qg, k_sel)
        kpos = sel[..., None] * SLC_LEN + tok_off
        smask = (kpos <= tq[None, :, None, None, None]).reshape(bsz, C_BLOCK, C_KV_HEADS, 1, -1)
        p_s = _masked_softmax(s_s.reshape(bsz, C_BLOCK, C_KV_HEADS, rep, -1), smask)
        o_s = jnp.einsum('bqgrnl,bgqnld->bqgrd', p_s.reshape(s_s.shape), v_sel)
        kwb = lax.dynamic_slice_in_dim(kw, start, WIN + C_BLOCK, axis=1)
        vwb = lax.dynamic_slice_in_dim(vw, start, WIN + C_BLOCK, axis=1)
        wpos = start - WIN + win_off
        wmask = ((wpos[None, :] <= tq[:, None]) & (wpos[None, :] > tq[:, None] - WIN)
                 & (wpos[None, :] >= 0))
        s_w = jnp.einsum('bqgrd,bkgd->bqgrk', qg, kwb)
        p_w = _masked_softmax(s_w, wmask[None, :, None, None, :])
        o_w = jnp.einsum('bqgrk,bkgd->bqgrd', p_w, vwb)
        o = gb[..., 0:1] * o_c + gb[..., 1:2] * o_s + gb[..., 2:3] * o_w
        return o.reshape(bsz, C_BLOCK, C_HEADS * HEAD_DIM)

    starts = jnp.arange(seq // C_BLOCK) * C_BLOCK
    out = lax.map(block, (_to_blocks(q, C_BLOCK), _to_blocks(gates, C_BLOCK), starts))
    return out.swapaxes(0, 1).reshape(bsz, seq, C_HEADS * HEAD_DIM)


GLA_BLOCK = 256


def _gla_kernel(q_ref, k_ref, v_ref, gl_ref, r_ref, wg_ref, bg_ref, og_ref, tri_ref, same_ref,
                o_ref, st_scr):
    @pl.when(pl.program_id(1) == 0)
    def _():
        st_scr[...] = jnp.zeros(st_scr.shape, F32)

    tb = q_ref.shape[0]
    logit = jnp.dot(gl_ref[...].astype(BF16), wg_ref[...], preferred_element_type=F32) + bg_ref[...]
    log_a = jax.nn.log_sigmoid(logit) * (1.0 / GATE_TAU)
    a_hi = log_a.astype(BF16)
    a_lo = (log_a - a_hi.astype(F32)).astype(BF16)
    two_term = lambda m: (jnp.dot(m, a_hi, preferred_element_type=F32)
                          + jnp.dot(m, a_lo, preferred_element_type=F32))
    tri = tri_ref[...]
    bcum = two_term(tri)
    blast = two_term(same_ref[...])
    k = k_ref[...]
    q_dec = q_ref[...] * D_KDIM ** -0.5 * jnp.exp(bcum)
    k_inv = (k * jnp.exp(-bcum)).astype(BF16)
    k_rem = k * jnp.exp(blast - bcum)
    decay = jnp.exp(blast)
    causal = tri.astype(F32)
    low = lax.broadcasted_iota(jnp.int32, (tb, LANES), 1) < D_KDIM
    row_chunk = lax.broadcasted_iota(jnp.int32, (tb, LANES), 0) // GLA_CHUNK
    gain = og_ref[...]
    for j in range(D_HEADS // 2):
        cols = slice(j * LANES, (j + 1) * LANES)
        qd, ki, kr, dec = q_dec[:, cols], k_inv[:, cols], k_rem[:, cols], decay[:, cols]
        for h, qd_h in ((2 * j, jnp.where(low, qd, 0.0).astype(BF16)),
                        (2 * j + 1, jnp.where(low, 0.0, qd).astype(BF16))):
            vcols = slice(h * D_VDIM, (h + 1) * D_VDIM)
            v_h = v_ref[:, vcols]
            att = _dot_nt(qd_h, ki) * causal
            o_h = jnp.dot(att.astype(BF16), v_h.astype(BF16), preferred_element_type=F32)
            v_t = v_h.T.astype(BF16)
            inter = []
            for n in range(tb // GLA_CHUNK):
                rows = slice(n * GLA_CHUNK, (n + 1) * GLA_CHUNK)
                state = st_scr[h]
                inter.append(_dot_nt(qd_h[rows], state.astype(BF16)))
                upd = jnp.dot(v_t, jnp.where(row_chunk == n, kr, 0.0).astype(BF16),
                              preferred_element_type=F32)
                st_scr[h] = state * dec[n * GLA_CHUNK:n * GLA_CHUNK + 1, :] + upd
            o_h = o_h + jnp.concatenate(inter, axis=0)
            o_h = o_h * lax.rsqrt(jnp.mean(o_h * o_h, axis=-1, keepdims=True) + NORM_EPS) * gain
            o_ref[:, vcols] = o_h * jax.nn.silu(r_ref[:, vcols])


def _gla_mixer(gq, gk, gv, glow, gr, bsz, seq, w_gate, b_gate, o_gain):
    n = bsz * seq
    tb = GLA_BLOCK
    assert seq % tb == 0
    kw, vw = D_HEADS * D_KDIM, D_HEADS * D_VDIM
    glow_p = jnp.pad(glow, ((0, 0), (0, LANES - GATE_RANK)))
    wg = jnp.pad(w_gate.astype(BF16), ((0, LANES - GATE_RANK), (0, 0)))
    r = np.arange(tb)
    same = (r[:, None] // GLA_CHUNK) == (r[None, :] // GLA_CHUNK)
    tri = jnp.asarray(same & (r[None, :] <= r[:, None]), BF16)
    same = jnp.asarray(same, BF16)
    nb = seq // tb
    rows = lambda w: pl.BlockSpec((tb, w), lambda b, i: (b * nb + i, 0))
    full = lambda a: pl.BlockSpec(a.shape, lambda b, i: (0,) * a.ndim)
    bg = b_gate.astype(F32).reshape(1, kw)
    og = o_gain.astype(F32).reshape(1, D_VDIM)
    return pl.pallas_call(
        _gla_kernel,
        grid=(bsz, nb),
        in_specs=[rows(kw), rows(kw), rows(vw), rows(LANES), rows(vw),
                  full(wg), full(bg), full(og), full(tri), full(same)],
        out_specs=rows(vw),
        out_shape=jax.ShapeDtypeStruct((n, vw), F32),
        scratch_shapes=[pltpu.VMEM((D_HEADS, D_VDIM, LANES), F32)],
        compiler_params=pltpu.CompilerParams(
            dimension_semantics=("arbitrary", "arbitrary"), vmem_limit_bytes=VMEM_LIMIT),
        name="gla",
    )(gq, gk, gv, glow_p, gr, wg, bg, og, tri, same)


def _gla_attention(q, k, v, g_low, r, w_gate, b_gate, o_gain):
    bsz, seq = q.shape[:2]
    n_chunk = seq // GLA_CHUNK
    logit = g_low.astype(F32) @ w_gate.astype(F32) + b_gate.astype(F32)
    log_a = jax.nn.log_sigmoid(logit) / GATE_TAU

    def chunks(t):
        return t.astype(F32).reshape(bsz, n_chunk, GLA_CHUNK, D_HEADS, -1).transpose(0, 3, 1, 2, 4)

    qc = chunks(q) * D_KDIM ** -0.5
    kc, vc, gc = chunks(k), chunks(v), chunks(log_a)
    bcum = jnp.cumsum(gc, axis=3)
    blast = bcum[:, :, :, -1:, :]
    q_dec = qc * jnp.exp(bcum)
    k_inv = kc * jnp.exp(-bcum)
    causal = jnp.tril(jnp.ones((GLA_CHUNK, GLA_CHUNK), dtype=bool))
    att = jnp.where(causal, jnp.einsum('bhncd,bhnsd->bhncs', q_dec, k_inv), 0.0)
    o_intra = jnp.einsum('bhncs,bhnse->bhnce', att, vc)
    upd = jnp.einsum('bhncd,bhnce->bhnde', kc * jnp.exp(blast - bcum), vc)
    decay = jnp.exp(blast[:, :, :, 0, :])

    def step(state, inp):
        dec, u = inp
        return dec[..., None] * state + u, state

    init = jnp.zeros((bsz, D_HEADS, D_KDIM, D_VDIM), F32)
    _, prev = lax.scan(step, init, (jnp.moveaxis(decay, 2, 0), jnp.moveaxis(upd, 2, 0)))
    o_inter = jnp.einsum('bhncd,nbhde->bhnce', q_dec, prev)
    o = (o_intra + o_inter).transpose(0, 2, 3, 1, 4).reshape(bsz, seq, D_HEADS, D_VDIM)
    o = _rms_norm(o, o_gain).reshape(bsz, seq, D_HEADS * D_VDIM)
    return o * jax.nn.silu(r.astype(F32))


PEER_SLOTS = PEER_HEADS * PEER_TOPK
PEER_TT = 128
ROW_TILE = (SUBLANES, LANES)
NEG_INF = float('-inf')
ROUTE_ILP = 4


def _top16_rows(s, row_id, big):
    vals, ids = [], []
    for _ in range(PEER_TOPK):
        m = jnp.max(s, axis=0, keepdims=True)
        pick = jnp.min(jnp.where(s == m, row_id, big), axis=0, keepdims=True)
        s = jnp.where(row_id == pick, NEG_INF, s)
        vals.append(m)
        ids.append(pick)
    return vals, ids


def _peer_route_kernel(x_ref, g_ref, wq_ref, sk_ref, xn_ref, idx_ref, gate_ref, v_scr, i_scr):
    tt = x_ref.shape[0]
    x = x_ref[...]
    xn = x * lax.rsqrt(jnp.mean(x * x, axis=-1, keepdims=True) + NORM_EPS) * g_ref[...]
    xn_ref[...] = xn
    xb = xn.astype(BF16)
    key_id = lax.broadcasted_iota(jnp.int32, (PEER_KEYS, tt), 0)

    def sub_scores(step, carry):
        for u in range(ROUTE_ILP):
            hp = step * ROUTE_ILP + u
            q = jnp.dot(xb, wq_ref[hp], preferred_element_type=F32).astype(BF16)
            s = lax.dot_general(sk_ref[hp], q, (((1,), (1,)), ((), ())),
                                preferred_element_type=F32)
            vals, ids = _top16_rows(s, key_id, PEER_KEYS)
            v_scr[hp] = jnp.concatenate(vals, axis=0)
            i_scr[hp] = jnp.concatenate(ids, axis=0)
        return carry

    lax.fori_loop(0, 2 * PEER_HEADS // ROUTE_ILP, sub_scores, 0)

    n_blk = 10
    r = lax.broadcasted_iota(jnp.int32, (n_blk * SUBLANES, tt), 0)
    blk, b = r // SUBLANES, r % SUBLANES
    a_of_blk = jnp.where(blk < 2, 0, blk - 1)
    pos = jnp.where(blk == 9, (SUBLANES + b) * PEER_TOPK,
                    jnp.where(blk == 1, SUBLANES + b, a_of_blk * PEER_TOPK + b))
    b_max = jnp.where(blk < 3, 7, jnp.where(blk == 3, 4, jnp.where(blk == 4, 3, jnp.where(
        blk == 5, 2, jnp.where(blk < 9, 1, 7)))))
    live = b <= b_max

    def combine_head(h):
        v1, v2 = v_scr[2 * h], v_scr[2 * h + 1]
        i1, i2 = i_scr[2 * h], i_scr[2 * h + 1]
        lo_v, hi_v = v2[0:SUBLANES], v2[SUBLANES:]
        lo_i, hi_i = i2[0:SUBLANES], i2[SUBLANES:]
        cand = [v1[0:1] + lo_v, v1[0:1] + hi_v]
        cid = [i1[0:1] * PEER_KEYS + lo_i, i1[0:1] * PEER_KEYS + hi_i]
        for a in range(1, SUBLANES):
            cand.append(v1[a:a + 1] + lo_v)
            cid.append(i1[a:a + 1] * PEER_KEYS + lo_i)
        cand.append(v1[SUBLANES:] + v2[0:1])
        cid.append(i1[SUBLANES:] * PEER_KEYS + i2[0:1])
        cand = jnp.where(live, jnp.concatenate(cand, axis=0), NEG_INF)
        cid = jnp.concatenate(cid, axis=0)
        top_s, experts = [], []
        for _ in range(PEER_TOPK):
            m = jnp.max(cand, axis=0, keepdims=True)
            pick = jnp.min(jnp.where(cand == m, pos, PEER_TOPK * PEER_TOPK), axis=0, keepdims=True)
            hit = pos == pick
            experts.append(jnp.max(jnp.where(hit, cid, -1), axis=0, keepdims=True))
            cand = jnp.where(hit, NEG_INF, cand)
            top_s.append(m)
        top_s = jnp.concatenate(top_s, axis=0)
        ex = jnp.exp(top_s - top_s[0:1])
        gate_ref[0, h] = ex / jnp.sum(ex, axis=0, keepdims=True)
        idx_ref[0, h] = jnp.concatenate(experts, axis=0)

    def combine(step, carry):
        for u in range(ROUTE_ILP):
            combine_head(step * ROUTE_ILP + u)
        return carry

    lax.fori_loop(0, PEER_HEADS // ROUTE_ILP, combine, 0)


def _peer_route(x2d, g_ffn, wq, subkeys):
    n, dm = x2d.shape
    tt = PEER_TT
    half = PEER_QDIM // 2
    wq_t = wq.astype(BF16).reshape(dm, 2 * PEER_HEADS, half).transpose(1, 0, 2)
    sk = subkeys.astype(BF16).reshape(2 * PEER_HEADS, PEER_KEYS, half)
    tile4 = (1, PEER_HEADS, PEER_TOPK, tt)
    return pl.pallas_call(
        _peer_route_kernel,
        grid=(n // tt,),
        in_specs=[pl.BlockSpec((tt, dm), lambda i: (i, 0)),
                  pl.BlockSpec((1, dm), lambda i: (0, 0)),
                  pl.BlockSpec(wq_t.shape, lambda i: (0, 0, 0)),
                  pl.BlockSpec(sk.shape, lambda i: (0, 0, 0))],
        out_specs=[pl.BlockSpec((tt, dm), lambda i: (i, 0)),
                   pl.BlockSpec(tile4, lambda i: (i, 0, 0, 0)),
                   pl.BlockSpec(tile4, lambda i: (i, 0, 0, 0))],
        out_shape=[jax.ShapeDtypeStruct((n, dm), F32),
                   jax.ShapeDtypeStruct((n // tt,) + tile4[1:], jnp.int32),
                   jax.ShapeDtypeStruct((n // tt,) + tile4[1:], F32)],
        scratch_shapes=[pltpu.VMEM((2 * PEER_HEADS, PEER_TOPK, tt), F32),
                        pltpu.VMEM((2 * PEER_HEADS, PEER_TOPK, tt), jnp.int32)],
        compiler_params=pltpu.CompilerParams(
            dimension_semantics=("arbitrary",), vmem_limit_bytes=VMEM_LIMIT),
        name="peer_route",
    )(x2d, g_ffn.reshape(1, dm), wq_t, sk)


PEER_UNROLL = 8


def _gather_rows(idx_smem, tab_ref, slab, t):
    for i in range(PEER_SLOTS):
        slab[i * SUBLANES:(i + 1) * SUBLANES, :] = tab_ref[idx_smem[i * PEER_TT + t]]


def _peer_up_kernel(idx_hbm, xn_ref, gate_ref, tab_ref, fold_ref, out_ref, idx_smem, slab, rbuf, sem):
    tt = xn_ref.shape[0]
    cp = pltpu.make_async_copy(idx_hbm.at[pl.program_id(0)], idx_smem, sem)
    cp.start()
    cp.wait()
    eye = (lax.broadcasted_iota(jnp.int32, (LANES, LANES), 0)
           == lax.broadcasted_iota(jnp.int32, (LANES, LANES), 1))

    def step(p, carry):
        for u in range(PEER_UNROLL):
            t = p * PEER_UNROLL + u
            rows = jnp.concatenate(
                [tab_ref[idx_smem[i * PEER_TT + t]] for i in range(PEER_SLOTS)], axis=0)
            x16 = jnp.tile(xn_ref[t], (LANES // SUBLANES, 1)).astype(BF16)
            y = lax.dot_general(rows, x16, (((1,), (1,)), ((), ())),
                                preferred_element_type=F32)
            y = y.reshape(PEER_SLOTS * SUBLANES // LANES, LANES, LANES)
            rbuf[t] = jnp.sum(jnp.where(eye[None], y, 0.0), axis=1)
        return carry

    lax.fori_loop(0, tt // PEER_UNROLL, step, 0)
    r = rbuf[...].reshape(tt * SUBLANES, LANES)
    r_hi = r.astype(BF16)
    r_lo = (r - r_hi.astype(F32)).astype(BF16)
    h = (jnp.dot(r_hi, fold_ref[...], preferred_element_type=F32)
         + jnp.dot(r_lo, fold_ref[...], preferred_element_type=F32))
    out_ref[...] = gate_ref[...] * jax.nn.gelu(h)


def _peer_down_kernel(idx_hbm, gh_ref, res_ref, tab_ref, spread_ref, out_ref, idx_smem, slab, ghx, sem):
    tt = gh_ref.shape[0]
    cp = pltpu.make_async_copy(idx_hbm.at[pl.program_id(0)], idx_smem, sem)
    cp.start()
    cp.wait()
    ghx[...] = jnp.dot(gh_ref[...].astype(BF16), spread_ref[...], preferred_element_type=F32)
    width = PEER_SLOTS * SUBLANES
    diag = (lax.broadcasted_iota(jnp.int32, (SUBLANES, width), 0)
            == lax.broadcasted_iota(jnp.int32, (SUBLANES, width), 1) % SUBLANES)

    def step(p, carry):
        for u in range(PEER_UNROLL):
            t = p * PEER_UNROLL + u
            rows = jnp.concatenate(
                [tab_ref[idx_smem[i * PEER_TT + t]] for i in range(PEER_SLOTS)], axis=0)
            row = jnp.broadcast_to(ghx[pl.ds(t, 1), :], (SUBLANES, width))
            coef = jnp.where(diag, row, 0.0).astype(BF16)
            out_ref[t] = res_ref[t] + jnp.dot(coef, rows, preferred_element_type=F32)
        return carry

    lax.fori_loop(0, tt // PEER_UNROLL, step, 0)


def _peer_table(tab):
    e, dm = tab.shape
    assert dm == SUBLANES * LANES
    return tab.astype(BF16).reshape(e, SUBLANES, LANES)


def _peer_ffn_residual(x2d, g_ffn, wq, subkeys, u_tab, v_tab):
    n, dm = x2d.shape
    tt = PEER_TT
    n_tiles = n // tt
    xn, idx, gate = _peer_route(x2d, g_ffn, wq, subkeys)
    idx_tiles = idx.reshape(n_tiles, PEER_SLOTS * tt)
    gate_rows = gate.transpose(0, 3, 1, 2).reshape(n * PEER_HEADS, PEER_TOPK)
    width = PEER_SLOTS * SUBLANES
    lane = np.arange(LANES)
    fold = jnp.asarray(lane[:, None] // SUBLANES == np.arange(PEER_TOPK)[None, :], BF16)
    spread = jnp.asarray(np.arange(PEER_SLOTS)[:, None] == np.arange(width)[None, :] // SUBLANES, BF16)
    params = pltpu.CompilerParams(dimension_semantics=("arbitrary",),
                                  vmem_limit_bytes=PEER_VMEM_LIMIT)
    table_spec = pl.BlockSpec(memory_space=pltpu.VMEM)
    slab_shape = pltpu.VMEM((PEER_UNROLL, PEER_SLOTS * SUBLANES, LANES), BF16)
    idx_scratch = pltpu.SMEM((PEER_SLOTS * tt,), jnp.int32)
    row3 = (tt,) + ROW_TILE

    gh = pl.pallas_call(
        _peer_up_kernel,
        grid=(n_tiles,),
        in_specs=[pl.BlockSpec(memory_space=pl.ANY),
                  pl.BlockSpec(row3, lambda i: (i, 0, 0)),
                  pl.BlockSpec((tt * PEER_HEADS, PEER_TOPK), lambda i: (i, 0)),
                  table_spec,
                  pl.BlockSpec((LANES, PEER_TOPK), lambda i: (0, 0))],
        out_specs=pl.BlockSpec((tt * PEER_HEADS, PEER_TOPK), lambda i: (i, 0)),
        out_shape=jax.ShapeDtypeStruct((n * PEER_HEADS, PEER_TOPK), F32),
        scratch_shapes=[idx_scratch, slab_shape, pltpu.VMEM(row3, F32), pltpu.SemaphoreType.DMA(())],
        compiler_params=params,
        name="peer_up",
    )(idx_tiles,xn.reshape((n,) + ROW_TILE), gate_rows, _peer_table(u_tab), fold)

    out = pl.pallas_call(
        _peer_down_kernel,
        grid=(n_tiles,),
        in_specs=[pl.BlockSpec(memory_space=pl.ANY),
                  pl.BlockSpec((tt, PEER_SLOTS), lambda i: (i, 0)),
                  pl.BlockSpec(row3, lambda i: (i, 0, 0)),
                  table_spec,
                  pl.BlockSpec((PEER_SLOTS, width), lambda i: (0, 0))],
        out_specs=pl.BlockSpec(row3, lambda i: (i, 0, 0)),
        out_shape=jax.ShapeDtypeStruct((n,) + ROW_TILE, F32),
        scratch_shapes=[idx_scratch, slab_shape, pltpu.VMEM((tt, width), F32), pltpu.SemaphoreType.DMA(())],
        compiler_params=params,
        name="peer_down",
    )(idx_tiles,gh.reshape(n, PEER_SLOTS), x2d.reshape((n,) + ROW_TILE), _peer_table(v_tab), spread)
    return out.reshape(n, dm)


def _even_layer(x, pos, mix_norm, w_in, q_gain, k_gain, log_dt, a_re, a_im, b_re, b_im,
                c_re, c_im, d_skip, w_glu, b_glu, w_out):
    bsz, seq, dm = x.shape
    x2d = x.reshape(bsz * seq, dm)
    pieces = dict(zip(('q', 'k', 'v', 'iq', 'ik', 'iw', 'u'), _split_cols(w_in, EVEN_COLS)))
    w_lay = jnp.concatenate(
        [jnp.pad(pieces[name], ((0, 0), (0, EVEN_LAYOUT[name][1] - pieces[name].shape[1])))
         for name in sorted(EVEN_LAYOUT, key=lambda s: EVEN_LAYOUT[s][0])], axis=1)
    h = _norm_proj(x2d, mix_norm, w_lay)
    o_a = _dsa_mixer(h, bsz, seq, q_gain, k_gain)
    u0, uw = EVEN_LAYOUT['u']
    u = h[:, u0:u0 + uw].reshape(bsz, seq, uw)
    o_b = _s5_mixer(u, log_dt, a_re, a_im, b_re, b_im, c_re, c_im, d_skip, w_glu, b_glu)
    mix = jnp.concatenate([o_a, o_b.reshape(bsz * seq, uw)], axis=-1)
    return _proj_residual(mix, w_out, x2d).reshape(bsz, seq, dm)


def _odd_layer(x, pos, mix_norm, w_in, q_gain, k_gain, cmp_pos, k_w1, k_w2, v_w1, v_w2,
               w_gate, b_gate, o_gain, w_out):
    bsz, seq, dm = x.shape
    x2d = x.reshape(bsz * seq, dm)
    h = _norm_proj(x2d, mix_norm, w_in)
    (q, kc, vc, ks, vs, kw, vw, gl, gq, gk, gv, glow, gr) = _split_cols(h, ODD_COLS)
    o_c = _nsa_mixer(q, kc, vc, ks, vs, kw, vw, gl, bsz, seq, q_gain, k_gain,
                     cmp_pos, k_w1, k_w2, v_w1, v_w2)
    o_d = _gla_mixer(gq, gk, gv, glow, gr, bsz, seq, w_gate, b_gate, o_gain)
    mix = jnp.concatenate([o_c, o_d], axis=-1)
    return _proj_residual(mix, w_out, x2d).reshape(bsz, seq, dm)


def kernel(x, l0_mix_norm, l0_w_in, l0_a_q_gain, l0_a_k_gain, l0_s5_log_dt, l0_s5_a_re, l0_s5_a_im, l0_s5_b_re, l0_s5_b_im, l0_s5_c_re, l0_s5_c_im, l0_s5_d, l0_s5_w_glu, l0_s5_b_glu, l0_w_out, l0_ffn_norm, l0_peer_wq, l0_peer_subkeys, l0_peer_u, l0_peer_v, l1_mix_norm, l1_w_in, l1_c_q_gain, l1_c_k_gain, l1_nsa_cmp_pos, l1_nsa_k_w1, l1_nsa_k_w2, l1_nsa_v_w1, l1_nsa_v_w2, l1_gla_w_gate, l1_gla_b_gate, l1_gla_o_gain, l1_w_out, l1_ffn_norm, l1_peer_wq, l1_peer_subkeys, l1_peer_u, l1_peer_v):
    seq = x.shape[1]
    pos = jnp.arange(seq, dtype=jnp.int32)
    x = _even_layer(x, pos, l0_mix_norm, l0_w_in, l0_a_q_gain, l0_a_k_gain, l0_s5_log_dt,
                    l0_s5_a_re, l0_s5_a_im, l0_s5_b_re, l0_s5_b_im, l0_s5_c_re, l0_s5_c_im,
                    l0_s5_d, l0_s5_w_glu, l0_s5_b_glu, l0_w_out)
    shape = x.shape
    flat = lambda t: t.reshape(-1, shape[-1])
    x = _peer_ffn_residual(flat(x), l0_ffn_norm, l0_peer_wq, l0_peer_subkeys,
                           l0_peer_u, l0_peer_v).reshape(shape)
    x = _odd_layer(x, pos, l1_mix_norm, l1_w_in, l1_c_q_gain, l1_c_k_gain, l1_nsa_cmp_pos,
                   l1_nsa_k_w1, l1_nsa_k_w2, l1_nsa_v_w1, l1_nsa_v_w2, l1_gla_w_gate,
                   l1_gla_b_gate, l1_gla_o_gain, l1_w_out)
    return _peer_ffn_residual(flat(x), l1_ffn_norm, l1_peer_wq, l1_peer_subkeys,
                              l1_peer_u, l1_peer_v).reshape(shape)
```

```python
import functools
import math

import jax
import jax.numpy as jnp
import numpy as np
from jax import lax
from jax.experimental import pallas as pl
from jax.experimental.pallas import tpu as pltpu

F32 = jnp.float32
BF16 = jnp.bfloat16
NORM_EPS = 1e-6
ROPE_THETA = 10000.0
HEAD_DIM = 64
ATTN_BLOCK = 128
A_HEADS = 8
A_KV_HEADS = 2
IDX_HEADS = 8
IDX_DIM = 64
A_TOPK_MAX = 256
B_WIDTH = 512
B_GROUP = 16
B_GROUPS = B_WIDTH // B_GROUP
B_STATE = 64
C_HEADS = 8
C_KV_HEADS = 2
CMP_LEN = 32
CMP_STRIDE = 16
CMP_HIDDEN = 256
SLC_LEN = 64
SLC_TOPN = 16
WIN = 512
C_BLOCK = 64
FORCE_SCORE = 1e9
D_HEADS = 4
D_KDIM = 64
D_VDIM = 128
GATE_RANK = 16
GATE_TAU = 16.0
GLA_CHUNK = 32
PEER_HEADS = 8
PEER_KEYS = 128
PEER_QDIM = 256
PEER_TOPK = 16
PEER_BLOCK = 128

EVEN_COLS = (A_HEADS * HEAD_DIM, A_KV_HEADS * HEAD_DIM, A_KV_HEADS * HEAD_DIM,
             IDX_HEADS * IDX_DIM, IDX_DIM, IDX_HEADS, B_WIDTH)
ODD_COLS = (C_HEADS * HEAD_DIM,) + (C_KV_HEADS * HEAD_DIM,) * 6 + (
    C_HEADS * 3, D_HEADS * D_KDIM, D_HEADS * D_KDIM, D_HEADS * D_VDIM, GATE_RANK, D_HEADS * D_VDIM)

LANES = 128
SUBLANES = 8
MXU_DIM = 256
VMEM_LIMIT = 48 * 1024 * 1024
PEER_VMEM_LIMIT = 56 * 1024 * 1024


def _norm_proj_kernel(x_ref, g_ref, w_ref, o_ref):
    x = x_ref[...]
    y = x * lax.rsqrt(jnp.mean(x * x, axis=-1, keepdims=True) + NORM_EPS) * g_ref[...]
    o_ref[...] = jnp.dot(y.astype(BF16), w_ref[...].astype(BF16), preferred_element_type=F32)


def _norm_proj(x2d, gain, w, tm=512, tn=512):
    m, k = x2d.shape
    n = w.shape[1]
    n_pad = -(-n // tn) * tn
    w_p = jnp.pad(w, ((0, 0), (0, n_pad - n)))
    out = pl.pallas_call(
        _norm_proj_kernel,
        grid=(m // tm, n_pad // tn),
        in_specs=[pl.BlockSpec((tm, k), lambda i, j: (i, 0)),
                  pl.BlockSpec((1, k), lambda i, j: (0, 0)),
                  pl.BlockSpec((k, tn), lambda i, j: (0, j))],
        out_specs=pl.BlockSpec((tm, tn), lambda i, j: (i, j)),
        out_shape=jax.ShapeDtypeStruct((m, n_pad), F32),
        compiler_params=pltpu.CompilerParams(
            dimension_semantics=("arbitrary", "arbitrary"), vmem_limit_bytes=VMEM_LIMIT),
    )(x2d, gain.reshape(1, k), w_p)
    return out[:, :n]


def _proj_res_kernel(a_ref, w_ref, r_ref, o_ref):
    o_ref[...] = r_ref[...] + jnp.dot(a_ref[...].astype(BF16), w_ref[...].astype(BF16),
                                      preferred_element_type=F32)


def _proj_residual(a2d, w, res2d, tm=512):
    m, k = a2d.shape
    n = w.shape[1]
    return pl.pallas_call(
        _proj_res_kernel,
        grid=(m // tm,),
        in_specs=[pl.BlockSpec((tm, k), lambda i: (i, 0)),
                  pl.BlockSpec((k, n), lambda i: (0, 0)),
                  pl.BlockSpec((tm, n), lambda i: (i, 0))],
        out_specs=pl.BlockSpec((tm, n), lambda i: (i, 0)),
        out_shape=jax.ShapeDtypeStruct((m, n), F32),
        compiler_params=pltpu.CompilerParams(
            dimension_semantics=("arbitrary",), vmem_limit_bytes=VMEM_LIMIT),
    )(a2d, w, res2d)


def _rms_norm(x, gain):
    xf = x.astype(F32)
    y = xf * lax.rsqrt(jnp.mean(xf * xf, axis=-1, keepdims=True) + NORM_EPS)
    return (y * gain.astype(F32)).astype(x.dtype)


def _rope(x, pos):
    half = x.shape[-1] // 2
    inv_freq = ROPE_THETA ** (-jnp.arange(half, dtype=F32) / half)
    ang = pos.astype(F32)[:, None] * inv_freq[None, :]
    cos = jnp.cos(ang)[:, None, :]
    sin = jnp.sin(ang)[:, None, :]
    xf = x.astype(F32)
    x1, x2 = xf[..., :half], xf[..., half:]
    return jnp.concatenate([x1 * cos - x2 * sin, x1 * sin + x2 * cos], axis=-1).astype(x.dtype)


def _masked_softmax(s, mask):
    s = jnp.where(mask, s.astype(F32), -1e30)
    m = jnp.max(s, axis=-1, keepdims=True)
    p = jnp.exp(s - m) * mask
    return p / jnp.maximum(jnp.sum(p, axis=-1, keepdims=True), 1e-30)


def _split_cols(h, sizes):
    return jnp.split(h, np.cumsum(sizes)[:-1].tolist(), axis=-1)


def _to_blocks(t, blk):
    return t.reshape(t.shape[0], t.shape[1] // blk, blk, *t.shape[2:]).swapaxes(0, 1)


INT_MIN = -2 ** 31
MASKED = -1e30
DSA_KC = 512
EVEN_LAYOUT = dict(q=(0, 512), iq=(512, 512), u=(1024, 512), k=(1536, 128), v=(1664, 128),
                   ik=(1792, 128), iw=(1920, 128))
EVEN_WIDTH = 2048


def _exact_dot(x, m):
    hi = x.astype(BF16)
    lo = (x - hi.astype(F32)).astype(BF16)
    return (jnp.dot(hi, m, preferred_element_type=F32) + jnp.dot(lo, m, preferred_element_type=F32))


def _head_rms(x, gain, hsum):
    return x * lax.rsqrt(_exact_dot(x * x, hsum) + NORM_EPS) * gain


def _head_rope(x, cos, sin, rot):
    return x * cos + _exact_dot(x, rot) * sin


def _split_pair(x, swap):
    lane = lax.broadcasted_iota(jnp.int32, x.shape, 1)
    low = lane < HEAD_DIM
    xs = jnp.dot(x, swap, preferred_element_type=F32).astype(BF16)
    zero = jnp.zeros_like(x)
    return (jnp.where(low, x, zero), jnp.where(low, zero, xs),
            jnp.where(low, xs, zero), jnp.where(low, zero, x))


def _dsa_prep_kernel(q_ref, iq_ref, k_ref, v_ref, ik_ref, iw_ref, cos_ref, sin_ref, qg_ref, kg_ref,
                     hsum_ref, rot_ref, swap_ref, qn_ref, iqn_ref, k4_ref, v4_ref, ik2_ref, iwn_ref):
    cos1, sin1 = cos_ref[...], sin_ref[...]
    reps = q_ref.shape[1] // LANES
    cos4, sin4 = jnp.tile(cos1, (1, reps)), jnp.tile(sin1, (1, reps))
    hsum, rot = hsum_ref[...], rot_ref[...]
    hsum1, rot1 = hsum[0:LANES, 0:LANES], rot[0:LANES, 0:LANES]
    scale = HEAD_DIM ** -0.5
    q = _head_rope(_head_rms(q_ref[...], qg_ref[...], hsum), cos4, sin4, rot) * scale
    qn_ref[...] = q.astype(BF16)
    iqn_ref[...] = (_head_rope(iq_ref[...], cos4, sin4, rot) * IDX_DIM ** -0.5).astype(BF16)
    k = _head_rope(_head_rms(k_ref[...], kg_ref[...], hsum1), cos1, sin1, rot1).astype(BF16)
    swap = swap_ref[...]
    for n, part in enumerate(_split_pair(k, swap)):
        k4_ref[n] = part
    for n, part in enumerate(_split_pair(v_ref[...].astype(BF16), swap)):
        v4_ref[n] = part
    ik = _head_rope(ik_ref[...], cos1, sin1, rot1).astype(BF16)
    ik_a, ik_b, _, _ = _split_pair(ik, swap)
    ik2_ref[0] = ik_a
    ik2_ref[1] = ik_b
    iwn_ref[...] = iw_ref[...] * IDX_HEADS ** -0.5


def _dot_nt(a, b):
    return lax.dot_general(a, b, (((1,), (1,)), ((), ())), preferred_element_type=F32)


def _dsa_attn_kernel(q_ref, iq_ref, iw_ref, k4_ref, v4_ref, ik2_ref, tri_ref, o_ref,
                     key_scr, m_scr, l_scr, acc_scr, *, topk):
    tb = q_ref.shape[0]
    kc = DSA_KC
    qi = pl.program_id(1)
    n_ch = qi // (kc // tb) + 1
    tq = qi * tb + lax.broadcasted_iota(jnp.int32, (tb, 1), 0)
    lane_k = lax.broadcasted_iota(jnp.int32, (tb, kc), 1)
    iw = iw_ref[...]

    def score_chunk(c, carry):
        ks = pl.multiple_of(c * kc, kc)
        ik_a = ik2_ref[0, pl.ds(ks, kc), :]
        ik_b = ik2_ref[1, pl.ds(ks, kc), :]
        sc = jnp.zeros((tb, kc), F32)
        for j in range(IDX_HEADS // 2):
            iqp = iq_ref[:, j * LANES:(j + 1) * LANES]
            sc = sc + iw[:, 2 * j:2 * j + 1] * jnp.maximum(_dot_nt(iqp, ik_a), 0.0)
            sc = sc + iw[:, 2 * j + 1:2 * j + 2] * jnp.maximum(_dot_nt(iqp, ik_b), 0.0)
        sc = jnp.where(sc == 0.0, 0.0, sc)
        bits = lax.bitcast_convert_type(sc, jnp.int32)
        key = bits ^ (jnp.right_shift(bits, 31) & 0x7FFFFFFF)
        key_scr[c] = jnp.where(ks + lane_k <= tq, key, INT_MIN)
        return carry

    lax.fori_loop(0, n_ch, score_chunk, 0)

    def count(pred_of_key):
        def body(c, acc):
            hit = jnp.where(pred_of_key(key_scr[c]), 1.0, 0.0)
            for s in range(kc // LANES):
                acc = acc + hit[:, s * LANES:(s + 1) * LANES]
            return acc
        acc = lax.fori_loop(0, n_ch, body, jnp.zeros((tb, LANES), F32))
        return jnp.sum(acc, axis=1, keepdims=True)

    def bit_step(i, t_u):
        cand = t_u | jnp.left_shift(jnp.int32(1), 31 - i)
        cand_s = cand ^ INT_MIN
        return jnp.where(count(lambda key: key >= cand_s) >= topk, cand, t_u)

    thr = lax.fori_loop(0, 32, bit_step, jnp.zeros((tb, 1), jnp.int32)) ^ INT_MIN
    need = topk - count(lambda key: key > thr)

    m_scr[...] = jnp.full(m_scr.shape, MASKED, F32)
    l_scr[...] = jnp.zeros(l_scr.shape, F32)
    acc_scr[...] = jnp.zeros(acc_scr.shape, F32)
    low = lax.broadcasted_iota(jnp.int32, (tb, LANES), 1) < HEAD_DIM

    def attn_chunk(c, ties_before):
        ks = pl.multiple_of(c * kc, kc)
        key = key_scr[c]
        eq = jnp.where(key == thr, 1.0, 0.0)
        rank = ties_before + jnp.dot(eq.astype(BF16), tri_ref[...], preferred_element_type=F32)
        tie_ok = jnp.where(rank <= need, eq, 0.0)
        sel = jnp.where(ks + lane_k <= tq, jnp.where(key > thr, 1.0, tie_ok), 0.0)
        chosen = sel > 0.0

        def head(s, h):
            sm = jnp.where(chosen, s, MASKED)
            m_old = m_scr[h]
            m_new = jnp.maximum(m_old, jnp.max(sm, axis=1, keepdims=True))
            p = jnp.exp(sm - m_new) * sel
            alpha = jnp.exp(m_old - m_new)
            l_scr[h] = alpha * l_scr[h] + jnp.sum(p, axis=1, keepdims=True)
            m_scr[h] = m_new
            return p.astype(BF16), alpha

        for j in range(A_HEADS // 2):
            g = (2 * j) // (A_HEADS // A_KV_HEADS)
            qp = q_ref[:, j * LANES:(j + 1) * LANES]
            k_a = k4_ref[2 * g, pl.ds(ks, kc), :]
            k_b = k4_ref[2 * g + 1, pl.ds(ks, kc), :]
            p_a, alpha_a = head(_dot_nt(qp, k_a), 2 * j)
            p_b, alpha_b = head(_dot_nt(qp, k_b), 2 * j + 1)
            pv = (jnp.dot(p_a, v4_ref[2 * g, pl.ds(ks, kc), :], preferred_element_type=F32)
                  + jnp.dot(p_b, v4_ref[2 * g + 1, pl.ds(ks, kc), :], preferred_element_type=F32))
            acc_scr[j] = jnp.where(low, alpha_a, alpha_b) * acc_scr[j] + pv
        return ties_before + jnp.sum(eq, axis=1, keepdims=True)

    lax.fori_loop(0, n_ch, attn_chunk, jnp.zeros((tb, 1), F32))
    for j in range(A_HEADS // 2):
        denom = jnp.maximum(jnp.where(low, l_scr[2 * j], l_scr[2 * j + 1]), 1e-30)
        o_ref[:, j * LANES:(j + 1) * LANES] = acc_scr[j] / denom


def _rope_tables(seq, width):
    half = HEAD_DIM // 2
    inv_freq = ROPE_THETA ** (-jnp.arange(half, dtype=F32) / half)
    ang = jnp.arange(seq, dtype=F32)[:, None] * inv_freq[None, :]
    reps = width // half
    return jnp.tile(jnp.cos(ang), (1, reps)), jnp.tile(jnp.sin(ang), (1, reps))


def _head_matrices(width):
    i = np.arange(width)
    same = (i[:, None] // HEAD_DIM) == (i[None, :] // HEAD_DIM)
    hsum = np.where(same, 1.0 / HEAD_DIM, 0.0)
    half = HEAD_DIM // 2
    src, dst = i[:, None], i[None, :]
    rot = np.where(same & (src == dst + half), -1.0, 0.0) + np.where(same & (src == dst - half), 1.0, 0.0)
    j = np.arange(LANES)
    swap = (j[:, None] == (j[None, :] + HEAD_DIM) % LANES).astype(np.float32)
    return jnp.asarray(hsum, BF16), jnp.asarray(rot, BF16), jnp.asarray(swap, BF16)


def _dsa_mixer(h, bsz, seq, q_gain, k_gain, tm=512):
    n = bsz * seq
    tb = ATTN_BLOCK
    assert seq % DSA_KC == 0 and seq % tm == 0
    topk = min(A_TOPK_MAX, seq // 4)
    cos, sin = _rope_tables(seq, LANES)
    hsum, rot, swap = _head_matrices(A_HEADS * HEAD_DIM)
    wide = A_HEADS * HEAD_DIM
    col = lambda name: EVEN_LAYOUT[name][0] // EVEN_LAYOUT[name][1]
    hspec = lambda name: pl.BlockSpec((tm, EVEN_LAYOUT[name][1]), lambda i, c=col(name): (i, c))
    full = lambda a: pl.BlockSpec(a.shape, lambda i: (0,) * a.ndim)
    pos_spec = pl.BlockSpec((tm, LANES), lambda i: (i % (seq // tm), 0))
    gq = jnp.tile(q_gain.astype(F32), A_HEADS).reshape(1, wide)
    gk = jnp.tile(k_gain.astype(F32), A_KV_HEADS).reshape(1, LANES)
    qn, iqn, k4, v4, ik2, iwn = pl.pallas_call(
        _dsa_prep_kernel,
        grid=(n // tm,),
        in_specs=[hspec('q'), hspec('iq'), hspec('k'), hspec('v'), hspec('ik'), hspec('iw'),
                  pos_spec, pos_spec, full(gq), full(gk), full(hsum), full(rot), full(swap)],
        out_specs=[pl.BlockSpec((tm, wide), lambda i: (i, 0)),
                   pl.BlockSpec((tm, wide), lambda i: (i, 0)),
                   pl.BlockSpec((4, tm, LANES), lambda i: (0, i, 0)),
                   pl.BlockSpec((4, tm, LANES), lambda i: (0, i, 0)),
                   pl.BlockSpec((2, tm, LANES), lambda i: (0, i, 0)),
                   pl.BlockSpec((tm, LANES), lambda i: (i, 0))],
        out_shape=[jax.ShapeDtypeStruct((n, wide), BF16), jax.ShapeDtypeStruct((n, wide), BF16),
                   jax.ShapeDtypeStruct((4, n, LANES), BF16), jax.ShapeDtypeStruct((4, n, LANES), BF16),
                   jax.ShapeDtypeStruct((2, n, LANES), BF16), jax.ShapeDtypeStruct((n, LANES), F32)],
        compiler_params=pltpu.CompilerParams(
            dimension_semantics=("arbitrary",), vmem_limit_bytes=VMEM_LIMIT),
        name="dsa_prep",
    )(h, h, h, h, h, h, cos, sin, gq, gk, hsum, rot, swap)

    r = np.arange(DSA_KC)
    tri = jnp.asarray(r[:, None] <= r[None, :], BF16)
    nq = seq // tb
    qspec = pl.BlockSpec((tb, wide), lambda b, i: (b * nq + i, 0))
    seq_spec = lambda lead: pl.BlockSpec((lead, seq, LANES), lambda b, i: (0, b, 0))
    return pl.pallas_call(
        functools.partial(_dsa_attn_kernel, topk=topk),
        grid=(bsz, nq),
        in_specs=[qspec, qspec, pl.BlockSpec((tb, LANES), lambda b, i: (b * nq + i, 0)),
                  seq_spec(4), seq_spec(4), seq_spec(2),
                  pl.BlockSpec(tri.shape, lambda b, i: (0, 0))],
        out_specs=qspec,
        out_shape=jax.ShapeDtypeStruct((n, wide), F32),
        scratch_shapes=[pltpu.VMEM((seq // DSA_KC, tb, DSA_KC), jnp.int32),
                        pltpu.VMEM((A_HEADS, tb, 1), F32), pltpu.VMEM((A_HEADS, tb, 1), F32),
                        pltpu.VMEM((A_HEADS // 2, tb, LANES), F32)],
        compiler_params=pltpu.CompilerParams(
            dimension_semantics=("arbitrary", "arbitrary"), vmem_limit_bytes=VMEM_LIMIT),
        name="dsa_attn",
    )(qn, iqn, iwn, k4, v4, ik2, tri)


def _dsa_attention(q, k, v, iq, ik, iw, q_gain, k_gain, pos):
    bsz, seq = q.shape[:2]
    topk = min(A_TOPK_MAX, seq // 4)
    rep = A_HEADS // A_KV_HEADS
    q = _rope(_rms_norm(q, q_gain), pos).astype(F32) * HEAD_DIM ** -0.5
    k = _rope(_rms_norm(k, k_gain), pos).astype(F32)
    v = v.astype(F32)
    iq = _rope(iq, pos).astype(F32) * IDX_DIM ** -0.5
    ik = _rope(ik[:, :, None, :], pos)[:, :, 0, :].astype(F32)
    iw = iw.astype(F32) * IDX_HEADS ** -0.5
    key_pos = jnp.arange(seq)
    gather = jax.vmap(lambda t, idx: t[idx])

    def block(args):
        qb, iqb, iwb, start = args
        tq = start + jnp.arange(ATTN_BLOCK)
        rel = jax.nn.relu(jnp.einsum('bqhd,bsd->bqhs', iqb, ik))
        score = jnp.einsum('bqh,bqhs->bqs', iwb, rel)
        causal = key_pos[None, :] <= tq[:, None]
        score = jnp.where(causal[None], score, -jnp.inf)
        _, sel = lax.top_k(score, topk)
        valid = sel <= tq[None, :, None]
        k_sel = gather(k, sel)
        v_sel = gather(v, sel)
        qg = qb.reshape(bsz, ATTN_BLOCK, A_KV_HEADS, rep, HEAD_DIM)
        s = jnp.einsum('bqgrd,bqkgd->bqgrk', qg, k_sel)
        p = _masked_softmax(s, valid[:, :, None, None, :])
        o = jnp.einsum('bqgrk,bqkgd->bqgrd', p, v_sel)
        return o.reshape(bsz, ATTN_BLOCK, A_HEADS * HEAD_DIM)

    starts = jnp.arange(seq // ATTN_BLOCK) * ATTN_BLOCK
    out = lax.map(block, (_to_blocks(q, ATTN_BLOCK), _to_blocks(iq, ATTN_BLOCK),
                          _to_blocks(iw, ATTN_BLOCK), starts))
    return out.swapaxes(0, 1).reshape(bsz, seq, A_HEADS * HEAD_DIM)


S5_STATES = B_GROUPS * B_STATE
S5_CHUNK = 64


def _s5_kernel(u_ref, bmat_ref, cmat_ref, a_re_ref, a_im_ref, d_ref, wg_ref, bg_ref, o_ref, h_scr):
    rows = u_ref.shape[0]
    nb = SUBLANES
    ns = S5_STATES

    @pl.when(pl.program_id(0) == 0)
    def _():
        h_scr[0:nb, :] = jnp.zeros((nb, 2 * ns), F32)

    u = u_ref[...]
    h_scr[nb:, :] = jnp.dot(u.astype(BF16), bmat_ref[...], preferred_element_type=F32)
    a_re = a_re_ref[...]
    a_im = a_im_ref[...]

    def step(t, carry):
        prev = pl.multiple_of(t * nb, nb)
        cur = pl.multiple_of(t * nb + nb, nb)
        p_re = h_scr[pl.ds(prev, nb), 0:ns]
        p_im = h_scr[pl.ds(prev, nb), ns:]
        h_scr[pl.ds(cur, nb), 0:ns] = a_re * p_re - a_im * p_im + h_scr[pl.ds(cur, nb), 0:ns]
        h_scr[pl.ds(cur, nb), ns:] = a_re * p_im + a_im * p_re + h_scr[pl.ds(cur, nb), ns:]
        return carry

    lax.fori_loop(0, rows // nb, step, 0)
    h_all = h_scr[nb:, :]
    h_scr[0:nb, :] = h_scr[rows:, :]
    y = jnp.dot(h_all.astype(BF16), cmat_ref[...], preferred_element_type=F32) + d_ref[...] * u
    y = jax.nn.gelu(y)
    gate = jnp.dot(y.astype(BF16), wg_ref[...], preferred_element_type=F32) + bg_ref[...]
    o_ref[...] = y * jax.nn.sigmoid(gate)


def _s5_operators(log_dt, a_re, a_im, b_re, b_im, c_re, c_im):
    dt = jnp.exp(log_dt.astype(F32))[:, None]
    lr, li = a_re.astype(F32), a_im.astype(F32)
    mag = jnp.exp(dt * lr)
    ab_re = mag * jnp.cos(dt * li)
    ab_im = mag * jnp.sin(dt * li)
    den = lr * lr + li * li
    f_re = ((ab_re - 1.0) * lr + ab_im * li) / den
    f_im = (ab_im * lr - (ab_re - 1.0) * li) / den
    br, bi = b_re.astype(F32), b_im.astype(F32)
    bb_re = f_re[..., None] * br - f_im[..., None] * bi
    bb_im = f_re[..., None] * bi + f_im[..., None] * br
    eye = jnp.eye(B_GROUPS, dtype=F32)
    blk_in = lambda m: jnp.einsum('gpc,gh->gchp', m, eye).reshape(B_WIDTH, S5_STATES)
    bmat = jnp.concatenate([blk_in(bb_re), blk_in(bb_im)], axis=1)
    blk_out = lambda m: jnp.einsum('gcp,gh->gphc', m.astype(F32), eye).reshape(S5_STATES, B_WIDTH)
    cmat = jnp.concatenate([blk_out(c_re), -blk_out(c_im)], axis=0)
    bcast = lambda m: jnp.broadcast_to(m.reshape(1, S5_STATES), (SUBLANES, S5_STATES))
    return bmat.astype(BF16), cmat.astype(BF16), bcast(ab_re), bcast(ab_im)


def _s5_mixer(u, log_dt, a_re, a_im, b_re, b_im, c_re, c_im, d_skip, w_glu, b_glu):
    bsz, seq, width = u.shape
    assert bsz == SUBLANES and width == B_WIDTH and seq % S5_CHUNK == 0
    bmat, cmat, ab_re, ab_im = _s5_operators(log_dt, a_re, a_im, b_re, b_im, c_re, c_im)
    u_tm = u.transpose(1, 0, 2).reshape(seq * bsz, width)
    rows = S5_CHUNK * bsz
    full = lambda shape: pl.BlockSpec(shape, lambda i: (0,) * len(shape))
    out = pl.pallas_call(
        _s5_kernel,
        grid=(seq // S5_CHUNK,),
        in_specs=[pl.BlockSpec((rows, width), lambda i: (i, 0)),
                  full(bmat.shape), full(cmat.shape), full(ab_re.shape), full(ab_im.shape),
                  full((1, width)), full((width, width)), full((1, width))],
        out_specs=pl.BlockSpec((rows, width), lambda i: (i, 0)),
        out_shape=jax.ShapeDtypeStruct((seq * bsz, width), F32),
        scratch_shapes=[pltpu.VMEM((rows + bsz, 2 * S5_STATES), F32)],
        compiler_params=pltpu.CompilerParams(
            dimension_semantics=("arbitrary",), vmem_limit_bytes=VMEM_LIMIT),
        name="s5_scan",
    )(u_tm, bmat, cmat, ab_re, ab_im, d_skip.astype(F32).reshape(1, width),
      w_glu.astype(BF16), b_glu.astype(F32).reshape(1, width))
    return out.reshape(seq, bsz, width).transpose(1, 0, 2)


def _s5_ssm(u, log_dt, a_re, a_im, b_re, b_im, c_re, c_im, d_skip, w_glu, b_glu):
    bsz, seq = u.shape[:2]
    uf = u.astype(F32).reshape(bsz, seq, B_GROUPS, B_GROUP)
    dt = jnp.exp(log_dt.astype(F32))[:, None]
    lr, li = a_re.astype(F32), a_im.astype(F32)
    mag = jnp.exp(dt * lr)
    ab_re = mag * jnp.cos(dt * li)
    ab_im = mag * jnp.sin(dt * li)
    den = lr * lr + li * li
    f_re = ((ab_re - 1.0) * lr + ab_im * li) / den
    f_im = (ab_im * lr - (ab_re - 1.0) * li) / den
    br, bi = b_re.astype(F32), b_im.astype(F32)
    bb_re = f_re[..., None] * br - f_im[..., None] * bi
    bb_im = f_re[..., None] * bi + f_im[..., None] * br
    x_re = jnp.einsum('gpc,bsgc->bsgp', bb_re, uf)
    x_im = jnp.einsum('gpc,bsgc->bsgp', bb_im, uf)
    a_re_t = jnp.broadcast_to(ab_re, x_re.shape)
    a_im_t = jnp.broadcast_to(ab_im, x_im.shape)

    def combine(e1, e2):
        a1r, a1i, b1r, b1i = e1
        a2r, a2i, b2r, b2i = e2
        return (a1r * a2r - a1i * a2i, a1r * a2i + a1i * a2r,
                a2r * b1r - a2i * b1i + b2r, a2r * b1i + a2i * b1r + b2i)

    _, _, h_re, h_im = lax.associative_scan(combine, (a_re_t, a_im_t, x_re, x_im), axis=1)
    y = (jnp.einsum('gcp,bsgp->bsgc', c_re.astype(F32), h_re)
         - jnp.einsum('gcp,bsgp->bsgc', c_im.astype(F32), h_im)
         + d_skip.astype(F32) * uf)
    y = jax.nn.gelu(y.reshape(bsz, seq, B_WIDTH))
    return y * jax.nn.sigmoid(y @ w_glu.astype(F32) + b_glu.astype(F32))


NSA_KC = 512
NSA_WIN_SPAN = WIN + ATTN_BLOCK


def _nsa_prep_kernel(q_ref, ks_ref, vs_ref, kw_ref, vw_ref, gl_ref, cos_ref, sin_ref, qg_ref, kg_ref,
                     hsum_ref, rot_ref, swap_ref, qn_ref, ks4_ref, vs4_ref, kw4_ref, vw4_ref, gate_ref):
    cos1, sin1 = cos_ref[...], sin_ref[...]
    reps = q_ref.shape[1] // LANES
    cos4, sin4 = jnp.tile(cos1, (1, reps)), jnp.tile(sin1, (1, reps))
    hsum, rot = hsum_ref[...], rot_ref[...]
    hsum1, rot1 = hsum[0:LANES, 0:LANES], rot[0:LANES, 0:LANES]
    swap = swap_ref[...]
    q = _head_rope(_head_rms(q_ref[...], qg_ref[...], hsum), cos4, sin4, rot) * HEAD_DIM ** -0.5
    qn_ref[...] = q.astype(BF16)
    for src, dst, is_key in ((ks_ref, ks4_ref, True), (vs_ref, vs4_ref, False),
                             (kw_ref, kw4_ref, True), (vw_ref, vw4_ref, False)):
        t = src[...]
        if is_key:
            t = _head_rope(_head_rms(t, kg_ref[...], hsum1), cos1, sin1, rot1)
        for n, part in enumerate(_split_pair(t.astype(BF16), swap)):
            dst[n] = part
    gate_ref[...] = jax.nn.sigmoid(gl_ref[...])


def _nsa_cmp_kernel(hk_ref, hv_ref, pos_ref, kw1_ref, kw2_ref, vw1_ref, vw2_ref, cos_ref, sin_ref,
                    kg_ref, hsum_ref, rot_ref, swap_ref, kc_ref, vc_ref):
    def mlp(h_ref, w1_ref, w2_ref):
        h = h_ref[0]
        first = jnp.dot((h + pos_ref[0:1, :]).astype(BF16), w1_ref[0], preferred_element_type=F32)
        second = jnp.dot((h + pos_ref[1:2, :]).astype(BF16), w1_ref[1], preferred_element_type=F32)
        z = jax.nn.gelu(first + pltpu.roll(second, second.shape[0] - 1, axis=0))
        return jnp.dot(z.astype(BF16), w2_ref[...], preferred_element_type=F32)

    swap = swap_ref[...]
    kc = _head_rope(_head_rms(mlp(hk_ref, kw1_ref, kw2_ref), kg_ref[...], hsum_ref[...]),
                    cos_ref[...], sin_ref[...], rot_ref[...])
    kc_a, kc_b, _, _ = _split_pair(kc.astype(BF16), swap)
    kc_ref[0, 0] = kc_a
    kc_ref[0, 1] = kc_b
    vc_a, vc_b, _, _ = _split_pair(mlp(hv_ref, vw1_ref, vw2_ref).astype(BF16), swap)
    vc_ref[0, 0] = vc_a
    vc_ref[0, 1] = vc_b


def _softmax_rows(s, mask):
    sm = jnp.where(mask > 0.0, s, MASKED)
    p = jnp.exp(sm - jnp.max(sm, axis=1, keepdims=True)) * mask
    return p / jnp.maximum(jnp.sum(p, axis=1, keepdims=True), 1e-30)


def _nsa_attn_kernel(q_ref, gate_ref, kc_ref, vc_ref, ks4_ref, vs4_ref, kw4_ref, vw4_ref,
                     m2s_ref, blk_ref, gx_ref, o_ref, m_scr, l_scr, acc_scr, oc_scr, ow_scr, *, top_n):
    tb = q_ref.shape[0]
    kc = NSA_KC
    qi = pl.program_id(1)
    start = qi * tb
    tq = start + lax.broadcasted_iota(jnp.int32, (tb, 1), 0)
    low = lax.broadcasted_iota(jnp.int32, (tb, LANES), 1) < HEAD_DIM
    heads_per_group = C_HEADS // C_KV_HEADS
    pairs = C_HEADS // 2

    n_cmp = kc_ref.shape[3]
    cmp_end = lax.broadcasted_iota(jnp.int32, (tb, n_cmp), 1) * CMP_STRIDE + (CMP_LEN - 1)
    cmask = jnp.where(cmp_end <= tq, 1.0, 0.0)
    p_sum = [jnp.zeros((tb, n_cmp), F32) for _ in range(C_KV_HEADS)]
    for j in range(pairs):
        g = (2 * j) // heads_per_group
        qp = q_ref[:, j * LANES:(j + 1) * LANES]
        p_a = _softmax_rows(_dot_nt(qp, kc_ref[0, g, 0]), cmask)
        p_b = _softmax_rows(_dot_nt(qp, kc_ref[0, g, 1]), cmask)
        p_sum[g] = p_sum[g] + p_a + p_b
        oc_scr[j] = (jnp.dot(p_a.astype(BF16), vc_ref[0, g, 0], preferred_element_type=F32)
                     + jnp.dot(p_b.astype(BF16), vc_ref[0, g, 1], preferred_element_type=F32))

    n_slc = m2s_ref.shape[0]
    slc_id = lax.broadcasted_iota(jnp.int32, (n_slc, tb), 0)
    cur = (start + lax.broadcasted_iota(jnp.int32, (n_slc, tb), 1)) // SLC_LEN
    forced = jnp.where(slc_id == cur, 1.0, jnp.where(slc_id == 0, 1.0, 0.0))
    chosen = []
    for g in range(C_KV_HEADS):
        hi = p_sum[g].astype(BF16)
        lo = (p_sum[g] - hi.astype(F32)).astype(BF16)
        imp = _dot_nt(m2s_ref[...], hi) + _dot_nt(m2s_ref[...], lo)
        imp = jnp.where(forced > 0.0, FORCE_SCORE, imp)
        imp = jnp.where(slc_id <= cur, imp, NEG_INF)
        picked = jnp.zeros((n_slc, tb), F32)
        for _ in range(top_n):
            m = jnp.max(imp, axis=0, keepdims=True)
            first = jnp.min(jnp.where(imp == m, slc_id, n_slc), axis=0, keepdims=True)
            hit = slc_id == first
            picked = jnp.where(hit, 1.0, picked)
            imp = jnp.where(hit, NEG_INF, imp)
        chosen.append(picked.T.astype(BF16))

    m_scr[...] = jnp.full(m_scr.shape, MASKED, F32)
    l_scr[...] = jnp.zeros(l_scr.shape, F32)
    acc_scr[...] = jnp.zeros(acc_scr.shape, F32)
    lane_k = lax.broadcasted_iota(jnp.int32, (tb, kc), 1)

    def slc_chunk(c, carry):
        ks = pl.multiple_of(c * kc, kc)
        causal = ks + lane_k <= tq
        sel = [jnp.where(causal, jnp.dot(chosen[g], blk_ref[c], preferred_element_type=F32), 0.0)
               for g in range(C_KV_HEADS)]

        def head(s, h, g):
            sm = jnp.where(sel[g] > 0.0, s, MASKED)
            m_old = m_scr[h]
            m_new = jnp.maximum(m_old, jnp.max(sm, axis=1, keepdims=True))
            p = jnp.exp(sm - m_new) * sel[g]
            alpha = jnp.exp(m_old - m_new)
            l_scr[h] = alpha * l_scr[h] + jnp.sum(p, axis=1, keepdims=True)
            m_scr[h] = m_new
            return p.astype(BF16), alpha

        for j in range(pairs):
            g = (2 * j) // heads_per_group
            qp = q_ref[:, j * LANES:(j + 1) * LANES]
            p_a, alpha_a = head(_dot_nt(qp, ks4_ref[2 * g, pl.ds(ks, kc), :]), 2 * j, g)
            p_b, alpha_b = head(_dot_nt(qp, ks4_ref[2 * g + 1, pl.ds(ks, kc), :]), 2 * j + 1, g)
            pv = (jnp.dot(p_a, vs4_ref[2 * g, pl.ds(ks, kc), :], preferred_element_type=F32)
                  + jnp.dot(p_b, vs4_ref[2 * g + 1, pl.ds(ks, kc), :], preferred_element_type=F32))
            acc_scr[j] = jnp.where(low, alpha_a, alpha_b) * acc_scr[j] + pv
        return carry

    lax.fori_loop(0, qi // (kc // tb) + 1, slc_chunk, 0)

    ws = pl.multiple_of(jnp.maximum(start - WIN, 0), tb)
    wpos = ws + lax.broadcasted_iota(jnp.int32, (tb, NSA_WIN_SPAN), 1)
    wmask = jnp.where(wpos <= tq, jnp.where(wpos > tq - WIN, 1.0, 0.0), 0.0)
    for j in range(pairs):
        g = (2 * j) // heads_per_group
        qp = q_ref[:, j * LANES:(j + 1) * LANES]
        p_a = _softmax_rows(_dot_nt(qp, kw4_ref[2 * g, pl.ds(ws, NSA_WIN_SPAN), :]), wmask)
        p_b = _softmax_rows(_dot_nt(qp, kw4_ref[2 * g + 1, pl.ds(ws, NSA_WIN_SPAN), :]), wmask)
        ow_scr[j] = (jnp.dot(p_a.astype(BF16), vw4_ref[2 * g, pl.ds(ws, NSA_WIN_SPAN), :],
                             preferred_element_type=F32)
                     + jnp.dot(p_b.astype(BF16), vw4_ref[2 * g + 1, pl.ds(ws, NSA_WIN_SPAN), :],
                               preferred_element_type=F32))

    gates = gate_ref[...]
    g_cmp, g_slc, g_win = (_exact_dot(gates, gx_ref[n]) for n in range(3))
    for j in range(pairs):
        cols = slice(j * LANES, (j + 1) * LANES)
        denom = jnp.maximum(jnp.where(low, l_scr[2 * j], l_scr[2 * j + 1]), 1e-30)
        o_ref[:, cols] = (g_cmp[:, cols] * oc_scr[j] + g_slc[:, cols] * (acc_scr[j] / denom)
                          + g_win[:, cols] * ow_scr[j])


def _nsa_mixer(q, kcmp, vcmp, kslc, vslc, kwin, vwin, gl, bsz, seq,
               q_gain, k_gain, cmp_pos, k_w1, k_w2, v_w1, v_w2, tm=512):
    n = bsz * seq
    tb = ATTN_BLOCK
    assert seq % NSA_KC == 0 and seq % tm == 0
    n_cmp = seq // CMP_STRIDE
    wide = C_HEADS * HEAD_DIM
    cos, sin = _rope_tables(seq, LANES)
    hsum, rot, swap = _head_matrices(wide)
    hsum1, rot1 = hsum[:LANES, :LANES], rot[:LANES, :LANES]
    full = lambda a: pl.BlockSpec(a.shape, lambda *i: (0,) * a.ndim)
    rows = lambda w: pl.BlockSpec((tm, w), lambda i: (i, 0))
    quad = pl.BlockSpec((4, tm, LANES), lambda i: (0, i, 0))
    pos_spec = pl.BlockSpec((tm, LANES), lambda i: (i % (seq // tm), 0))
    gq = jnp.tile(q_gain.astype(F32), C_HEADS).reshape(1, wide)
    gk = jnp.tile(k_gain.astype(F32), C_KV_HEADS).reshape(1, LANES)
    gl_pad = jnp.pad(gl, ((0, 0), (0, LANES - gl.shape[1])))
    params1 = pltpu.CompilerParams(dimension_semantics=("arbitrary",), vmem_limit_bytes=VMEM_LIMIT)
    quad_shape = jax.ShapeDtypeStruct((4, n, LANES), BF16)
    qn, ks4, vs4, kw4, vw4, gates = pl.pallas_call(
        _nsa_prep_kernel,
        grid=(n // tm,),
        in_specs=[rows(wide)] + [rows(LANES)] * 5 + [pos_spec, pos_spec, full(gq), full(gk),
                                                     full(hsum), full(rot), full(swap)],
        out_specs=[rows(wide), quad, quad, quad, quad, rows(LANES)],
        out_shape=[jax.ShapeDtypeStruct((n, wide), BF16), quad_shape, quad_shape, quad_shape,
                   quad_shape, jax.ShapeDtypeStruct((n, LANES), F32)],
        compiler_params=params1,
        name="nsa_prep",
    )(q, kslc, vslc, kwin, vwin, gl_pad, cos, sin, gq, gk, hsum, rot, swap)

    row_w = CMP_STRIDE * HEAD_DIM
    to_rows = lambda t: (t.reshape(bsz, seq, C_KV_HEADS, HEAD_DIM).transpose(0, 2, 1, 3)
                         .reshape(bsz * C_KV_HEADS, n_cmp, row_w))
    pos2 = cmp_pos.astype(F32).reshape(2, row_w)
    w1_halves = lambda w: w.astype(BF16).reshape(2, row_w, CMP_HIDDEN)
    w2_pad = lambda w: jnp.pad(w.astype(BF16), ((0, 0), (0, LANES - HEAD_DIM)))
    half = HEAD_DIM // 2
    cmp_end = jnp.arange(n_cmp, dtype=F32) * CMP_STRIDE + (CMP_LEN - 1)
    ang = cmp_end[:, None] * (ROPE_THETA ** (-jnp.arange(half, dtype=F32) / half))[None, :]
    cos_c, sin_c = jnp.tile(jnp.cos(ang), (1, LANES // half)), jnp.tile(jnp.sin(ang), (1, LANES // half))
    seq_rows = pl.BlockSpec((1, n_cmp, row_w), lambda i: (i, 0, 0))
    pair_out = pl.BlockSpec((1, 2, n_cmp, LANES), lambda i: (i, 0, 0, 0))
    pair_shape = jax.ShapeDtypeStruct((bsz * C_KV_HEADS, 2, n_cmp, LANES), BF16)
    cmp_in = (to_rows(kcmp), to_rows(vcmp), pos2, w1_halves(k_w1), w2_pad(k_w2), w1_halves(v_w1),
              w2_pad(v_w2), cos_c, sin_c, gk, hsum1, rot1, swap)
    kc2, vc2 = pl.pallas_call(
        _nsa_cmp_kernel,
        grid=(bsz * C_KV_HEADS,),
        in_specs=[seq_rows, seq_rows] + [full(a) for a in cmp_in[2:]],
        out_specs=[pair_out, pair_out],
        out_shape=[pair_shape, pair_shape],
        compiler_params=params1,
        name="nsa_compress",
    )(*cmp_in)
    kc2 = kc2.reshape(bsz, C_KV_HEADS, 2, n_cmp, LANES)
    vc2 = vc2.reshape(bsz, C_KV_HEADS, 2, n_cmp, LANES)

    n_slc = seq // SLC_LEN
    ratio, span = SLC_LEN // CMP_STRIDE, CMP_LEN // CMP_STRIDE
    off = np.arange(n_cmp)[None, :] - ratio * np.arange(n_slc)[:, None]
    m2s = np.maximum(np.minimum(ratio - 1, off) - np.maximum(0, off - span + 1) + 1, 0)
    m2s = np.pad(m2s, ((0, LANES - n_slc), (0, 0)))
    key_blk = np.arange(seq) // SLC_LEN
    blk = (np.arange(LANES)[:, None] == key_blk[None, :]).reshape(LANES, seq // NSA_KC, NSA_KC)
    blk = jnp.asarray(blk.transpose(1, 0, 2), BF16)
    col = np.arange(wide) // HEAD_DIM
    gx = np.stack([np.arange(LANES)[:, None] == (3 * col + br)[None, :] for br in range(3)])
    gx = jnp.asarray(gx, BF16)
    m2s = jnp.asarray(m2s, BF16)
    nq = seq // tb
    qspec = pl.BlockSpec((tb, wide), lambda b, i: (b * nq + i, 0))
    cmp_spec = pl.BlockSpec((1, C_KV_HEADS, 2, n_cmp, LANES), lambda b, i: (b, 0, 0, 0, 0))
    seq_spec = pl.BlockSpec((4, seq, LANES), lambda b, i: (0, b, 0))
    assert n_slc <= LANES
    return pl.pallas_call(
        functools.partial(_nsa_attn_kernel, top_n=min(SLC_TOPN, n_slc)),
        grid=(bsz, nq),
        in_specs=[qspec, pl.BlockSpec((tb, LANES), lambda b, i: (b * nq + i, 0)), cmp_spec, cmp_spec,
                  seq_spec, seq_spec, seq_spec, seq_spec, full(m2s), full(blk), full(gx)],
        out_specs=qspec,
        out_shape=jax.ShapeDtypeStruct((n, wide), F32),
        scratch_shapes=[pltpu.VMEM((C_HEADS, tb, 1), F32), pltpu.VMEM((C_HEADS, tb, 1), F32),
                        pltpu.VMEM((C_HEADS // 2, tb, LANES), F32),
                        pltpu.VMEM((C_HEADS // 2, tb, LANES), F32),
                        pltpu.VMEM((C_HEADS // 2, tb, LANES), F32)],
        compiler_params=pltpu.CompilerParams(
            dimension_semantics=("arbitrary", "arbitrary"), vmem_limit_bytes=PEER_VMEM_LIMIT),
        name="nsa_attn",
    )(qn, gates, kc2, vc2, ks4, vs4, kw4, vw4, m2s, blk, gx)


def _compress_blocks(t, cmp_pos, w1, w2):
    bsz, seq, groups, dh = t.shape
    n_cmp = (seq - CMP_LEN) // CMP_STRIDE + 1
    idx = jnp.arange(n_cmp)[:, None] * CMP_STRIDE + jnp.arange(CMP_LEN)[None, :]
    blocks = t.astype(F32)[:, idx] + cmp_pos.astype(F32)[:, None, :]
    blocks = blocks.transpose(0, 1, 3, 2, 4).reshape(bsz, n_cmp, groups, CMP_LEN * dh)
    return jax.nn.gelu(blocks @ w1.astype(F32)) @ w2.astype(F32)


def _nsa_attention(q, k_cmp, v_cmp, k_slc, v_slc, k_win, v_win, gate_logits,
                   q_gain, k_gain, cmp_pos, k_w1, k_w2, v_w1, v_w2, pos):
    bsz, seq = q.shape[:2]
    rep = C_HEADS // C_KV_HEADS
    n_cmp = (seq - CMP_LEN) // CMP_STRIDE + 1
    n_slc = seq // SLC_LEN
    top_n = min(SLC_TOPN, n_slc)
    q = _rope(_rms_norm(q, q_gain), pos).astype(F32) * HEAD_DIM ** -0.5
    cmp_end = jnp.arange(n_cmp) * CMP_STRIDE + (CMP_LEN - 1)
    kc = _rope(_rms_norm(_compress_blocks(k_cmp, cmp_pos, k_w1, k_w2), k_gain), cmp_end)
    vc = _compress_blocks(v_cmp, cmp_pos, v_w1, v_w2)
    ks = _rope(_rms_norm(k_slc, k_gain), pos).astype(F32)
    ks = ks.reshape(bsz, n_slc, SLC_LEN, C_KV_HEADS, HEAD_DIM).transpose(0, 3, 1, 2, 4)
    vs = v_slc.astype(F32).reshape(bsz, n_slc, SLC_LEN, C_KV_HEADS, HEAD_DIM).transpose(0, 3, 1, 2, 4)
    pad = ((0, 0), (WIN, 0), (0, 0), (0, 0))
    kw = jnp.pad(_rope(_rms_norm(k_win, k_gain), pos).astype(F32), pad)
    vw = jnp.pad(v_win.astype(F32), pad)
    gates = jax.nn.sigmoid(gate_logits.astype(F32))
    ratio = SLC_LEN // CMP_STRIDE
    span = CMP_LEN // CMP_STRIDE
    off = jnp.arange(n_cmp)[:, None] - ratio * jnp.arange(n_slc)[None, :]
    cmp_to_slc = jnp.maximum(
        jnp.minimum(ratio - 1, off) - jnp.maximum(0, off - span + 1) + 1, 0).astype(F32)
    slc_ids = jnp.arange(n_slc)
    tok_off = jnp.arange(SLC_LEN)
    win_off = jnp.arange(WIN + C_BLOCK)
    gather = jax.vmap(jax.vmap(lambda t, idx: t[idx]))

    def block(args):
        qb, gb, start = args
        tq = start + jnp.arange(C_BLOCK)
        qg = qb.reshape(bsz, C_BLOCK, C_KV_HEADS, rep, HEAD_DIM)
        gb = gb.reshape(bsz, C_BLOCK, C_KV_HEADS, rep, 3)
        s_c = jnp.einsum('bqgrd,bcgd->bqgrc', qg, kc)
        p_c = _masked_softmax(s_c, (cmp_end[None, :] <= tq[:, None])[None, :, None, None, :])
        o_c = jnp.einsum('bqgrc,bcgd->bqgrd', p_c, vc)
        imp = jnp.einsum('bqgrc,cj->bqgj', p_c, cmp_to_slc)
        cur = tq // SLC_LEN
        forced = (slc_ids[None, :] == cur[:, None]) | (slc_ids[None, :] == 0)
        admissible = slc_ids[None, :] <= cur[:, None]
        imp = jnp.where(forced[None, :, None, :], FORCE_SCORE, imp)
        imp = jnp.where(admissible[None, :, None, :], imp, -jnp.inf)
        _, sel = lax.top_k(imp, top_n)
        sel_g = sel.transpose(0, 2, 1, 3)
        k_sel = gather(ks, sel_g)
        v_sel = gather(vs, sel_g)
        s_s = jnp.einsum('bqgrd,bgqnld->bqgrnl', qg, k_sel)
        kpos = sel[..., None] * SLC_LEN + tok_off
        smask = (kpos <= tq[None, :, None, None, None]).reshape(bsz, C_BLOCK, C_KV_HEADS, 1, -1)
        p_s = _masked_softmax(s_s.reshape(bsz, C_BLOCK, C_KV_HEADS, rep, -1), smask)
        o_s = jnp.einsum('bqgrnl,bgqnld->bqgrd', p_s.reshape(s_s.shape), v_sel)
        kwb = lax.dynamic_slice_in_dim(kw, start, WIN + C_BLOCK, axis=1)
        vwb = lax.dynamic_slice_in_dim(vw, start, WIN + C_BLOCK, axis=1)
        wpos = start - WIN + win_off
        wmask = ((wpos[None, :] <= tq[:, None]) & (wpos[None, :] > tq[:, None] - WIN)
                 & (wpos[None, :] >= 0))
        s_w = jnp.einsum('bqgrd,bkgd->bqgrk', qg, kwb)
        p_w = _masked_softmax(s_w, wmask[None, :, None, None, :])
        o_w = jnp.einsum('bqgrk,bkgd->bqgrd', p_w, vwb)
        o = gb[..., 0:1] * o_c + gb[..., 1:2] * o_s + gb[..., 2:3] * o_w
        return o.reshape(bsz, C_BLOCK, C_HEADS * HEAD_DIM)

    starts = jnp.arange(seq // C_BLOCK) * C_BLOCK
    out = lax.map(block, (_to_blocks(q, C_BLOCK), _to_blocks(gates, C_BLOCK), starts))
    return out.swapaxes(0, 1).reshape(bsz, seq, C_HEADS * HEAD_DIM)


GLA_BLOCK = 256


def _gla_kernel(q_ref, k_ref, v_ref, gl_ref, r_ref, wg_ref, bg_ref, og_ref, tri_ref, same_ref,
                o_ref, st_scr):
    @pl.when(pl.program_id(1) == 0)
    def _():
        st_scr[...] = jnp.zeros(st_scr.shape, F32)

    tb = q_ref.shape[0]
    logit = jnp.dot(gl_ref[...].astype(BF16), wg_ref[...], preferred_element_type=F32) + bg_ref[...]
    log_a = jax.nn.log_sigmoid(logit) * (1.0 / GATE_TAU)
    a_hi = log_a.astype(BF16)
    a_lo = (log_a - a_hi.astype(F32)).astype(BF16)
    two_term = lambda m: (jnp.dot(m, a_hi, preferred_element_type=F32)
                          + jnp.dot(m, a_lo, preferred_element_type=F32))
    tri = tri_ref[...]
    bcum = two_term(tri)
    blast = two_term(same_ref[...])
    k = k_ref[...]
    q_dec = q_ref[...] * D_KDIM ** -0.5 * jnp.exp(bcum)
    k_inv = (k * jnp.exp(-bcum)).astype(BF16)
    k_rem = k * jnp.exp(blast - bcum)
    decay = jnp.exp(blast)
    causal = tri.astype(F32)
    low = lax.broadcasted_iota(jnp.int32, (tb, LANES), 1) < D_KDIM
    row_chunk = lax.broadcasted_iota(jnp.int32, (tb, LANES), 0) // GLA_CHUNK
    gain = og_ref[...]
    for j in range(D_HEADS // 2):
        cols = slice(j * LANES, (j + 1) * LANES)
        qd, ki, kr, dec = q_dec[:, cols], k_inv[:, cols], k_rem[:, cols], decay[:, cols]
        for h, qd_h in ((2 * j, jnp.where(low, qd, 0.0).astype(BF16)),
                        (2 * j + 1, jnp.where(low, 0.0, qd).astype(BF16))):
            vcols = slice(h * D_VDIM, (h + 1) * D_VDIM)
            v_h = v_ref[:, vcols]
            att = _dot_nt(qd_h, ki) * causal
            o_h = jnp.dot(att.astype(BF16), v_h.astype(BF16), preferred_element_type=F32)
            v_t = v_h.T.astype(BF16)
            inter = []
            for n in range(tb // GLA_CHUNK):
                rows = slice(n * GLA_CHUNK, (n + 1) * GLA_CHUNK)
                state = st_scr[h]
                inter.append(_dot_nt(qd_h[rows], state.astype(BF16)))
                upd = jnp.dot(v_t, jnp.where(row_chunk == n, kr, 0.0).astype(BF16),
                              preferred_element_type=F32)
                st_scr[h] = state * dec[n * GLA_CHUNK:n * GLA_CHUNK + 1, :] + upd
            o_h = o_h + jnp.concatenate(inter, axis=0)
            o_h = o_h * lax.rsqrt(jnp.mean(o_h * o_h, axis=-1, keepdims=True) + NORM_EPS) * gain
            o_ref[:, vcols] = o_h * jax.nn.silu(r_ref[:, vcols])


def _gla_mixer(gq, gk, gv, glow, gr, bsz, seq, w_gate, b_gate, o_gain):
    n = bsz * seq
    tb = GLA_BLOCK
    assert seq % tb == 0
    kw, vw = D_HEADS * D_KDIM, D_HEADS * D_VDIM
    glow_p = jnp.pad(glow, ((0, 0), (0, LANES - GATE_RANK)))
    wg = jnp.pad(w_gate.astype(BF16), ((0, LANES - GATE_RANK), (0, 0)))
    r = np.arange(tb)
    same = (r[:, None] // GLA_CHUNK) == (r[None, :] // GLA_CHUNK)
    tri = jnp.asarray(same & (r[None, :] <= r[:, None]), BF16)
    same = jnp.asarray(same, BF16)
    nb = seq // tb
    rows = lambda w: pl.BlockSpec((tb, w), lambda b, i: (b * nb + i, 0))
    full = lambda a: pl.BlockSpec(a.shape, lambda b, i: (0,) * a.ndim)
    bg = b_gate.astype(F32).reshape(1, kw)
    og = o_gain.astype(F32).reshape(1, D_VDIM)
    return pl.pallas_call(
        _gla_kernel,
        grid=(bsz, nb),
        in_specs=[rows(kw), rows(kw), rows(vw), rows(LANES), rows(vw),
                  full(wg), full(bg), full(og), full(tri), full(same)],
        out_specs=rows(vw),
        out_shape=jax.ShapeDtypeStruct((n, vw), F32),
        scratch_shapes=[pltpu.VMEM((D_HEADS, D_VDIM, LANES), F32)],
        compiler_params=pltpu.CompilerParams(
            dimension_semantics=("arbitrary", "arbitrary"), vmem_limit_bytes=VMEM_LIMIT),
        name="gla",
    )(gq, gk, gv, glow_p, gr, wg, bg, og, tri, same)


def _gla_attention(q, k, v, g_low, r, w_gate, b_gate, o_gain):
    bsz, seq = q.shape[:2]
    n_chunk = seq // GLA_CHUNK
    logit = g_low.astype(F32) @ w_gate.astype(F32) + b_gate.astype(F32)
    log_a = jax.nn.log_sigmoid(logit) / GATE_TAU

    def chunks(t):
        return t.astype(F32).reshape(bsz, n_chunk, GLA_CHUNK, D_HEADS, -1).transpose(0, 3, 1, 2, 4)

    qc = chunks(q) * D_KDIM ** -0.5
    kc, vc, gc = chunks(k), chunks(v), chunks(log_a)
    bcum = jnp.cumsum(gc, axis=3)
    blast = bcum[:, :, :, -1:, :]
    q_dec = qc * jnp.exp(bcum)
    k_inv = kc * jnp.exp(-bcum)
    causal = jnp.tril(jnp.ones((GLA_CHUNK, GLA_CHUNK), dtype=bool))
    att = jnp.where(causal, jnp.einsum('bhncd,bhnsd->bhncs', q_dec, k_inv), 0.0)
    o_intra = jnp.einsum('bhncs,bhnse->bhnce', att, vc)
    upd = jnp.einsum('bhncd,bhnce->bhnde', kc * jnp.exp(blast - bcum), vc)
    decay = jnp.exp(blast[:, :, :, 0, :])

    def step(state, inp):
        dec, u = inp
        return dec[..., None] * state + u, state

    init = jnp.zeros((bsz, D_HEADS, D_KDIM, D_VDIM), F32)
    _, prev = lax.scan(step, init, (jnp.moveaxis(decay, 2, 0), jnp.moveaxis(upd, 2, 0)))
    o_inter = jnp.einsum('bhncd,nbhde->bhnce', q_dec, prev)
    o = (o_intra + o_inter).transpose(0, 2, 3, 1, 4).reshape(bsz, seq, D_HEADS, D_VDIM)
    o = _rms_norm(o, o_gain).reshape(bsz, seq, D_HEADS * D_VDIM)
    return o * jax.nn.silu(r.astype(F32))


PEER_SLOTS = PEER_HEADS * PEER_TOPK
PEER_TT = 128
ROW_TILE = (SUBLANES, LANES)
NEG_INF = float('-inf')
ROUTE_ILP = 4


def _top16_rows(s, row_id, big):
    vals, ids = [], []
    for _ in range(PEER_TOPK):
        m = jnp.max(s, axis=0, keepdims=True)
        pick = jnp.min(jnp.where(s == m, row_id, big), axis=0, keepdims=True)
        s = jnp.where(row_id == pick, NEG_INF, s)
        vals.append(m)
        ids.append(pick)
    return vals, ids


def _peer_route_kernel(x_ref, g_ref, wq_ref, sk_ref, xn_ref, idx_ref, gate_ref, v_scr, i_scr):
    tt = x_ref.shape[0]
    x = x_ref[...]
    xn = x * lax.rsqrt(jnp.mean(x * x, axis=-1, keepdims=True) + NORM_EPS) * g_ref[...]
    xn_ref[...] = xn
    xb = xn.astype(BF16)
    key_id = lax.broadcasted_iota(jnp.int32, (PEER_KEYS, tt), 0)

    def sub_scores(step, carry):
        for u in range(ROUTE_ILP):
            hp = step * ROUTE_ILP + u
            q = jnp.dot(xb, wq_ref[hp], preferred_element_type=F32).astype(BF16)
            s = lax.dot_general(sk_ref[hp], q, (((1,), (1,)), ((), ())),
                                preferred_element_type=F32)
            vals, ids = _top16_rows(s, key_id, PEER_KEYS)
            v_scr[hp] = jnp.concatenate(vals, axis=0)
            i_scr[hp] = jnp.concatenate(ids, axis=0)
        return carry

    lax.fori_loop(0, 2 * PEER_HEADS // ROUTE_ILP, sub_scores, 0)

    n_blk = 10
    r = lax.broadcasted_iota(jnp.int32, (n_blk * SUBLANES, tt), 0)
    blk, b = r // SUBLANES, r % SUBLANES
    a_of_blk = jnp.where(blk < 2, 0, blk - 1)
    pos = jnp.where(blk == 9, (SUBLANES + b) * PEER_TOPK,
                    jnp.where(blk == 1, SUBLANES + b, a_of_blk * PEER_TOPK + b))
    b_max = jnp.where(blk < 3, 7, jnp.where(blk == 3, 4, jnp.where(blk == 4, 3, jnp.where(
        blk == 5, 2, jnp.where(blk < 9, 1, 7)))))
    live = b <= b_max

    def combine_head(h):
        v1, v2 = v_scr[2 * h], v_scr[2 * h + 1]
        i1, i2 = i_scr[2 * h], i_scr[2 * h + 1]
        lo_v, hi_v = v2[0:SUBLANES], v2[SUBLANES:]
        lo_i, hi_i = i2[0:SUBLANES], i2[SUBLANES:]
        cand = [v1[0:1] + lo_v, v1[0:1] + hi_v]
        cid = [i1[0:1] * PEER_KEYS + lo_i, i1[0:1] * PEER_KEYS + hi_i]
        for a in range(1, SUBLANES):
            cand.append(v1[a:a + 1] + lo_v)
            cid.append(i1[a:a + 1] * PEER_KEYS + lo_i)
        cand.append(v1[SUBLANES:] + v2[0:1])
        cid.append(i1[SUBLANES:] * PEER_KEYS + i2[0:1])
        cand = jnp.where(live, jnp.concatenate(cand, axis=0), NEG_INF)
        cid = jnp.concatenate(cid, axis=0)
        top_s, experts = [], []
        for _ in range(PEER_TOPK):
            m = jnp.max(cand, axis=0, keepdims=True)
            pick = jnp.min(jnp.where(cand == m, pos, PEER_TOPK * PEER_TOPK), axis=0, keepdims=True)
            hit = pos == pick
            experts.append(jnp.max(jnp.where(hit, cid, -1), axis=0, keepdims=True))
            cand = jnp.where(hit, NEG_INF, cand)
            top_s.append(m)
        top_s = jnp.concatenate(top_s, axis=0)
        ex = jnp.exp(top_s - top_s[0:1])
        gate_ref[0, h] = ex / jnp.sum(ex, axis=0, keepdims=True)
        idx_ref[0, h] = jnp.concatenate(experts, axis=0)

    def combine(step, carry):
        for u in range(ROUTE_ILP):
            combine_head(step * ROUTE_ILP + u)
        return carry

    lax.fori_loop(0, PEER_HEADS // ROUTE_ILP, combine, 0)


def _peer_route(x2d, g_ffn, wq, subkeys):
    n, dm = x2d.shape
    tt = PEER_TT
    half = PEER_QDIM // 2
    wq_t = wq.astype(BF16).reshape(dm, 2 * PEER_HEADS, half).transpose(1, 0, 2)
    sk = subkeys.astype(BF16).reshape(2 * PEER_HEADS, PEER_KEYS, half)
    tile4 = (1, PEER_HEADS, PEER_TOPK, tt)
    return pl.pallas_call(
        _peer_route_kernel,
        grid=(n // tt,),
        in_specs=[pl.BlockSpec((tt, dm), lambda i: (i, 0)),
                  pl.BlockSpec((1, dm), lambda i: (0, 0)),
                  pl.BlockSpec(wq_t.shape, lambda i: (0, 0, 0)),
                  pl.BlockSpec(sk.shape, lambda i: (0, 0, 0))],
        out_specs=[pl.BlockSpec((tt, dm), lambda i: (i, 0)),
                   pl.BlockSpec(tile4, lambda i: (i, 0, 0, 0)),
                   pl.BlockSpec(tile4, lambda i: (i, 0, 0, 0))],
        out_shape=[jax.ShapeDtypeStruct((n, dm), F32),
                   jax.ShapeDtypeStruct((n // tt,) + tile4[1:], jnp.int32),
                   jax.ShapeDtypeStruct((n // tt,) + tile4[1:], F32)],
        scratch_shapes=[pltpu.VMEM((2 * PEER_HEADS, PEER_TOPK, tt), F32),
                        pltpu.VMEM((2 * PEER_HEADS, PEER_TOPK, tt), jnp.int32)],
        compiler_params=pltpu.CompilerParams(
            dimension_semantics=("arbitrary",), vmem_limit_bytes=VMEM_LIMIT),
        name="peer_route",
    )(x2d, g_ffn.reshape(1, dm), wq_t, sk)


PEER_GROUP = 8


def _for_each_token(idx_hbm, idx_smem, sems, n_tok, per_token):
    tile = pl.program_id(0)
    n_grp = n_tok // PEER_GROUP

    def copy(g, buf):
        return pltpu.make_async_copy(idx_hbm.at[tile, g], idx_smem.at[buf], sems.at[buf])

    copy(0, 0).start()
    copy(1, 1).start()

    def step(p, carry):
        for buf in range(2):
            g = 2 * p + buf
            copy(g, buf).wait()
            for u in range(PEER_GROUP):
                per_token(g * PEER_GROUP + u,
                          lambda i, buf=buf, u=u: idx_smem[buf, u * PEER_SLOTS + i])

            @pl.when(g + 2 < n_grp)
            def _():
                copy(g + 2, buf).start()
        return carry

    lax.fori_loop(0, n_grp // 2, step, 0)


def _gathered_rows(tab_ref, idx_of):
    return jnp.concatenate([tab_ref[idx_of(i)] for i in range(PEER_SLOTS)], axis=0)


def _peer_up_kernel(idx_hbm, xn_ref, gate_ref, tab_ref, fold_ref, out_ref, idx_smem, rbuf, sems):
    tt = xn_ref.shape[0]
    eye = (lax.broadcasted_iota(jnp.int32, (LANES, LANES), 0)
           == lax.broadcasted_iota(jnp.int32, (LANES, LANES), 1))

    def per_token(t, idx_of):
        rows = _gathered_rows(tab_ref, idx_of)
        x16 = jnp.tile(xn_ref[t], (LANES // SUBLANES, 1)).astype(BF16)
        y = lax.dot_general(rows, x16, (((1,), (1,)), ((), ())),
                            preferred_element_type=F32)
        y = y.reshape(PEER_SLOTS * SUBLANES // LANES, LANES, LANES)
        rbuf[t] = jnp.sum(jnp.where(eye[None], y, 0.0), axis=1)

    _for_each_token(idx_hbm, idx_smem, sems, tt, per_token)
    r = rbuf[...].reshape(tt * SUBLANES, LANES)
    r_hi = r.astype(BF16)
    r_lo = (r - r_hi.astype(F32)).astype(BF16)
    h = (jnp.dot(r_hi, fold_ref[...], preferred_element_type=F32)
         + jnp.dot(r_lo, fold_ref[...], preferred_element_type=F32))
    out_ref[...] = gate_ref[...] * jax.nn.gelu(h)


def _peer_down_kernel(idx_hbm, gh_ref, res_ref, tab_ref, spread_ref, out_ref, idx_smem, ghx, sems):
    tt = gh_ref.shape[0]
    ghx[...] = jnp.dot(gh_ref[...].astype(BF16), spread_ref[...], preferred_element_type=F32)
    width = PEER_SLOTS * SUBLANES
    diag = (lax.broadcasted_iota(jnp.int32, (SUBLANES, width), 0)
            == lax.broadcasted_iota(jnp.int32, (SUBLANES, width), 1) % SUBLANES)

    def per_token(t, idx_of):
        rows = _gathered_rows(tab_ref, idx_of)
        row = jnp.broadcast_to(ghx[pl.ds(t, 1), :], (SUBLANES, width))
        coef = jnp.where(diag, row, 0.0).astype(BF16)
        out_ref[t] = res_ref[t] + jnp.dot(coef, rows, preferred_element_type=F32)

    _for_each_token(idx_hbm, idx_smem, sems, tt, per_token)


def _peer_table(tab):
    e, dm = tab.shape
    assert dm == SUBLANES * LANES
    return tab.astype(BF16).reshape(e, SUBLANES, LANES)


def _peer_ffn_residual(x2d, g_ffn, wq, subkeys, u_tab, v_tab):
    n, dm = x2d.shape
    tt = PEER_TT
    n_tiles = n // tt
    assert tt % (2 * PEER_GROUP) == 0
    xn, idx, gate = _peer_route(x2d, g_ffn, wq, subkeys)
    idx_tiles = idx.transpose(0, 3, 1, 2).reshape(n_tiles, tt // PEER_GROUP, PEER_GROUP * PEER_SLOTS)
    gate_rows = gate.transpose(0, 3, 1, 2).reshape(n * PEER_HEADS, PEER_TOPK)
    width = PEER_SLOTS * SUBLANES
    lane = np.arange(LANES)
    fold = jnp.asarray(lane[:, None] // SUBLANES == np.arange(PEER_TOPK)[None, :], BF16)
    spread = jnp.asarray(np.arange(PEER_SLOTS)[:, None] == np.arange(width)[None, :] // SUBLANES, BF16)
    params = pltpu.CompilerParams(dimension_semantics=("arbitrary",),
                                  vmem_limit_bytes=PEER_VMEM_LIMIT)
    table_spec = pl.BlockSpec(memory_space=pltpu.VMEM)
    idx_scratch = pltpu.SMEM((2, PEER_GROUP * PEER_SLOTS), jnp.int32)
    sems = pltpu.SemaphoreType.DMA((2,))
    row3 = (tt,) + ROW_TILE

    gh = pl.pallas_call(
        _peer_up_kernel,
        grid=(n_tiles,),
        in_specs=[pl.BlockSpec(memory_space=pl.ANY),
                  pl.BlockSpec(row3, lambda i: (i, 0, 0)),
                  pl.BlockSpec((tt * PEER_HEADS, PEER_TOPK), lambda i: (i, 0)),
                  table_spec,
                  pl.BlockSpec((LANES, PEER_TOPK), lambda i: (0, 0))],
        out_specs=pl.BlockSpec((tt * PEER_HEADS, PEER_TOPK), lambda i: (i, 0)),
        out_shape=jax.ShapeDtypeStruct((n * PEER_HEADS, PEER_TOPK), F32),
        scratch_shapes=[idx_scratch, pltpu.VMEM(row3, F32), sems],
        compiler_params=params,
        name="peer_up",
    )(idx_tiles,xn.reshape((n,) + ROW_TILE), gate_rows, _peer_table(u_tab), fold)

    out = pl.pallas_call(
        _peer_down_kernel,
        grid=(n_tiles,),
        in_specs=[pl.BlockSpec(memory_space=pl.ANY),
                  pl.BlockSpec((tt, PEER_SLOTS), lambda i: (i, 0)),
                  pl.BlockSpec(row3, lambda i: (i, 0, 0)),
                  table_spec,
                  pl.BlockSpec((PEER_SLOTS, width), lambda i: (0, 0))],
        out_specs=pl.BlockSpec(row3, lambda i: (i, 0, 0)),
        out_shape=jax.ShapeDtypeStruct((n,) + ROW_TILE, F32),
        scratch_shapes=[idx_scratch, pltpu.VMEM((tt, width), F32), sems],
        compiler_params=params,
        name="peer_down",
    )(idx_tiles,gh.reshape(n, PEER_SLOTS), x2d.reshape((n,) + ROW_TILE), _peer_table(v_tab), spread)
    return out.reshape(n, dm)


def _even_layer(x, pos, mix_norm, w_in, q_gain, k_gain, log_dt, a_re, a_im, b_re, b_im,
                c_re, c_im, d_skip, w_glu, b_glu, w_out):
    bsz, seq, dm = x.shape
    x2d = x.reshape(bsz * seq, dm)
    pieces = dict(zip(('q', 'k', 'v', 'iq', 'ik', 'iw', 'u'), _split_cols(w_in, EVEN_COLS)))
    w_lay = jnp.concatenate(
        [jnp.pad(pieces[name], ((0, 0), (0, EVEN_LAYOUT[name][1] - pieces[name].shape[1])))
         for name in sorted(EVEN_LAYOUT, key=lambda s: EVEN_LAYOUT[s][0])], axis=1)
    h = _norm_proj(x2d, mix_norm, w_lay)
    o_a = _dsa_mixer(h, bsz, seq, q_gain, k_gain)
    u0, uw = EVEN_LAYOUT['u']
    u = h[:, u0:u0 + uw].reshape(bsz, seq, uw)
    o_b = _s5_mixer(u, log_dt, a_re, a_im, b_re, b_im, c_re, c_im, d_skip, w_glu, b_glu)
    mix = jnp.concatenate([o_a, o_b.reshape(bsz * seq, uw)], axis=-1)
    return _proj_residual(mix, w_out, x2d).reshape(bsz, seq, dm)


def _odd_layer(x, pos, mix_norm, w_in, q_gain, k_gain, cmp_pos, k_w1, k_w2, v_w1, v_w2,
               w_gate, b_gate, o_gain, w_out):
    bsz, seq, dm = x.shape
    x2d = x.reshape(bsz * seq, dm)
    h = _norm_proj(x2d, mix_norm, w_in)
    (q, kc, vc, ks, vs, kw, vw, gl, gq, gk, gv, glow, gr) = _split_cols(h, ODD_COLS)
    o_c = _nsa_mixer(q, kc, vc, ks, vs, kw, vw, gl, bsz, seq, q_gain, k_gain,
                     cmp_pos, k_w1, k_w2, v_w1, v_w2)
    o_d = _gla_mixer(gq, gk, gv, glow, gr, bsz, seq, w_gate, b_gate, o_gain)
    mix = jnp.concatenate([o_c, o_d], axis=-1)
    return _proj_residual(mix, w_out, x2d).reshape(bsz, seq, dm)


def kernel(x, l0_mix_norm, l0_w_in, l0_a_q_gain, l0_a_k_gain, l0_s5_log_dt, l0_s5_a_re, l0_s5_a_im, l0_s5_b_re, l0_s5_b_im, l0_s5_c_re, l0_s5_c_im, l0_s5_d, l0_s5_w_glu, l0_s5_b_glu, l0_w_out, l0_ffn_norm, l0_peer_wq, l0_peer_subkeys, l0_peer_u, l0_peer_v, l1_mix_norm, l1_w_in, l1_c_q_gain, l1_c_k_gain, l1_nsa_cmp_pos, l1_nsa_k_w1, l1_nsa_k_w2, l1_nsa_v_w1, l1_nsa_v_w2, l1_gla_w_gate, l1_gla_b_gate, l1_gla_o_gain, l1_w_out, l1_ffn_norm, l1_peer_wq, l1_peer_subkeys, l1_peer_u, l1_peer_v):
    seq = x.shape[1]
    pos = jnp.arange(seq, dtype=jnp.int32)
    x = _even_layer(x, pos, l0_mix_norm, l0_w_in, l0_a_q_gain, l0_a_k_gain, l0_s5_log_dt,
                    l0_s5_a_re, l0_s5_a_im, l0_s5_b_re, l0_s5_b_im, l0_s5_c_re, l0_s5_c_im,
                    l0_s5_d, l0_s5_w_glu, l0_s5_b_glu, l0_w_out)
    shape = x.shape
    flat = lambda t: t.reshape(-1, shape[-1])
    x = _peer_ffn_residual(flat(x), l0_ffn_norm, l0_peer_wq, l0_peer_subkeys,
                           l0_peer_u, l0_peer_v).reshape(shape)
    x = _odd_layer(x, pos, l1_mix_norm, l1_w_in, l1_c_q_gain, l1_c_k_gain, l1_nsa_cmp_pos,
                   l1_nsa_k_w1, l1_nsa_k_w2, l1_nsa_v_w1, l1_nsa_v_w2, l1_gla_w_gate,
                   l1_gla_b_gate, l1_gla_o_gain, l1_w_out)
    return _peer_ffn_residual(flat(x), l1_ffn_norm, l1_peer_wq, l1_peer_subkeys,
                              l1_peer_u, l1_peer_v).reshape(shape)
```

```python
import functools
import math

import jax
import jax.numpy as jnp
import numpy as np
from jax import lax
from jax.experimental import pallas as pl
from jax.experimental.pallas import tpu as pltpu

F32 = jnp.float32
BF16 = jnp.bfloat16
NORM_EPS = 1e-6
ROPE_THETA = 10000.0
HEAD_DIM = 64
ATTN_BLOCK = 128
A_HEADS = 8
A_KV_HEADS = 2
IDX_HEADS = 8
IDX_DIM = 64
A_TOPK_MAX = 256
B_WIDTH = 512
B_GROUP = 16
B_GROUPS = B_WIDTH // B_GROUP
B_STATE = 64
C_HEADS = 8
C_KV_HEADS = 2
CMP_LEN = 32
CMP_STRIDE = 16
CMP_HIDDEN = 256
SLC_LEN = 64
SLC_TOPN = 16
WIN = 512
C_BLOCK = 64
FORCE_SCORE = 1e9
D_HEADS = 4
D_KDIM = 64
D_VDIM = 128
GATE_RANK = 16
GATE_TAU = 16.0
GLA_CHUNK = 32
PEER_HEADS = 8
PEER_KEYS = 128
PEER_QDIM = 256
PEER_TOPK = 16
PEER_BLOCK = 128

EVEN_COLS = (A_HEADS * HEAD_DIM, A_KV_HEADS * HEAD_DIM, A_KV_HEADS * HEAD_DIM,
             IDX_HEADS * IDX_DIM, IDX_DIM, IDX_HEADS, B_WIDTH)
ODD_COLS = (C_HEADS * HEAD_DIM,) + (C_KV_HEADS * HEAD_DIM,) * 6 + (
    C_HEADS * 3, D_HEADS * D_KDIM, D_HEADS * D_KDIM, D_HEADS * D_VDIM, GATE_RANK, D_HEADS * D_VDIM)

LANES = 128
SUBLANES = 8
MXU_DIM = 256
VMEM_LIMIT = 48 * 1024 * 1024
PEER_VMEM_LIMIT = 56 * 1024 * 1024


def _norm_proj_kernel(x_ref, g_ref, w_ref, o_ref):
    x = x_ref[...]
    y = x * lax.rsqrt(jnp.mean(x * x, axis=-1, keepdims=True) + NORM_EPS) * g_ref[...]
    o_ref[...] = jnp.dot(y.astype(BF16), w_ref[...].astype(BF16), preferred_element_type=F32)


def _norm_proj(x2d, gain, w, tm=512, tn=512):
    m, k = x2d.shape
    n = w.shape[1]
    n_pad = -(-n // tn) * tn
    w_p = jnp.pad(w, ((0, 0), (0, n_pad - n)))
    out = pl.pallas_call(
        _norm_proj_kernel,
        grid=(m // tm, n_pad // tn),
        in_specs=[pl.BlockSpec((tm, k), lambda i, j: (i, 0)),
                  pl.BlockSpec((1, k), lambda i, j: (0, 0)),
                  pl.BlockSpec((k, tn), lambda i, j: (0, j))],
        out_specs=pl.BlockSpec((tm, tn), lambda i, j: (i, j)),
        out_shape=jax.ShapeDtypeStruct((m, n_pad), F32),
        compiler_params=pltpu.CompilerParams(
            dimension_semantics=("arbitrary", "arbitrary"), vmem_limit_bytes=VMEM_LIMIT),
    )(x2d, gain.reshape(1, k), w_p)
    return out[:, :n]


def _proj_res_kernel(a_ref, w_ref, r_ref, o_ref):
    o_ref[...] = r_ref[...] + jnp.dot(a_ref[...].astype(BF16), w_ref[...].astype(BF16),
                                      preferred_element_type=F32)


def _proj_residual(a2d, w, res2d, tm=512):
    m, k = a2d.shape
    n = w.shape[1]
    return pl.pallas_call(
        _proj_res_kernel,
        grid=(m // tm,),
        in_specs=[pl.BlockSpec((tm, k), lambda i: (i, 0)),
                  pl.BlockSpec((k, n), lambda i: (0, 0)),
                  pl.BlockSpec((tm, n), lambda i: (i, 0))],
        out_specs=pl.BlockSpec((tm, n), lambda i: (i, 0)),
        out_shape=jax.ShapeDtypeStruct((m, n), F32),
        compiler_params=pltpu.CompilerParams(
            dimension_semantics=("arbitrary",), vmem_limit_bytes=VMEM_LIMIT),
    )(a2d, w, res2d)


def _rms_norm(x, gain):
    xf = x.astype(F32)
    y = xf * lax.rsqrt(jnp.mean(xf * xf, axis=-1, keepdims=True) + NORM_EPS)
    return (y * gain.astype(F32)).astype(x.dtype)


def _rope(x, pos):
    half = x.shape[-1] // 2
    inv_freq = ROPE_THETA ** (-jnp.arange(half, dtype=F32) / half)
    ang = pos.astype(F32)[:, None] * inv_freq[None, :]
    cos = jnp.cos(ang)[:, None, :]
    sin = jnp.sin(ang)[:, None, :]
    xf = x.astype(F32)
    x1, x2 = xf[..., :half], xf[..., half:]
    return jnp.concatenate([x1 * cos - x2 * sin, x1 * sin + x2 * cos], axis=-1).astype(x.dtype)


def _masked_softmax(s, mask):
    s = jnp.where(mask, s.astype(F32), -1e30)
    m = jnp.max(s, axis=-1, keepdims=True)
    p = jnp.exp(s - m) * mask
    return p / jnp.maximum(jnp.sum(p, axis=-1, keepdims=True), 1e-30)


def _split_cols(h, sizes):
    return jnp.split(h, np.cumsum(sizes)[:-1].tolist(), axis=-1)


def _to_blocks(t, blk):
    return t.reshape(t.shape[0], t.shape[1] // blk, blk, *t.shape[2:]).swapaxes(0, 1)


INT_MIN = -2 ** 31
MASKED = -1e30
MAX_FLOOR = -1e29
DSA_KC = 1024
EVEN_LAYOUT = dict(q=(0, 512), iq=(512, 512), u=(1024, 512), k=(1536, 128), v=(1664, 128),
                   ik=(1792, 128), iw=(1920, 128))
EVEN_WIDTH = 2048


def _exact_dot(x, m):
    hi = x.astype(BF16)
    lo = (x - hi.astype(F32)).astype(BF16)
    return (jnp.dot(hi, m, preferred_element_type=F32) + jnp.dot(lo, m, preferred_element_type=F32))


def _head_rms(x, gain, hsum):
    return x * lax.rsqrt(_exact_dot(x * x, hsum) + NORM_EPS) * gain


def _head_rope(x, cos, sin, rot):
    return x * cos + _exact_dot(x, rot) * sin


def _split_pair(x, swap):
    lane = lax.broadcasted_iota(jnp.int32, x.shape, 1)
    low = lane < HEAD_DIM
    xs = jnp.dot(x, swap, preferred_element_type=F32).astype(BF16)
    zero = jnp.zeros_like(x)
    return (jnp.where(low, x, zero), jnp.where(low, zero, xs),
            jnp.where(low, xs, zero), jnp.where(low, zero, x))


def _dsa_prep_kernel(q_ref, iq_ref, k_ref, v_ref, ik_ref, iw_ref, cos_ref, sin_ref, qg_ref, kg_ref,
                     hsum_ref, rot_ref, swap_ref, qn_ref, iqn_ref, k4_ref, v4_ref, ik2_ref, iwn_ref):
    cos1, sin1 = cos_ref[...], sin_ref[...]
    reps = q_ref.shape[1] // LANES
    cos4, sin4 = jnp.tile(cos1, (1, reps)), jnp.tile(sin1, (1, reps))
    hsum, rot = hsum_ref[...], rot_ref[...]
    hsum1, rot1 = hsum[0:LANES, 0:LANES], rot[0:LANES, 0:LANES]
    scale = HEAD_DIM ** -0.5
    q = _head_rope(_head_rms(q_ref[...], qg_ref[...], hsum), cos4, sin4, rot) * scale
    qn_ref[...] = q.astype(BF16)
    iqn_ref[...] = (_head_rope(iq_ref[...], cos4, sin4, rot) * IDX_DIM ** -0.5).astype(BF16)
    k = _head_rope(_head_rms(k_ref[...], kg_ref[...], hsum1), cos1, sin1, rot1).astype(BF16)
    swap = swap_ref[...]
    for n, part in enumerate(_split_pair(k, swap)):
        k4_ref[n] = part
    for n, part in enumerate(_split_pair(v_ref[...].astype(BF16), swap)):
        v4_ref[n] = part
    ik = _head_rope(ik_ref[...], cos1, sin1, rot1).astype(BF16)
    ik_a, ik_b, _, _ = _split_pair(ik, swap)
    ik2_ref[0] = ik_a
    ik2_ref[1] = ik_b
    iwn_ref[...] = iw_ref[...] * IDX_HEADS ** -0.5


def _dot_nt(a, b):
    return lax.dot_general(a, b, (((1,), (1,)), ((), ())), preferred_element_type=F32)


def _dsa_attn_kernel(q_ref, iq_ref, iw_ref, k4_ref, v4_ref, ik2_ref, tri_ref, o_ref,
                     key_scr, m_scr, l_scr, acc_scr, *, topk):
    tb = q_ref.shape[0]
    kc = DSA_KC
    qi = pl.program_id(1)
    n_ch = qi // (kc // tb) + 1
    tq = qi * tb + lax.broadcasted_iota(jnp.int32, (tb, 1), 0)
    lane_k = lax.broadcasted_iota(jnp.int32, (tb, kc), 1)
    iw = iw_ref[...]

    def score_chunk(c, carry):
        ks = pl.multiple_of(c * kc, kc)
        ik_a = ik2_ref[0, pl.ds(ks, kc), :]
        ik_b = ik2_ref[1, pl.ds(ks, kc), :]
        sc = jnp.zeros((tb, kc), F32)
        for j in range(IDX_HEADS // 2):
            iqp = iq_ref[:, j * LANES:(j + 1) * LANES]
            sc = sc + iw[:, 2 * j:2 * j + 1] * jnp.maximum(_dot_nt(iqp, ik_a), 0.0)
            sc = sc + iw[:, 2 * j + 1:2 * j + 2] * jnp.maximum(_dot_nt(iqp, ik_b), 0.0)
        sc = jnp.where(sc == 0.0, 0.0, sc)
        bits = lax.bitcast_convert_type(sc, jnp.int32)
        key = bits ^ (jnp.right_shift(bits, 31) & 0x7FFFFFFF)
        key_scr[c] = jnp.where(ks + lane_k <= tq, key, INT_MIN)
        return carry

    lax.fori_loop(0, n_ch, score_chunk, 0)

    def count(pred_of_key):
        def body(c, acc):
            hit = jnp.where(pred_of_key(key_scr[c]), 1.0, 0.0)
            for s in range(kc // LANES):
                acc = acc + hit[:, s * LANES:(s + 1) * LANES]
            return acc
        acc = lax.fori_loop(0, n_ch, body, jnp.zeros((tb, LANES), F32))
        return jnp.sum(acc, axis=1, keepdims=True)

    def bit_step(i, t_u):
        cand = t_u | jnp.left_shift(jnp.int32(1), 31 - i)
        cand_s = cand ^ INT_MIN
        return jnp.where(count(lambda key: key >= cand_s) >= topk, cand, t_u)

    thr = lax.fori_loop(0, 32, bit_step, jnp.zeros((tb, 1), jnp.int32)) ^ INT_MIN
    need = topk - count(lambda key: key > thr)

    m_scr[...] = jnp.full(m_scr.shape, MAX_FLOOR, F32)
    l_scr[...] = jnp.zeros(l_scr.shape, F32)
    acc_scr[...] = jnp.zeros(acc_scr.shape, F32)
    low = lax.broadcasted_iota(jnp.int32, (tb, LANES), 1) < HEAD_DIM

    def attn_chunk(c, ties_before):
        ks = pl.multiple_of(c * kc, kc)
        key = key_scr[c]
        eq = jnp.where(key == thr, 1.0, 0.0)
        rank = ties_before + jnp.dot(eq.astype(BF16), tri_ref[...], preferred_element_type=F32)
        tie_ok = jnp.where(rank <= need, eq, 0.0)
        sel = jnp.where(ks + lane_k <= tq, jnp.where(key > thr, 1.0, tie_ok), 0.0)
        chosen = sel > 0.0

        def head(s, h):
            sm = jnp.where(chosen, s, MASKED)
            m_old = m_scr[h]
            m_new = jnp.maximum(m_old, jnp.max(sm, axis=1, keepdims=True))
            p = jnp.exp(sm - m_new)
            alpha = jnp.exp(m_old - m_new)
            l_scr[h] = alpha * l_scr[h] + jnp.sum(p, axis=1, keepdims=True)
            m_scr[h] = m_new
            return p.astype(BF16), alpha

        for j in range(A_HEADS // 2):
            g = (2 * j) // (A_HEADS // A_KV_HEADS)
            qp = q_ref[:, j * LANES:(j + 1) * LANES]
            k_a = k4_ref[2 * g, pl.ds(ks, kc), :]
            k_b = k4_ref[2 * g + 1, pl.ds(ks, kc), :]
            p_a, alpha_a = head(_dot_nt(qp, k_a), 2 * j)
            p_b, alpha_b = head(_dot_nt(qp, k_b), 2 * j + 1)
            pv = (jnp.dot(p_a, v4_ref[2 * g, pl.ds(ks, kc), :], preferred_element_type=F32)
                  + jnp.dot(p_b, v4_ref[2 * g + 1, pl.ds(ks, kc), :], preferred_element_type=F32))
            acc_scr[j] = jnp.where(low, alpha_a, alpha_b) * acc_scr[j] + pv
        return ties_before + jnp.sum(eq, axis=1, keepdims=True)

    lax.fori_loop(0, n_ch, attn_chunk, jnp.zeros((tb, 1), F32))
    for j in range(A_HEADS // 2):
        denom = jnp.maximum(jnp.where(low, l_scr[2 * j], l_scr[2 * j + 1]), 1e-30)
        o_ref[:, j * LANES:(j + 1) * LANES] = acc_scr[j] / denom


def _rope_tables(seq, width):
    half = HEAD_DIM // 2
    inv_freq = ROPE_THETA ** (-jnp.arange(half, dtype=F32) / half)
    ang = jnp.arange(seq, dtype=F32)[:, None] * inv_freq[None, :]
    reps = width // half
    return jnp.tile(jnp.cos(ang), (1, reps)), jnp.tile(jnp.sin(ang), (1, reps))


def _head_matrices(width):
    i = np.arange(width)
    same = (i[:, None] // HEAD_DIM) == (i[None, :] // HEAD_DIM)
    hsum = np.where(same, 1.0 / HEAD_DIM, 0.0)
    half = HEAD_DIM // 2
    src, dst = i[:, None], i[None, :]
    rot = np.where(same & (src == dst + half), -1.0, 0.0) + np.where(same & (src == dst - half), 1.0, 0.0)
    j = np.arange(LANES)
    swap = (j[:, None] == (j[None, :] + HEAD_DIM) % LANES).astype(np.float32)
    return jnp.asarray(hsum, BF16), jnp.asarray(rot, BF16), jnp.asarray(swap, BF16)


def _dsa_mixer(h, bsz, seq, q_gain, k_gain, tm=512):
    n = bsz * seq
    tb = ATTN_BLOCK
    assert seq % DSA_KC == 0 and seq % tm == 0
    topk = min(A_TOPK_MAX, seq // 4)
    cos, sin = _rope_tables(seq, LANES)
    hsum, rot, swap = _head_matrices(A_HEADS * HEAD_DIM)
    wide = A_HEADS * HEAD_DIM
    col = lambda name: EVEN_LAYOUT[name][0] // EVEN_LAYOUT[name][1]
    hspec = lambda name: pl.BlockSpec((tm, EVEN_LAYOUT[name][1]), lambda i, c=col(name): (i, c))
    full = lambda a: pl.BlockSpec(a.shape, lambda i: (0,) * a.ndim)
    pos_spec = pl.BlockSpec((tm, LANES), lambda i: (i % (seq // tm), 0))
    gq = jnp.tile(q_gain.astype(F32), A_HEADS).reshape(1, wide)
    gk = jnp.tile(k_gain.astype(F32), A_KV_HEADS).reshape(1, LANES)
    qn, iqn, k4, v4, ik2, iwn = pl.pallas_call(
        _dsa_prep_kernel,
        grid=(n // tm,),
        in_specs=[hspec('q'), hspec('iq'), hspec('k'), hspec('v'), hspec('ik'), hspec('iw'),
                  pos_spec, pos_spec, full(gq), full(gk), full(hsum), full(rot), full(swap)],
        out_specs=[pl.BlockSpec((tm, wide), lambda i: (i, 0)),
                   pl.BlockSpec((tm, wide), lambda i: (i, 0)),
                   pl.BlockSpec((4, tm, LANES), lambda i: (0, i, 0)),
                   pl.BlockSpec((4, tm, LANES), lambda i: (0, i, 0)),
                   pl.BlockSpec((2, tm, LANES), lambda i: (0, i, 0)),
                   pl.BlockSpec((tm, LANES), lambda i: (i, 0))],
        out_shape=[jax.ShapeDtypeStruct((n, wide), BF16), jax.ShapeDtypeStruct((n, wide), BF16),
                   jax.ShapeDtypeStruct((4, n, LANES), BF16), jax.ShapeDtypeStruct((4, n, LANES), BF16),
                   jax.ShapeDtypeStruct((2, n, LANES), BF16), jax.ShapeDtypeStruct((n, LANES), F32)],
        compiler_params=pltpu.CompilerParams(
            dimension_semantics=("arbitrary",), vmem_limit_bytes=VMEM_LIMIT),
        name="dsa_prep",
    )(h, h, h, h, h, h, cos, sin, gq, gk, hsum, rot, swap)

    r = np.arange(DSA_KC)
    tri = jnp.asarray(r[:, None] <= r[None, :], BF16)
    nq = seq // tb
    qspec = pl.BlockSpec((tb, wide), lambda b, i: (b * nq + i, 0))
    seq_spec = lambda lead: pl.BlockSpec((lead, seq, LANES), lambda b, i: (0, b, 0))
    return pl.pallas_call(
        functools.partial(_dsa_attn_kernel, topk=topk),
        grid=(bsz, nq),
        in_specs=[qspec, qspec, pl.BlockSpec((tb, LANES), lambda b, i: (b * nq + i, 0)),
                  seq_spec(4), seq_spec(4), seq_spec(2),
                  pl.BlockSpec(tri.shape, lambda b, i: (0, 0))],
        out_specs=qspec,
        out_shape=jax.ShapeDtypeStruct((n, wide), F32),
        scratch_shapes=[pltpu.VMEM((seq // DSA_KC, tb, DSA_KC), jnp.int32),
                        pltpu.VMEM((A_HEADS, tb, 1), F32), pltpu.VMEM((A_HEADS, tb, 1), F32),
                        pltpu.VMEM((A_HEADS // 2, tb, LANES), F32)],
        compiler_params=pltpu.CompilerParams(
            dimension_semantics=("arbitrary", "arbitrary"), vmem_limit_bytes=VMEM_LIMIT),
        name="dsa_attn",
    )(qn, iqn, iwn, k4, v4, ik2, tri)


def _dsa_attention(q, k, v, iq, ik, iw, q_gain, k_gain, pos):
    bsz, seq = q.shape[:2]
    topk = min(A_TOPK_MAX, seq // 4)
    rep = A_HEADS // A_KV_HEADS
    q = _rope(_rms_norm(q, q_gain), pos).astype(F32) * HEAD_DIM ** -0.5
    k = _rope(_rms_norm(k, k_gain), pos).astype(F32)
    v = v.astype(F32)
    iq = _rope(iq, pos).astype(F32) * IDX_DIM ** -0.5
    ik = _rope(ik[:, :, None, :], pos)[:, :, 0, :].astype(F32)
    iw = iw.astype(F32) * IDX_HEADS ** -0.5
    key_pos = jnp.arange(seq)
    gather = jax.vmap(lambda t, idx: t[idx])

    def block(args):
        qb, iqb, iwb, start = args
        tq = start + jnp.arange(ATTN_BLOCK)
        rel = jax.nn.relu(jnp.einsum('bqhd,bsd->bqhs', iqb, ik))
        score = jnp.einsum('bqh,bqhs->bqs', iwb, rel)
        causal = key_pos[None, :] <= tq[:, None]
        score = jnp.where(causal[None], score, -jnp.inf)
        _, sel = lax.top_k(score, topk)
        valid = sel <= tq[None, :, None]
        k_sel = gather(k, sel)
        v_sel = gather(v, sel)
        qg = qb.reshape(bsz, ATTN_BLOCK, A_KV_HEADS, rep, HEAD_DIM)
        s = jnp.einsum('bqgrd,bqkgd->bqgrk', qg, k_sel)
        p = _masked_softmax(s, valid[:, :, None, None, :])
        o = jnp.einsum('bqgrk,bqkgd->bqgrd', p, v_sel)
        return o.reshape(bsz, ATTN_BLOCK, A_HEADS * HEAD_DIM)

    starts = jnp.arange(seq // ATTN_BLOCK) * ATTN_BLOCK
    out = lax.map(block, (_to_blocks(q, ATTN_BLOCK), _to_blocks(iq, ATTN_BLOCK),
                          _to_blocks(iw, ATTN_BLOCK), starts))
    return out.swapaxes(0, 1).reshape(bsz, seq, A_HEADS * HEAD_DIM)


S5_STATES = B_GROUPS * B_STATE
S5_CHUNK = 64


def _s5_kernel(u_ref, bmat_ref, cmat_ref, a_re_ref, a_im_ref, d_ref, wg_ref, bg_ref, o_ref, h_scr):
    rows = u_ref.shape[0]
    nb = SUBLANES
    ns = S5_STATES

    @pl.when(pl.program_id(0) == 0)
    def _():
        h_scr[0:nb, :] = jnp.zeros((nb, 2 * ns), F32)

    u = u_ref[...]
    h_scr[nb:, :] = jnp.dot(u.astype(BF16), bmat_ref[...], preferred_element_type=F32)
    a_re = a_re_ref[...]
    a_im = a_im_ref[...]

    def step(t, carry):
        prev = pl.multiple_of(t * nb, nb)
        cur = pl.multiple_of(t * nb + nb, nb)
        p_re = h_scr[pl.ds(prev, nb), 0:ns]
        p_im = h_scr[pl.ds(prev, nb), ns:]
        h_scr[pl.ds(cur, nb), 0:ns] = a_re * p_re - a_im * p_im + h_scr[pl.ds(cur, nb), 0:ns]
        h_scr[pl.ds(cur, nb), ns:] = a_re * p_im + a_im * p_re + h_scr[pl.ds(cur, nb), ns:]
        return carry

    lax.fori_loop(0, rows // nb, step, 0)
    h_all = h_scr[nb:, :]
    h_scr[0:nb, :] = h_scr[rows:, :]
    y = jnp.dot(h_all.astype(BF16), cmat_ref[...], preferred_element_type=F32) + d_ref[...] * u
    y = jax.nn.gelu(y)
    gate = jnp.dot(y.astype(BF16), wg_ref[...], preferred_element_type=F32) + bg_ref[...]
    o_ref[...] = y * jax.nn.sigmoid(gate)


def _s5_operators(log_dt, a_re, a_im, b_re, b_im, c_re, c_im):
    dt = jnp.exp(log_dt.astype(F32))[:, None]
    lr, li = a_re.astype(F32), a_im.astype(F32)
    mag = jnp.exp(dt * lr)
    ab_re = mag * jnp.cos(dt * li)
    ab_im = mag * jnp.sin(dt * li)
    den = lr * lr + li * li
    f_re = ((ab_re - 1.0) * lr + ab_im * li) / den
    f_im = (ab_im * lr - (ab_re - 1.0) * li) / den
    br, bi = b_re.astype(F32), b_im.astype(F32)
    bb_re = f_re[..., None] * br - f_im[..., None] * bi
    bb_im = f_re[..., None] * bi + f_im[..., None] * br
    eye = jnp.eye(B_GROUPS, dtype=F32)
    blk_in = lambda m: jnp.einsum('gpc,gh->gchp', m, eye).reshape(B_WIDTH, S5_STATES)
    bmat = jnp.concatenate([blk_in(bb_re), blk_in(bb_im)], axis=1)
    blk_out = lambda m: jnp.einsum('gcp,gh->gphc', m.astype(F32), eye).reshape(S5_STATES, B_WIDTH)
    cmat = jnp.concatenate([blk_out(c_re), -blk_out(c_im)], axis=0)
    bcast = lambda m: jnp.broadcast_to(m.reshape(1, S5_STATES), (SUBLANES, S5_STATES))
    return bmat.astype(BF16), cmat.astype(BF16), bcast(ab_re), bcast(ab_im)


def _s5_mixer(u, log_dt, a_re, a_im, b_re, b_im, c_re, c_im, d_skip, w_glu, b_glu):
    bsz, seq, width = u.shape
    assert bsz == SUBLANES and width == B_WIDTH and seq % S5_CHUNK == 0
    bmat, cmat, ab_re, ab_im = _s5_operators(log_dt, a_re, a_im, b_re, b_im, c_re, c_im)
    u_tm = u.transpose(1, 0, 2).reshape(seq * bsz, width)
    rows = S5_CHUNK * bsz
    full = lambda shape: pl.BlockSpec(shape, lambda i: (0,) * len(shape))
    out = pl.pallas_call(
        _s5_kernel,
        grid=(seq // S5_CHUNK,),
        in_specs=[pl.BlockSpec((rows, width), lambda i: (i, 0)),
                  full(bmat.shape), full(cmat.shape), full(ab_re.shape), full(ab_im.shape),
                  full((1, width)), full((width, width)), full((1, width))],
        out_specs=pl.BlockSpec((rows, width), lambda i: (i, 0)),
        out_shape=jax.ShapeDtypeStruct((seq * bsz, width), F32),
        scratch_shapes=[pltpu.VMEM((rows + bsz, 2 * S5_STATES), F32)],
        compiler_params=pltpu.CompilerParams(
            dimension_semantics=("arbitrary",), vmem_limit_bytes=VMEM_LIMIT),
        name="s5_scan",
    )(u_tm, bmat, cmat, ab_re, ab_im, d_skip.astype(F32).reshape(1, width),
      w_glu.astype(BF16), b_glu.astype(F32).reshape(1, width))
    return out.reshape(seq, bsz, width).transpose(1, 0, 2)


def _s5_ssm(u, log_dt, a_re, a_im, b_re, b_im, c_re, c_im, d_skip, w_glu, b_glu):
    bsz, seq = u.shape[:2]
    uf = u.astype(F32).reshape(bsz, seq, B_GROUPS, B_GROUP)
    dt = jnp.exp(log_dt.astype(F32))[:, None]
    lr, li = a_re.astype(F32), a_im.astype(F32)
    mag = jnp.exp(dt * lr)
    ab_re = mag * jnp.cos(dt * li)
    ab_im = mag * jnp.sin(dt * li)
    den = lr * lr + li * li
    f_re = ((ab_re - 1.0) * lr + ab_im * li) / den
    f_im = (ab_im * lr - (ab_re - 1.0) * li) / den
    br, bi = b_re.astype(F32), b_im.astype(F32)
    bb_re = f_re[..., None] * br - f_im[..., None] * bi
    bb_im = f_re[..., None] * bi + f_im[..., None] * br
    x_re = jnp.einsum('gpc,bsgc->bsgp', bb_re, uf)
    x_im = jnp.einsum('gpc,bsgc->bsgp', bb_im, uf)
    a_re_t = jnp.broadcast_to(ab_re, x_re.shape)
    a_im_t = jnp.broadcast_to(ab_im, x_im.shape)

    def combine(e1, e2):
        a1r, a1i, b1r, b1i = e1
        a2r, a2i, b2r, b2i = e2
        return (a1r * a2r - a1i * a2i, a1r * a2i + a1i * a2r,
                a2r * b1r - a2i * b1i + b2r, a2r * b1i + a2i * b1r + b2i)

    _, _, h_re, h_im = lax.associative_scan(combine, (a_re_t, a_im_t, x_re, x_im), axis=1)
    y = (jnp.einsum('gcp,bsgp->bsgc', c_re.astype(F32), h_re)
         - jnp.einsum('gcp,bsgp->bsgc', c_im.astype(F32), h_im)
         + d_skip.astype(F32) * uf)
    y = jax.nn.gelu(y.reshape(bsz, seq, B_WIDTH))
    return y * jax.nn.sigmoid(y @ w_glu.astype(F32) + b_glu.astype(F32))


NSA_KC = 1024
NSA_WIN_SPAN = WIN + ATTN_BLOCK


def _nsa_prep_kernel(q_ref, ks_ref, vs_ref, kw_ref, vw_ref, gl_ref, cos_ref, sin_ref, qg_ref, kg_ref,
                     hsum_ref, rot_ref, swap_ref, qn_ref, ks4_ref, vs4_ref, kw4_ref, vw4_ref, gate_ref):
    cos1, sin1 = cos_ref[...], sin_ref[...]
    reps = q_ref.shape[1] // LANES
    cos4, sin4 = jnp.tile(cos1, (1, reps)), jnp.tile(sin1, (1, reps))
    hsum, rot = hsum_ref[...], rot_ref[...]
    hsum1, rot1 = hsum[0:LANES, 0:LANES], rot[0:LANES, 0:LANES]
    swap = swap_ref[...]
    q = _head_rope(_head_rms(q_ref[...], qg_ref[...], hsum), cos4, sin4, rot) * HEAD_DIM ** -0.5
    qn_ref[...] = q.astype(BF16)
    for src, dst, is_key in ((ks_ref, ks4_ref, True), (vs_ref, vs4_ref, False),
                             (kw_ref, kw4_ref, True), (vw_ref, vw4_ref, False)):
        t = src[...]
        if is_key:
            t = _head_rope(_head_rms(t, kg_ref[...], hsum1), cos1, sin1, rot1)
        for n, part in enumerate(_split_pair(t.astype(BF16), swap)):
            dst[n] = part
    gate_ref[...] = jax.nn.sigmoid(gl_ref[...])


def _nsa_cmp_kernel(hk_ref, hv_ref, pos_ref, kw1_ref, kw2_ref, vw1_ref, vw2_ref, cos_ref, sin_ref,
                    kg_ref, hsum_ref, rot_ref, swap_ref, kc_ref, vc_ref):
    def mlp(h_ref, w1_ref, w2_ref):
        h = h_ref[0]
        first = jnp.dot((h + pos_ref[0:1, :]).astype(BF16), w1_ref[0], preferred_element_type=F32)
        second = jnp.dot((h + pos_ref[1:2, :]).astype(BF16), w1_ref[1], preferred_element_type=F32)
        z = jax.nn.gelu(first + pltpu.roll(second, second.shape[0] - 1, axis=0))
        return jnp.dot(z.astype(BF16), w2_ref[...], preferred_element_type=F32)

    swap = swap_ref[...]
    kc = _head_rope(_head_rms(mlp(hk_ref, kw1_ref, kw2_ref), kg_ref[...], hsum_ref[...]),
                    cos_ref[...], sin_ref[...], rot_ref[...])
    kc_a, kc_b, _, _ = _split_pair(kc.astype(BF16), swap)
    kc_ref[0, 0] = kc_a
    kc_ref[0, 1] = kc_b
    vc_a, vc_b, _, _ = _split_pair(mlp(hv_ref, vw1_ref, vw2_ref).astype(BF16), swap)
    vc_ref[0, 0] = vc_a
    vc_ref[0, 1] = vc_b


def _softmax_rows(s, mask):
    sm = jnp.where(mask > 0.0, s, MASKED)
    p = jnp.exp(sm - jnp.max(sm, axis=1, keepdims=True)) * mask
    return p / jnp.maximum(jnp.sum(p, axis=1, keepdims=True), 1e-30)


def _nsa_attn_kernel(q_ref, gate_ref, kc_ref, vc_ref, ks4_ref, vs4_ref, kw4_ref, vw4_ref,
                     m2s_ref, blk_ref, gx_ref, o_ref, m_scr, l_scr, acc_scr, oc_scr, ow_scr, *, top_n):
    tb = q_ref.shape[0]
    kc = NSA_KC
    qi = pl.program_id(1)
    start = qi * tb
    tq = start + lax.broadcasted_iota(jnp.int32, (tb, 1), 0)
    low = lax.broadcasted_iota(jnp.int32, (tb, LANES), 1) < HEAD_DIM
    heads_per_group = C_HEADS // C_KV_HEADS
    pairs = C_HEADS // 2

    n_cmp = kc_ref.shape[3]
    cmp_end = lax.broadcasted_iota(jnp.int32, (tb, n_cmp), 1) * CMP_STRIDE + (CMP_LEN - 1)
    cmask = jnp.where(cmp_end <= tq, 1.0, 0.0)
    p_sum = [jnp.zeros((tb, n_cmp), F32) for _ in range(C_KV_HEADS)]
    for j in range(pairs):
        g = (2 * j) // heads_per_group
        qp = q_ref[:, j * LANES:(j + 1) * LANES]
        p_a = _softmax_rows(_dot_nt(qp, kc_ref[0, g, 0]), cmask)
        p_b = _softmax_rows(_dot_nt(qp, kc_ref[0, g, 1]), cmask)
        p_sum[g] = p_sum[g] + p_a + p_b
        oc_scr[j] = (jnp.dot(p_a.astype(BF16), vc_ref[0, g, 0], preferred_element_type=F32)
                     + jnp.dot(p_b.astype(BF16), vc_ref[0, g, 1], preferred_element_type=F32))

    n_slc = m2s_ref.shape[0]
    slc_id = lax.broadcasted_iota(jnp.int32, (n_slc, tb), 0)
    cur = (start + lax.broadcasted_iota(jnp.int32, (n_slc, tb), 1)) // SLC_LEN
    forced = jnp.where(slc_id == cur, 1.0, jnp.where(slc_id == 0, 1.0, 0.0))
    chosen = []
    for g in range(C_KV_HEADS):
        hi = p_sum[g].astype(BF16)
        lo = (p_sum[g] - hi.astype(F32)).astype(BF16)
        imp = _dot_nt(m2s_ref[...], hi) + _dot_nt(m2s_ref[...], lo)
        imp = jnp.where(forced > 0.0, FORCE_SCORE, imp)
        imp = jnp.where(slc_id <= cur, imp, NEG_INF)
        picked = jnp.zeros((n_slc, tb), F32)
        for _ in range(top_n):
            m = jnp.max(imp, axis=0, keepdims=True)
            first = jnp.min(jnp.where(imp == m, slc_id, n_slc), axis=0, keepdims=True)
            hit = slc_id == first
            picked = jnp.where(hit, 1.0, picked)
            imp = jnp.where(hit, NEG_INF, imp)
        chosen.append(picked.T.astype(BF16))

    m_scr[...] = jnp.full(m_scr.shape, MAX_FLOOR, F32)
    l_scr[...] = jnp.zeros(l_scr.shape, F32)
    acc_scr[...] = jnp.zeros(acc_scr.shape, F32)
    lane_k = lax.broadcasted_iota(jnp.int32, (tb, kc), 1)

    def slc_chunk(c, carry):
        ks = pl.multiple_of(c * kc, kc)
        causal = ks + lane_k <= tq
        sel = [jnp.where(causal, jnp.dot(chosen[g], blk_ref[c], preferred_element_type=F32), 0.0)
               for g in range(C_KV_HEADS)]

        def head(s, h, g):
            sm = jnp.where(sel[g] > 0.0, s, MASKED)
            m_old = m_scr[h]
            m_new = jnp.maximum(m_old, jnp.max(sm, axis=1, keepdims=True))
            p = jnp.exp(sm - m_new)
            alpha = jnp.exp(m_old - m_new)
            l_scr[h] = alpha * l_scr[h] + jnp.sum(p, axis=1, keepdims=True)
            m_scr[h] = m_new
            return p.astype(BF16), alpha

        for j in range(pairs):
            g = (2 * j) // heads_per_group
            qp = q_ref[:, j * LANES:(j + 1) * LANES]
            p_a, alpha_a = head(_dot_nt(qp, ks4_ref[2 * g, pl.ds(ks, kc), :]), 2 * j, g)
            p_b, alpha_b = head(_dot_nt(qp, ks4_ref[2 * g + 1, pl.ds(ks, kc), :]), 2 * j + 1, g)
            pv = (jnp.dot(p_a, vs4_ref[2 * g, pl.ds(ks, kc), :], preferred_element_type=F32)
                  + jnp.dot(p_b, vs4_ref[2 * g + 1, pl.ds(ks, kc), :], preferred_element_type=F32))
            acc_scr[j] = jnp.where(low, alpha_a, alpha_b) * acc_scr[j] + pv
        return carry

    lax.fori_loop(0, qi // (kc // tb) + 1, slc_chunk, 0)

    ws = pl.multiple_of(jnp.maximum(start - WIN, 0), tb)
    wpos = ws + lax.broadcasted_iota(jnp.int32, (tb, NSA_WIN_SPAN), 1)
    wmask = jnp.where(wpos <= tq, jnp.where(wpos > tq - WIN, 1.0, 0.0), 0.0)
    for j in range(pairs):
        g = (2 * j) // heads_per_group
        qp = q_ref[:, j * LANES:(j + 1) * LANES]
        p_a = _softmax_rows(_dot_nt(qp, kw4_ref[2 * g, pl.ds(ws, NSA_WIN_SPAN), :]), wmask)
        p_b = _softmax_rows(_dot_nt(qp, kw4_ref[2 * g + 1, pl.ds(ws, NSA_WIN_SPAN), :]), wmask)
        ow_scr[j] = (jnp.dot(p_a.astype(BF16), vw4_ref[2 * g, pl.ds(ws, NSA_WIN_SPAN), :],
                             preferred_element_type=F32)
                     + jnp.dot(p_b.astype(BF16), vw4_ref[2 * g + 1, pl.ds(ws, NSA_WIN_SPAN), :],
                               preferred_element_type=F32))

    gates = gate_ref[...]
    g_cmp, g_slc, g_win = (_exact_dot(gates, gx_ref[n]) for n in range(3))
    for j in range(pairs):
        cols = slice(j * LANES, (j + 1) * LANES)
        denom = jnp.maximum(jnp.where(low, l_scr[2 * j], l_scr[2 * j + 1]), 1e-30)
        o_ref[:, cols] = (g_cmp[:, cols] * oc_scr[j] + g_slc[:, cols] * (acc_scr[j] / denom)
                          + g_win[:, cols] * ow_scr[j])


def _nsa_mixer(q, kcmp, vcmp, kslc, vslc, kwin, vwin, gl, bsz, seq,
               q_gain, k_gain, cmp_pos, k_w1, k_w2, v_w1, v_w2, tm=512):
    n = bsz * seq
    tb = ATTN_BLOCK
    assert seq % NSA_KC == 0 and seq % tm == 0
    n_cmp = seq // CMP_STRIDE
    wide = C_HEADS * HEAD_DIM
    cos, sin = _rope_tables(seq, LANES)
    hsum, rot, swap = _head_matrices(wide)
    hsum1, rot1 = hsum[:LANES, :LANES], rot[:LANES, :LANES]
    full = lambda a: pl.BlockSpec(a.shape, lambda *i: (0,) * a.ndim)
    rows = lambda w: pl.BlockSpec((tm, w), lambda i: (i, 0))
    quad = pl.BlockSpec((4, tm, LANES), lambda i: (0, i, 0))
    pos_spec = pl.BlockSpec((tm, LANES), lambda i: (i % (seq // tm), 0))
    gq = jnp.tile(q_gain.astype(F32), C_HEADS).reshape(1, wide)
    gk = jnp.tile(k_gain.astype(F32), C_KV_HEADS).reshape(1, LANES)
    gl_pad = jnp.pad(gl, ((0, 0), (0, LANES - gl.shape[1])))
    params1 = pltpu.CompilerParams(dimension_semantics=("arbitrary",), vmem_limit_bytes=VMEM_LIMIT)
    quad_shape = jax.ShapeDtypeStruct((4, n, LANES), BF16)
    qn, ks4, vs4, kw4, vw4, gates = pl.pallas_call(
        _nsa_prep_kernel,
        grid=(n // tm,),
        in_specs=[rows(wide)] + [rows(LANES)] * 5 + [pos_spec, pos_spec, full(gq), full(gk),
                                                     full(hsum), full(rot), full(swap)],
        out_specs=[rows(wide), quad, quad, quad, quad, rows(LANES)],
        out_shape=[jax.ShapeDtypeStruct((n, wide), BF16), quad_shape, quad_shape, quad_shape,
                   quad_shape, jax.ShapeDtypeStruct((n, LANES), F32)],
        compiler_params=params1,
        name="nsa_prep",
    )(q, kslc, vslc, kwin, vwin, gl_pad, cos, sin, gq, gk, hsum, rot, swap)

    row_w = CMP_STRIDE * HEAD_DIM
    to_rows = lambda t: (t.reshape(bsz, seq, C_KV_HEADS, HEAD_DIM).transpose(0, 2, 1, 3)
                         .reshape(bsz * C_KV_HEADS, n_cmp, row_w))
    pos2 = cmp_pos.astype(F32).reshape(2, row_w)
    w1_halves = lambda w: w.astype(BF16).reshape(2, row_w, CMP_HIDDEN)
    w2_pad = lambda w: jnp.pad(w.astype(BF16), ((0, 0), (0, LANES - HEAD_DIM)))
    half = HEAD_DIM // 2
    cmp_end = jnp.arange(n_cmp, dtype=F32) * CMP_STRIDE + (CMP_LEN - 1)
    ang = cmp_end[:, None] * (ROPE_THETA ** (-jnp.arange(half, dtype=F32) / half))[None, :]
    cos_c, sin_c = jnp.tile(jnp.cos(ang), (1, LANES // half)), jnp.tile(jnp.sin(ang), (1, LANES // half))
    seq_rows = pl.BlockSpec((1, n_cmp, row_w), lambda i: (i, 0, 0))
    pair_out = pl.BlockSpec((1, 2, n_cmp, LANES), lambda i: (i, 0, 0, 0))
    pair_shape = jax.ShapeDtypeStruct((bsz * C_KV_HEADS, 2, n_cmp, LANES), BF16)
    cmp_in = (to_rows(kcmp), to_rows(vcmp), pos2, w1_halves(k_w1), w2_pad(k_w2), w1_halves(v_w1),
              w2_pad(v_w2), cos_c, sin_c, gk, hsum1, rot1, swap)
    kc2, vc2 = pl.pallas_call(
        _nsa_cmp_kernel,
        grid=(bsz * C_KV_HEADS,),
        in_specs=[seq_rows, seq_rows] + [full(a) for a in cmp_in[2:]],
        out_specs=[pair_out, pair_out],
        out_shape=[pair_shape, pair_shape],
        compiler_params=params1,
        name="nsa_compress",
    )(*cmp_in)
    kc2 = kc2.reshape(bsz, C_KV_HEADS, 2, n_cmp, LANES)
    vc2 = vc2.reshape(bsz, C_KV_HEADS, 2, n_cmp, LANES)

    n_slc = seq // SLC_LEN
    ratio, span = SLC_LEN // CMP_STRIDE, CMP_LEN // CMP_STRIDE
    off = np.arange(n_cmp)[None, :] - ratio * np.arange(n_slc)[:, None]
    m2s = np.maximum(np.minimum(ratio - 1, off) - np.maximum(0, off - span + 1) + 1, 0)
    m2s = np.pad(m2s, ((0, LANES - n_slc), (0, 0)))
    key_blk = np.arange(seq) // SLC_LEN
    blk = (np.arange(LANES)[:, None] == key_blk[None, :]).reshape(LANES, seq // NSA_KC, NSA_KC)
    blk = jnp.asarray(blk.transpose(1, 0, 2), BF16)
    col = np.arange(wide) // HEAD_DIM
    gx = np.stack([np.arange(LANES)[:, None] == (3 * col + br)[None, :] for br in range(3)])
    gx = jnp.asarray(gx, BF16)
    m2s = jnp.asarray(m2s, BF16)
    nq = seq // tb
    qspec = pl.BlockSpec((tb, wide), lambda b, i: (b * nq + i, 0))
    cmp_spec = pl.BlockSpec((1, C_KV_HEADS, 2, n_cmp, LANES), lambda b, i: (b, 0, 0, 0, 0))
    seq_spec = pl.BlockSpec((4, seq, LANES), lambda b, i: (0, b, 0))
    assert n_slc <= LANES
    return pl.pallas_call(
        functools.partial(_nsa_attn_kernel, top_n=min(SLC_TOPN, n_slc)),
        grid=(bsz, nq),
        in_specs=[qspec, pl.BlockSpec((tb, LANES), lambda b, i: (b * nq + i, 0)), cmp_spec, cmp_spec,
                  seq_spec, seq_spec, seq_spec, seq_spec, full(m2s), full(blk), full(gx)],
        out_specs=qspec,
        out_shape=jax.ShapeDtypeStruct((n, wide), F32),
        scratch_shapes=[pltpu.VMEM((C_HEADS, tb, 1), F32), pltpu.VMEM((C_HEADS, tb, 1), F32),
                        pltpu.VMEM((C_HEADS // 2, tb, LANES), F32),
                        pltpu.VMEM((C_HEADS // 2, tb, LANES), F32),
                        pltpu.VMEM((C_HEADS // 2, tb, LANES), F32)],
        compiler_params=pltpu.CompilerParams(
            dimension_semantics=("arbitrary", "arbitrary"), vmem_limit_bytes=PEER_VMEM_LIMIT),
        name="nsa_attn",
    )(qn, gates, kc2, vc2, ks4, vs4, kw4, vw4, m2s, blk, gx)


def _compress_blocks(t, cmp_pos, w1, w2):
    bsz, seq, groups, dh = t.shape
    n_cmp = (seq - CMP_LEN) // CMP_STRIDE + 1
    idx = jnp.arange(n_cmp)[:, None] * CMP_STRIDE + jnp.arange(CMP_LEN)[None, :]
    blocks = t.astype(F32)[:, idx] + cmp_pos.astype(F32)[:, None, :]
    blocks = blocks.transpose(0, 1, 3, 2, 4).reshape(bsz, n_cmp, groups, CMP_LEN * dh)
    return jax.nn.gelu(blocks @ w1.astype(F32)) @ w2.astype(F32)


def _nsa_attention(q, k_cmp, v_cmp, k_slc, v_slc, k_win, v_win, gate_logits,
                   q_gain, k_gain, cmp_pos, k_w1, k_w2, v_w1, v_w2, pos):
    bsz, seq = q.shape[:2]
    rep = C_HEADS // C_KV_HEADS
    n_cmp = (seq - CMP_LEN) // CMP_STRIDE + 1
    n_slc = seq // SLC_LEN
    top_n = min(SLC_TOPN, n_slc)
    q = _rope(_rms_norm(q, q_gain), pos).astype(F32) * HEAD_DIM ** -0.5
    cmp_end = jnp.arange(n_cmp) * CMP_STRIDE + (CMP_LEN - 1)
    kc = _rope(_rms_norm(_compress_blocks(k_cmp, cmp_pos, k_w1, k_w2), k_gain), cmp_end)
    vc = _compress_blocks(v_cmp, cmp_pos, v_w1, v_w2)
    ks = _rope(_rms_norm(k_slc, k_gain), pos).astype(F32)
    ks = ks.reshape(bsz, n_slc, SLC_LEN, C_KV_HEADS, HEAD_DIM).transpose(0, 3, 1, 2, 4)
    vs = v_slc.astype(F32).reshape(bsz, n_slc, SLC_LEN, C_KV_HEADS, HEAD_DIM).transpose(0, 3, 1, 2, 4)
    pad = ((0, 0), (WIN, 0), (0, 0), (0, 0))
    kw = jnp.pad(_rope(_rms_norm(k_win, k_gain), pos).astype(F32), pad)
    vw = jnp.pad(v_win.astype(F32), pad)
    gates = jax.nn.sigmoid(gate_logits.astype(F32))
    ratio = SLC_LEN // CMP_STRIDE
    span = CMP_LEN // CMP_STRIDE
    off = jnp.arange(n_cmp)[:, None] - ratio * jnp.arange(n_slc)[None, :]
    cmp_to_slc = jnp.maximum(
        jnp.minimum(ratio - 1, off) - jnp.maximum(0, off - span + 1) + 1, 0).astype(F32)
    slc_ids = jnp.arange(n_slc)
    tok_off = jnp.arange(SLC_LEN)
    win_off = jnp.arange(WIN + C_BLOCK)
    gather = jax.vmap(jax.vmap(lambda t, idx: t[idx]))

    def block(args):
        qb, gb, start = args
        tq = start + jnp.arange(C_BLOCK)
        qg = qb.reshape(bsz, C_BLOCK, C_KV_HEADS, rep, HEAD_DIM)
        gb = gb.reshape(bsz, C_BLOCK, C_KV_HEADS, rep, 3)
        s_c = jnp.einsum('bqgrd,bcgd->bqgrc', qg, kc)
        p_c = _masked_softmax(s_c, (cmp_end[None, :] <= tq[:, None])[None, :, None, None, :])
        o_c = jnp.einsum('bqgrc,bcgd->bqgrd', p_c, vc)
        imp = jnp.einsum('bqgrc,cj->bqgj', p_c, cmp_to_slc)
        cur = tq // SLC_LEN
        forced = (slc_ids[None, :] == cur[:, None]) | (slc_ids[None, :] == 0)
        admissible = slc_ids[None, :] <= cur[:, None]
        imp = jnp.where(forced[None, :, None, :], FORCE_SCORE, imp)
        imp = jnp.where(admissible[None, :, None, :], imp, -jnp.inf)
        _, sel = lax.top_k(imp, top_n)
        sel_g = sel.transpose(0, 2, 1, 3)
        k_sel = gather(ks, sel_g)
        v_sel = gather(vs, sel_g)
        s_s = jnp.einsum('bqgrd,bgqnld->bqgrnl', qg, k_sel)
        kpos = sel[..., None] * SLC_LEN + tok_off
        smask = (kpos <= tq[None, :, None, None, None]).reshape(bsz, C_BLOCK, C_KV_HEADS, 1, -1)
        p_s = _masked_softmax(s_s.reshape(bsz, C_BLOCK, C_KV_HEADS, rep, -1), smask)
        o_s = jnp.einsum('bqgrnl,bgqnld->bqgrd', p_s.reshape(s_s.shape), v_sel)
        kwb = lax.dynamic_slice_in_dim(kw, start, WIN + C_BLOCK, axis=1)
        vwb = lax.dynamic_slice_in_dim(vw, start, WIN + C_BLOCK, axis=1)
        wpos = start - WIN + win_off
        wmask = ((wpos[None, :] <= tq[:, None]) & (wpos[None, :] > tq[:, None] - WIN)
                 & (wpos[None, :] >= 0))
        s_w = jnp.einsum('bqgrd,bkgd->bqgrk', qg, kwb)
        p_w = _masked_softmax(s_w, wmask[None, :, None, None, :])
        o_w = jnp.einsum('bqgrk,bkgd->bqgrd', p_w, vwb)
        o = gb[..., 0:1] * o_c + gb[..., 1:2] * o_s + gb[..., 2:3] * o_w
        return o.reshape(bsz, C_BLOCK, C_HEADS * HEAD_DIM)

    starts = jnp.arange(seq // C_BLOCK) * C_BLOCK
    out = lax.map(block, (_to_blocks(q, C_BLOCK), _to_blocks(gates, C_BLOCK), starts))
    return out.swapaxes(0, 1).reshape(bsz, seq, C_HEADS * HEAD_DIM)


GLA_BLOCK = 256


def _gla_kernel(q_ref, k_ref, v_ref, gl_ref, r_ref, wg_ref, bg_ref, og_ref, tri_ref, same_ref,
                o_ref, st_scr):
    @pl.when(pl.program_id(1) == 0)
    def _():
        st_scr[...] = jnp.zeros(st_scr.shape, F32)

    tb = q_ref.shape[0]
    logit = jnp.dot(gl_ref[...].astype(BF16), wg_ref[...], preferred_element_type=F32) + bg_ref[...]
    log_a = jax.nn.log_sigmoid(logit) * (1.0 / GATE_TAU)
    a_hi = log_a.astype(BF16)
    a_lo = (log_a - a_hi.astype(F32)).astype(BF16)
    two_term = lambda m: (jnp.dot(m, a_hi, preferred_element_type=F32)
                          + jnp.dot(m, a_lo, preferred_element_type=F32))
    tri = tri_ref[...]
    bcum = two_term(tri)
    blast = two_term(same_ref[...])
    k = k_ref[...]
    q_dec = q_ref[...] * D_KDIM ** -0.5 * jnp.exp(bcum)
    k_inv = (k * jnp.exp(-bcum)).astype(BF16)
    k_rem = k * jnp.exp(blast - bcum)
    decay = jnp.exp(blast)
    causal = tri.astype(F32)
    low = lax.broadcasted_iota(jnp.int32, (tb, LANES), 1) < D_KDIM
    row_chunk = lax.broadcasted_iota(jnp.int32, (tb, LANES), 0) // GLA_CHUNK
    gain = og_ref[...]
    for j in range(D_HEADS // 2):
        cols = slice(j * LANES, (j + 1) * LANES)
        qd, ki, kr, dec = q_dec[:, cols], k_inv[:, cols], k_rem[:, cols], decay[:, cols]
        for h, qd_h in ((2 * j, jnp.where(low, qd, 0.0).astype(BF16)),
                        (2 * j + 1, jnp.where(low, 0.0, qd).astype(BF16))):
            vcols = slice(h * D_VDIM, (h + 1) * D_VDIM)
            v_h = v_ref[:, vcols]
            att = _dot_nt(qd_h, ki) * causal
            o_h = jnp.dot(att.astype(BF16), v_h.astype(BF16), preferred_element_type=F32)
            v_t = v_h.T.astype(BF16)
            inter = []
            for n in range(tb // GLA_CHUNK):
                rows = slice(n * GLA_CHUNK, (n + 1) * GLA_CHUNK)
                state = st_scr[h]
                inter.append(_dot_nt(qd_h[rows], state.astype(BF16)))
                upd = jnp.dot(v_t, jnp.where(row_chunk == n, kr, 0.0).astype(BF16),
                              preferred_element_type=F32)
                st_scr[h] = state * dec[n * GLA_CHUNK:n * GLA_CHUNK + 1, :] + upd
            o_h = o_h + jnp.concatenate(inter, axis=0)
            o_h = o_h * lax.rsqrt(jnp.mean(o_h * o_h, axis=-1, keepdims=True) + NORM_EPS) * gain
            o_ref[:, vcols] = o_h * jax.nn.silu(r_ref[:, vcols])


def _gla_mixer(gq, gk, gv, glow, gr, bsz, seq, w_gate, b_gate, o_gain):
    n = bsz * seq
    tb = GLA_BLOCK
    assert seq % tb == 0
    kw, vw = D_HEADS * D_KDIM, D_HEADS * D_VDIM
    glow_p = jnp.pad(glow, ((0, 0), (0, LANES - GATE_RANK)))
    wg = jnp.pad(w_gate.astype(BF16), ((0, LANES - GATE_RANK), (0, 0)))
    r = np.arange(tb)
    same = (r[:, None] // GLA_CHUNK) == (r[None, :] // GLA_CHUNK)
    tri = jnp.asarray(same & (r[None, :] <= r[:, None]), BF16)
    same = jnp.asarray(same, BF16)
    nb = seq // tb
    rows = lambda w: pl.BlockSpec((tb, w), lambda b, i: (b * nb + i, 0))
    full = lambda a: pl.BlockSpec(a.shape, lambda b, i: (0,) * a.ndim)
    bg = b_gate.astype(F32).reshape(1, kw)
    og = o_gain.astype(F32).reshape(1, D_VDIM)
    return pl.pallas_call(
        _gla_kernel,
        grid=(bsz, nb),
        in_specs=[rows(kw), rows(kw), rows(vw), rows(LANES), rows(vw),
                  full(wg), full(bg), full(og), full(tri), full(same)],
        out_specs=rows(vw),
        out_shape=jax.ShapeDtypeStruct((n, vw), F32),
        scratch_shapes=[pltpu.VMEM((D_HEADS, D_VDIM, LANES), F32)],
        compiler_params=pltpu.CompilerParams(
            dimension_semantics=("arbitrary", "arbitrary"), vmem_limit_bytes=VMEM_LIMIT),
        name="gla",
    )(gq, gk, gv, glow_p, gr, wg, bg, og, tri, same)


def _gla_attention(q, k, v, g_low, r, w_gate, b_gate, o_gain):
    bsz, seq = q.shape[:2]
    n_chunk = seq // GLA_CHUNK
    logit = g_low.astype(F32) @ w_gate.astype(F32) + b_gate.astype(F32)
    log_a = jax.nn.log_sigmoid(logit) / GATE_TAU

    def chunks(t):
        return t.astype(F32).reshape(bsz, n_chunk, GLA_CHUNK, D_HEADS, -1).transpose(0, 3, 1, 2, 4)

    qc = chunks(q) * D_KDIM ** -0.5
    kc, vc, gc = chunks(k), chunks(v), chunks(log_a)
    bcum = jnp.cumsum(gc, axis=3)
    blast = bcum[:, :, :, -1:, :]
    q_dec = qc * jnp.exp(bcum)
    k_inv = kc * jnp.exp(-bcum)
    causal = jnp.tril(jnp.ones((GLA_CHUNK, GLA_CHUNK), dtype=bool))
    att = jnp.where(causal, jnp.einsum('bhncd,bhnsd->bhncs', q_dec, k_inv), 0.0)
    o_intra = jnp.einsum('bhncs,bhnse->bhnce', att, vc)
    upd = jnp.einsum('bhncd,bhnce->bhnde', kc * jnp.exp(blast - bcum), vc)
    decay = jnp.exp(blast[:, :, :, 0, :])

    def step(state, inp):
        dec, u = inp
        return dec[..., None] * state + u, state

    init = jnp.zeros((bsz, D_HEADS, D_KDIM, D_VDIM), F32)
    _, prev = lax.scan(step, init, (jnp.moveaxis(decay, 2, 0), jnp.moveaxis(upd, 2, 0)))
    o_inter = jnp.einsum('bhncd,nbhde->bhnce', q_dec, prev)
    o = (o_intra + o_inter).transpose(0, 2, 3, 1, 4).reshape(bsz, seq, D_HEADS, D_VDIM)
    o = _rms_norm(o, o_gain).reshape(bsz, seq, D_HEADS * D_VDIM)
    return o * jax.nn.silu(r.astype(F32))


PEER_SLOTS = PEER_HEADS * PEER_TOPK
PEER_TT = 128
ROW_TILE = (SUBLANES, LANES)
NEG_INF = float('-inf')
ROUTE_ILP = 4


def _top16_rows(s, row_id, big):
    vals, ids = [], []
    for _ in range(PEER_TOPK):
        m = jnp.max(s, axis=0, keepdims=True)
        pick = jnp.min(jnp.where(s == m, row_id, big), axis=0, keepdims=True)
        s = jnp.where(row_id == pick, NEG_INF, s)
        vals.append(m)
        ids.append(pick)
    return vals, ids


def _peer_route_kernel(x_ref, g_ref, wq_ref, sk_ref, xn_ref, idx_ref, gate_ref, v_scr, i_scr):
    tt = x_ref.shape[0]
    x = x_ref[...]
    xn = x * lax.rsqrt(jnp.mean(x * x, axis=-1, keepdims=True) + NORM_EPS) * g_ref[...]
    xn_ref[...] = xn
    xb = xn.astype(BF16)
    key_id = lax.broadcasted_iota(jnp.int32, (PEER_KEYS, tt), 0)

    def sub_scores(step, carry):
        for u in range(ROUTE_ILP):
            hp = step * ROUTE_ILP + u
            q = jnp.dot(xb, wq_ref[hp], preferred_element_type=F32).astype(BF16)
            s = lax.dot_general(sk_ref[hp], q, (((1,), (1,)), ((), ())),
                                preferred_element_type=F32)
            vals, ids = _top16_rows(s, key_id, PEER_KEYS)
            v_scr[hp] = jnp.concatenate(vals, axis=0)
            i_scr[hp] = jnp.concatenate(ids, axis=0)
        return carry

    lax.fori_loop(0, 2 * PEER_HEADS // ROUTE_ILP, sub_scores, 0)

    n_blk = 10
    r = lax.broadcasted_iota(jnp.int32, (n_blk * SUBLANES, tt), 0)
    blk, b = r // SUBLANES, r % SUBLANES
    a_of_blk = jnp.where(blk < 2, 0, blk - 1)
    pos = jnp.where(blk == 9, (SUBLANES + b) * PEER_TOPK,
                    jnp.where(blk == 1, SUBLANES + b, a_of_blk * PEER_TOPK + b))
    b_max = jnp.where(blk < 3, 7, jnp.where(blk == 3, 4, jnp.where(blk == 4, 3, jnp.where(
        blk == 5, 2, jnp.where(blk < 9, 1, 7)))))
    live = b <= b_max

    def combine_head(h):
        v1, v2 = v_scr[2 * h], v_scr[2 * h + 1]
        i1, i2 = i_scr[2 * h], i_scr[2 * h + 1]
        lo_v, hi_v = v2[0:SUBLANES], v2[SUBLANES:]
        lo_i, hi_i = i2[0:SUBLANES], i2[SUBLANES:]
        cand = [v1[0:1] + lo_v, v1[0:1] + hi_v]
        cid = [i1[0:1] * PEER_KEYS + lo_i, i1[0:1] * PEER_KEYS + hi_i]
        for a in range(1, SUBLANES):
            cand.append(v1[a:a + 1] + lo_v)
            cid.append(i1[a:a + 1] * PEER_KEYS + lo_i)
        cand.append(v1[SUBLANES:] + v2[0:1])
        cid.append(i1[SUBLANES:] * PEER_KEYS + i2[0:1])
        cand = jnp.where(live, jnp.concatenate(cand, axis=0), NEG_INF)
        cid = jnp.concatenate(cid, axis=0)
        top_s, experts = [], []
        for _ in range(PEER_TOPK):
            m = jnp.max(cand, axis=0, keepdims=True)
            pick = jnp.min(jnp.where(cand == m, pos, PEER_TOPK * PEER_TOPK), axis=0, keepdims=True)
            hit = pos == pick
            experts.append(jnp.max(jnp.where(hit, cid, -1), axis=0, keepdims=True))
            cand = jnp.where(hit, NEG_INF, cand)
            top_s.append(m)
        top_s = jnp.concatenate(top_s, axis=0)
        ex = jnp.exp(top_s - top_s[0:1])
        gate_ref[0, h] = ex / jnp.sum(ex, axis=0, keepdims=True)
        idx_ref[0, h] = jnp.concatenate(experts, axis=0)

    def combine(step, carry):
        for u in range(ROUTE_ILP):
            combine_head(step * ROUTE_ILP + u)
        return carry

    lax.fori_loop(0, PEER_HEADS // ROUTE_ILP, combine, 0)


def _peer_route(x2d, g_ffn, wq, subkeys):
    n, dm = x2d.shape
    tt = PEER_TT
    half = PEER_QDIM // 2
    wq_t = wq.astype(BF16).reshape(dm, 2 * PEER_HEADS, half).transpose(1, 0, 2)
    sk = subkeys.astype(BF16).reshape(2 * PEER_HEADS, PEER_KEYS, half)
    tile4 = (1, PEER_HEADS, PEER_TOPK, tt)
    return pl.pallas_call(
        _peer_route_kernel,
        grid=(n // tt,),
        in_specs=[pl.BlockSpec((tt, dm), lambda i: (i, 0)),
                  pl.BlockSpec((1, dm), lambda i: (0, 0)),
                  pl.BlockSpec(wq_t.shape, lambda i: (0, 0, 0)),
                  pl.BlockSpec(sk.shape, lambda i: (0, 0, 0))],
        out_specs=[pl.BlockSpec((tt, dm), lambda i: (i, 0)),
                   pl.BlockSpec(tile4, lambda i: (i, 0, 0, 0)),
                   pl.BlockSpec(tile4, lambda i: (i, 0, 0, 0))],
        out_shape=[jax.ShapeDtypeStruct((n, dm), F32),
                   jax.ShapeDtypeStruct((n // tt,) + tile4[1:], jnp.int32),
                   jax.ShapeDtypeStruct((n // tt,) + tile4[1:], F32)],
        scratch_shapes=[pltpu.VMEM((2 * PEER_HEADS, PEER_TOPK, tt), F32),
                        pltpu.VMEM((2 * PEER_HEADS, PEER_TOPK, tt), jnp.int32)],
        compiler_params=pltpu.CompilerParams(
            dimension_semantics=("arbitrary",), vmem_limit_bytes=VMEM_LIMIT),
        name="peer_route",
    )(x2d, g_ffn.reshape(1, dm), wq_t, sk)


PEER_GROUP = 8


def _for_each_token(idx_hbm, idx_smem, sems, n_tok, per_token):
    tile = pl.program_id(0)
    n_grp = n_tok // PEER_GROUP

    def copy(g, buf):
        return pltpu.make_async_copy(idx_hbm.at[tile, g], idx_smem.at[buf], sems.at[buf])

    copy(0, 0).start()
    copy(1, 1).start()

    def step(p, carry):
        for buf in range(2):
            g = 2 * p + buf
            copy(g, buf).wait()
            for u in range(PEER_GROUP):
                per_token(g * PEER_GROUP + u,
                          lambda i, buf=buf, u=u: idx_smem[buf, u * PEER_SLOTS + i])

            @pl.when(g + 2 < n_grp)
            def _():
                copy(g + 2, buf).start()
        return carry

    lax.fori_loop(0, n_grp // 2, step, 0)


def _gathered_rows(tab_ref, idx_of):
    return jnp.concatenate([tab_ref[idx_of(i)] for i in range(PEER_SLOTS)], axis=0)


def _peer_up_kernel(idx_hbm, xn_ref, gate_ref, tab_ref, fold_ref, out_ref, idx_smem, rbuf, sems):
    tt = xn_ref.shape[0]
    eye = (lax.broadcasted_iota(jnp.int32, (LANES, LANES), 0)
           == lax.broadcasted_iota(jnp.int32, (LANES, LANES), 1))

    def per_token(t, idx_of):
        rows = _gathered_rows(tab_ref, idx_of)
        x16 = jnp.tile(xn_ref[t], (LANES // SUBLANES, 1)).astype(BF16)
        y = lax.dot_general(rows, x16, (((1,), (1,)), ((), ())),
                            preferred_element_type=F32)
        y = y.reshape(PEER_SLOTS * SUBLANES // LANES, LANES, LANES)
        rbuf[t] = jnp.sum(jnp.where(eye[None], y, 0.0), axis=1)

    _for_each_token(idx_hbm, idx_smem, sems, tt, per_token)
    r = rbuf[...].reshape(tt * SUBLANES, LANES)
    r_hi = r.astype(BF16)
    r_lo = (r - r_hi.astype(F32)).astype(BF16)
    h = (jnp.dot(r_hi, fold_ref[...], preferred_element_type=F32)
         + jnp.dot(r_lo, fold_ref[...], preferred_element_type=F32))
    out_ref[...] = gate_ref[...] * jax.nn.gelu(h)


def _peer_down_kernel(idx_hbm, gh_ref, res_ref, tab_ref, spread_ref, out_ref, idx_smem, ghx, sems):
    tt = gh_ref.shape[0]
    ghx[...] = jnp.dot(gh_ref[...].astype(BF16), spread_ref[...], preferred_element_type=F32)
    width = PEER_SLOTS * SUBLANES
    diag = (lax.broadcasted_iota(jnp.int32, (SUBLANES, width), 0)
            == lax.broadcasted_iota(jnp.int32, (SUBLANES, width), 1) % SUBLANES)

    def per_token(t, idx_of):
        rows = _gathered_rows(tab_ref, idx_of)
        row = jnp.broadcast_to(ghx[pl.ds(t, 1), :], (SUBLANES, width))
        coef = jnp.where(diag, row, 0.0).astype(BF16)
        out_ref[t] = res_ref[t] + jnp.dot(coef, rows, preferred_element_type=F32)

    _for_each_token(idx_hbm, idx_smem, sems, tt, per_token)


def _peer_table(tab):
    e, dm = tab.shape
    assert dm == SUBLANES * LANES
    return tab.astype(BF16).reshape(e, SUBLANES, LANES)


def _peer_ffn_residual(x2d, g_ffn, wq, subkeys, u_tab, v_tab):
    n, dm = x2d.shape
    tt = PEER_TT
    n_tiles = n // tt
    assert tt % (2 * PEER_GROUP) == 0
    xn, idx, gate = _peer_route(x2d, g_ffn, wq, subkeys)
    idx_tiles = idx.transpose(0, 3, 1, 2).reshape(n_tiles, tt // PEER_GROUP, PEER_GROUP * PEER_SLOTS)
    gate_rows = gate.transpose(0, 3, 1, 2).reshape(n * PEER_HEADS, PEER_TOPK)
    width = PEER_SLOTS * SUBLANES
    lane = np.arange(LANES)
    fold = jnp.asarray(lane[:, None] // SUBLANES == np.arange(PEER_TOPK)[None, :], BF16)
    spread = jnp.asarray(np.arange(PEER_SLOTS)[:, None] == np.arange(width)[None, :] // SUBLANES, BF16)
    params = pltpu.CompilerParams(dimension_semantics=("arbitrary",),
                                  vmem_limit_bytes=PEER_VMEM_LIMIT)
    table_spec = pl.BlockSpec(memory_space=pltpu.VMEM)
    idx_scratch = pltpu.SMEM((2, PEER_GROUP * PEER_SLOTS), jnp.int32)
    sems = pltpu.SemaphoreType.DMA((2,))
    row3 = (tt,) + ROW_TILE

    gh = pl.pallas_call(
        _peer_up_kernel,
        grid=(n_tiles,),
        in_specs=[pl.BlockSpec(memory_space=pl.ANY),
                  pl.BlockSpec(row3, lambda i: (i, 0, 0)),
                  pl.BlockSpec((tt * PEER_HEADS, PEER_TOPK), lambda i: (i, 0)),
                  table_spec,
                  pl.BlockSpec((LANES, PEER_TOPK), lambda i: (0, 0))],
        out_specs=pl.BlockSpec((tt * PEER_HEADS, PEER_TOPK), lambda i: (i, 0)),
        out_shape=jax.ShapeDtypeStruct((n * PEER_HEADS, PEER_TOPK), F32),
        scratch_shapes=[idx_scratch, pltpu.VMEM(row3, F32), sems],
        compiler_params=params,
        name="peer_up",
    )(idx_tiles,xn.reshape((n,) + ROW_TILE), gate_rows, _peer_table(u_tab), fold)

    out = pl.pallas_call(
        _peer_down_kernel,
        grid=(n_tiles,),
        in_specs=[pl.BlockSpec(memory_space=pl.ANY),
                  pl.BlockSpec((tt, PEER_SLOTS), lambda i: (i, 0)),
                  pl.BlockSpec(row3, lambda i: (i, 0, 0)),
                  table_spec,
                  pl.BlockSpec((PEER_SLOTS, width), lambda i: (0, 0))],
        out_specs=pl.BlockSpec(row3, lambda i: (i, 0, 0)),
        out_shape=jax.ShapeDtypeStruct((n,) + ROW_TILE, F32),
        scratch_shapes=[idx_scratch, pltpu.VMEM((tt, width), F32), sems],
        compiler_params=params,
        name="peer_down",
    )(idx_tiles,gh.reshape(n, PEER_SLOTS), x2d.reshape((n,) + ROW_TILE), _peer_table(v_tab), spread)
    return out.reshape(n, dm)


def _even_layer(x, pos, mix_norm, w_in, q_gain, k_gain, log_dt, a_re, a_im, b_re, b_im,
                c_re, c_im, d_skip, w_glu, b_glu, w_out):
    bsz, seq, dm = x.shape
    x2d = x.reshape(bsz * seq, dm)
    pieces = dict(zip(('q', 'k', 'v', 'iq', 'ik', 'iw', 'u'), _split_cols(w_in, EVEN_COLS)))
    w_lay = jnp.concatenate(
        [jnp.pad(pieces[name], ((0, 0), (0, EVEN_LAYOUT[name][1] - pieces[name].shape[1])))
         for name in sorted(EVEN_LAYOUT, key=lambda s: EVEN_LAYOUT[s][0])], axis=1)
    h = _norm_proj(x2d, mix_norm, w_lay)
    o_a = _dsa_mixer(h, bsz, seq, q_gain, k_gain)
    u0, uw = EVEN_LAYOUT['u']
    u = h[:, u0:u0 + uw].reshape(bsz, seq, uw)
    o_b = _s5_mixer(u, log_dt, a_re, a_im, b_re, b_im, c_re, c_im, d_skip, w_glu, b_glu)
    mix = jnp.concatenate([o_a, o_b.reshape(bsz * seq, uw)], axis=-1)
    return _proj_residual(mix, w_out, x2d).reshape(bsz, seq, dm)


def _odd_layer(x, pos, mix_norm, w_in, q_gain, k_gain, cmp_pos, k_w1, k_w2, v_w1, v_w2,
               w_gate, b_gate, o_gain, w_out):
    bsz, seq, dm = x.shape
    x2d = x.reshape(bsz * seq, dm)
    h = _norm_proj(x2d, mix_norm, w_in)
    (q, kc, vc, ks, vs, kw, vw, gl, gq, gk, gv, glow, gr) = _split_cols(h, ODD_COLS)
    o_c = _nsa_mixer(q, kc, vc, ks, vs, kw, vw, gl, bsz, seq, q_gain, k_gain,
                     cmp_pos, k_w1, k_w2, v_w1, v_w2)
    o_d = _gla_mixer(gq, gk, gv, glow, gr, bsz, seq, w_gate, b_gate, o_gain)
    mix = jnp.concatenate([o_c, o_d], axis=-1)
    return _proj_residual(mix, w_out, x2d).reshape(bsz, seq, dm)


def kernel(x, l0_mix_norm, l0_w_in, l0_a_q_gain, l0_a_k_gain, l0_s5_log_dt, l0_s5_a_re, l0_s5_a_im, l0_s5_b_re, l0_s5_b_im, l0_s5_c_re, l0_s5_c_im, l0_s5_d, l0_s5_w_glu, l0_s5_b_glu, l0_w_out, l0_ffn_norm, l0_peer_wq, l0_peer_subkeys, l0_peer_u, l0_peer_v, l1_mix_norm, l1_w_in, l1_c_q_gain, l1_c_k_gain, l1_nsa_cmp_pos, l1_nsa_k_w1, l1_nsa_k_w2, l1_nsa_v_w1, l1_nsa_v_w2, l1_gla_w_gate, l1_gla_b_gate, l1_gla_o_gain, l1_w_out, l1_ffn_norm, l1_peer_wq, l1_peer_subkeys, l1_peer_u, l1_peer_v):
    seq = x.shape[1]
    pos = jnp.arange(seq, dtype=jnp.int32)
    x = _even_layer(x, pos, l0_mix_norm, l0_w_in, l0_a_q_gain, l0_a_k_gain, l0_s5_log_dt,
                    l0_s5_a_re, l0_s5_a_im, l0_s5_b_re, l0_s5_b_im, l0_s5_c_re, l0_s5_c_im,
                    l0_s5_d, l0_s5_w_glu, l0_s5_b_glu, l0_w_out)
    shape = x.shape
    flat = lambda t: t.reshape(-1, shape[-1])
    x = _peer_ffn_residual(flat(x), l0_ffn_norm, l0_peer_wq, l0_peer_subkeys,
                           l0_peer_u, l0_peer_v).reshape(shape)
    x = _odd_layer(x, pos, l1_mix_norm, l1_w_in, l1_c_q_gain, l1_c_k_gain, l1_nsa_cmp_pos,
                   l1_nsa_k_w1, l1_nsa_k_w2, l1_nsa_v_w1, l1_nsa_v_w2, l1_gla_w_gate,
                   l1_gla_b_gate, l1_gla_o_gain, l1_w_out)
    return _peer_ffn_residual(flat(x), l1_ffn_norm, l1_peer_wq, l1_peer_subkeys,
                              l1_peer_u, l1_peer_v).reshape(shape)
```
